```python
import math
import jax
import jax.numpy as jnp
from jax import lax
import numpy as np


D_MODEL = 1024
BATCH = 8
SEQ = 2048
DEPTH = 4

GRID_W = 64
CTX_LEN = 256
N_MIXERS = 4
N_CONV_LAYERS = (DEPTH + 3) // N_MIXERS
N_SSM_LAYERS = (DEPTH + 2) // N_MIXERS
N_SWA_LAYERS = (DEPTH + 1) // N_MIXERS
N_DIFF_LAYERS = DEPTH // N_MIXERS
NORM_EPS = 1e-6
ROPE_BASE = 10000.0
ADA_CHUNKS = 6

CONV_WIDTH = 31

SSM_D_INNER = 2 * D_MODEL
SSM_HEAD_DIM = 64
SSM_HEADS = SSM_D_INNER // SSM_HEAD_DIM
SSM_GROUPS = 4
SSM_STATE = 128
SSM_CONV = 5
SSM_CHUNK = 128
SSM_BC_DIM = 2 * SSM_GROUPS * SSM_STATE
SSM_CONV_DIM = SSM_D_INNER + SSM_BC_DIM
SSM_IN_DIM = SSM_D_INNER + SSM_CONV_DIM + 2 * SSM_HEADS

SWA_HEAD_DIM = 64
SWA_HEADS = D_MODEL // SWA_HEAD_DIM
SWA_KV_HEADS = 4
SWA_GROUP = SWA_HEADS // SWA_KV_HEADS
SWA_WINDOW = 128
SWA_BLOCK = 128
SWA_QKV_DIM = (SWA_HEADS + 2 * SWA_KV_HEADS) * SWA_HEAD_DIM

DIFF_HEAD_DIM = 64
DIFF_HEADS = D_MODEL // (2 * DIFF_HEAD_DIM)
DIFF_V_DIM = 2 * DIFF_HEAD_DIM
DIFF_BLOCK = 128

N_EXPERTS = 32
TOP_K = 4
D_EXPERT = D_MODEL
SWIGLU_LIMIT = 7.0
SWIGLU_ALPHA = 1.702

kernel_name = 'hybrid_flow_trunk_conv_ssd_swa_diff_moe'


def rms_norm(x, g):
    xf = x.astype(jnp.float32)
    y = xf * lax.rsqrt(jnp.mean(xf * xf, axis=-1, keepdims=True) + NORM_EPS)
    return (y * g.astype(jnp.float32)).astype(x.dtype)


def layer_norm(x, g, b):
    xf = x.astype(jnp.float32)
    mu = jnp.mean(xf, axis=-1, keepdims=True)
    var = jnp.mean(jnp.square(xf - mu), axis=-1, keepdims=True)
    y = (xf - mu) * lax.rsqrt(var + NORM_EPS)
    return (y * g.astype(jnp.float32) + b.astype(jnp.float32)).astype(x.dtype)


def modulate(h, shift, scale):
    return h * (1 + scale[:, None]) + shift[:, None]


def axial_rope(rows, head_dim):
    t = jnp.arange(rows * GRID_W)
    row = (t // GRID_W).astype(jnp.float32)
    col = (t % GRID_W).astype(jnp.float32)
    quarter = head_dim // 4
    inv_freq = ROPE_BASE ** (-jnp.arange(quarter, dtype=jnp.float32) / quarter)
    ang = jnp.concatenate([row[:, None] * inv_freq, col[:, None] * inv_freq], axis=-1)
    return jnp.cos(ang), jnp.sin(ang)


def apply_rope(x, cos, sin):
    shape = (cos.shape[0],) + (1,) * (x.ndim - 3) + (cos.shape[1],)
    c, s = cos.reshape(shape), sin.reshape(shape)
    xf = x.astype(jnp.float32)
    x1, x2 = jnp.split(xf, 2, axis=-1)
    return jnp.concatenate([x1 * c - x2 * s, x2 * c + x1 * s], axis=-1).astype(x.dtype)


def depthwise_conv(x, w, b):
    width = w.shape[0]
    half = (width - 1) // 2
    y = lax.conv_general_dilated(
        x, w[:, None, :].astype(x.dtype), window_strides=(1,),
        padding=[(half, width - 1 - half)],
        dimension_numbers=('NWC', 'WIO', 'NWC'),
        feature_group_count=x.shape[-1])
    return y + b


def conformer_mixer(hc, hl, w_pw1, b_pw1, w_dw, b_dw, ln_g, ln_b, w_pw2, b_pw2, ctx_out):
    def run(h):
        a, g = jnp.split(h @ w_pw1 + b_pw1, 2, axis=-1)
        u = a * jax.nn.sigmoid(g)
        u = depthwise_conv(u, w_dw, b_dw)
        u = jax.nn.silu(layer_norm(u, ln_g, ln_b))
        return u @ w_pw2 + b_pw2
    return (run(hc) if ctx_out else None), run(hl)


def ssd_scan(x, dt, a, bm, cm, h0):
    bsz, n = x.shape[:2]
    L = SSM_CHUNK
    nc = n // L
    G, R = SSM_GROUPS, SSM_HEADS // SSM_GROUPS
    loga = (dt * a).reshape(bsz, nc, L, G, R)
    xdt = (x * dt[..., None].astype(x.dtype)).reshape(bsz, nc, L, G, R, SSM_HEAD_DIM)
    bm = bm.reshape(bsz, nc, L, G, SSM_STATE)
    cm = cm.reshape(bsz, nc, L, G, SSM_STATE)
    acum = jnp.cumsum(loga, axis=2)
    lower = jnp.tril(jnp.ones((L, L), dtype=bool))
    seg = acum[:, :, :, None] - acum[:, :, None, :]
    within = jnp.exp(jnp.where(lower[:, :, None, None], seg, -jnp.inf)).astype(x.dtype)
    cb = jnp.einsum('bclgn,bcsgn->bcgls', cm, bm)
    y_diag = jnp.einsum('bcgls,bclsgr,bcsgrp->bclgrp', cb, within, xdt)
    to_end = jnp.exp(acum[:, :, -1:] - acum).astype(x.dtype)
    states = jnp.einsum('bclgn,bclgr,bclgrp->bcgrpn', bm, to_end, xdt)
    chunk_decay = jnp.exp(acum[:, :, -1])

    def step(h, inp):
        s, d = inp
        return h * d[..., None, None] + s, h

    h_final, h_in = lax.scan(step, h0, (jnp.moveaxis(states, 1, 0), jnp.moveaxis(chunk_decay, 1, 0)))
    h_in = jnp.moveaxis(h_in, 0, 1)
    y_off = jnp.einsum('bclgn,bcgrpn,bclgr->bclgrp', cm, h_in.astype(x.dtype),
                       jnp.exp(acum).astype(x.dtype))
    y = (y_diag + y_off).reshape(bsz, n, SSM_HEADS, SSM_HEAD_DIM)
    return y.astype(x.dtype), h_final


def ssm_mixer(hc, hl, w_in, w_conv, b_conv, a_log, dt_bias, d_skip, norm_g, w_out, ctx_out):
    def project(h):
        bsz, n = h.shape[:2]
        z, xbc, dt = jnp.split(h @ w_in, [SSM_D_INNER, SSM_D_INNER + SSM_CONV_DIM], axis=-1)
        xbc = jax.nn.silu(depthwise_conv(xbc, w_conv, b_conv))
        xs, bm, cm = jnp.split(xbc, [SSM_D_INNER, SSM_D_INNER + SSM_GROUPS * SSM_STATE], axis=-1)
        xs = xs.reshape(bsz, n, SSM_HEADS, SSM_HEAD_DIM)
        bm = bm.reshape(bsz, n, SSM_GROUPS, SSM_STATE)
        cm = cm.reshape(bsz, n, SSM_GROUPS, SSM_STATE)
        dt = jax.nn.softplus(dt.astype(jnp.float32).reshape(bsz, n, 2, SSM_HEADS)
                             + dt_bias.astype(jnp.float32))
        return z, xs, bm, cm, dt

    zc, xc, bc, cc, dtc = project(hc)
    zl, xl, bl, cl, dtl = project(hl)
    decay = -jnp.exp(a_log.astype(jnp.float32))
    bsz = hl.shape[0]
    h0 = jnp.zeros((bsz, SSM_GROUPS, SSM_HEADS // SSM_GROUPS, SSM_HEAD_DIM, SSM_STATE), jnp.float32)

    def flip(t):
        return jnp.flip(t, axis=1)

    yc_f, sc_f = ssd_scan(xc, dtc[:, :, 0], decay[0], bc, cc, h0)
    yl_f, _ = ssd_scan(xl, dtl[:, :, 0], decay[0], bl, cl, sc_f)
    yc_b, sc_b = ssd_scan(flip(xc), flip(dtc[:, :, 1]), decay[1], flip(bc), flip(cc), h0)
    yl_b, _ = ssd_scan(flip(xl), flip(dtl[:, :, 1]), decay[1], flip(bl), flip(cl), sc_b)

    def finish(z, xs, y_f, y_b):
        y = y_f + y_b + xs * d_skip[:, None]
        y = y.reshape(y.shape[0], y.shape[1], SSM_D_INNER)
        return rms_norm(y * jax.nn.silu(z), norm_g) @ w_out

    yl = finish(zl, xl, yl_f, flip(yl_b))
    yc = finish(zc, xc, yc_f, flip(yc_b)) if ctx_out else None
    return yc, yl


def gqa_sink_attend(q, k, v, sink, mask):
    s = jnp.einsum('bqkgd,bskd->bkgqs', q, k).astype(jnp.float32) * (SWA_HEAD_DIM ** -0.5)
    if mask is not None:
        s = jnp.where(mask, s, -jnp.inf)
    sink_col = jnp.broadcast_to(sink[None, :, :, None, None], s.shape[:-1] + (1,))
    p = jax.nn.softmax(jnp.concatenate([s, sink_col], axis=-1), axis=-1)[..., :-1]
    return jnp.einsum('bkgqs,bskd->bqkgd', p.astype(v.dtype), v)


def swa_mixer(hc, hl, cos, sin, w_qkv, b_qkv, sinks, w_o, b_o, ctx_out):
    def project(h):
        bsz, n = h.shape[:2]
        q, k, v = jnp.split(h @ w_qkv + b_qkv,
                            [SWA_HEADS * SWA_HEAD_DIM, (SWA_HEADS + SWA_KV_HEADS) * SWA_HEAD_DIM], axis=-1)
        return (q.reshape(bsz, n, SWA_KV_HEADS, SWA_GROUP, SWA_HEAD_DIM),
                k.reshape(bsz, n, SWA_KV_HEADS, SWA_HEAD_DIM),
                v.reshape(bsz, n, SWA_KV_HEADS, SWA_HEAD_DIM))

    qc, kc, vc = project(hc)
    ql, kl, vl = project(hl)
    ql, kl = apply_rope(ql, cos, sin), apply_rope(kl, cos, sin)
    sink = sinks.astype(jnp.float32).reshape(SWA_KV_HEADS, SWA_GROUP)
    bsz, n = hl.shape[:2]
    n_blocks = n // SWA_BLOCK
    span = SWA_BLOCK + 2 * SWA_WINDOW
    pad = ((0, 0), (SWA_WINDOW, SWA_WINDOW), (0, 0), (0, 0))
    kp, vp = jnp.pad(kl, pad), jnp.pad(vl, pad)
    rel = jnp.arange(span) - SWA_WINDOW
    band = jnp.abs(jnp.arange(SWA_BLOCK)[:, None] - rel[None, :]) <= SWA_WINDOW
    ctx_mask = jnp.ones((SWA_BLOCK, kc.shape[1]), dtype=bool)

    def block(args):
        q_blk, start = args
        k_win = lax.dynamic_slice_in_dim(kp, start, span, axis=1)
        v_win = lax.dynamic_slice_in_dim(vp, start, span, axis=1)
        inside = (start + rel >= 0) & (start + rel < n)
        mask = jnp.concatenate([band & inside[None, :], ctx_mask], axis=1)
        return gqa_sink_attend(q_blk, jnp.concatenate([k_win, kc], axis=1),
                               jnp.concatenate([v_win, vc], axis=1), sink, mask)

    q_blocks = jnp.swapaxes(
        ql.reshape(bsz, n_blocks, SWA_BLOCK, SWA_KV_HEADS, SWA_GROUP, SWA_HEAD_DIM), 0, 1)
    starts = jnp.arange(n_blocks) * SWA_BLOCK
    ol = jnp.swapaxes(lax.map(block, (q_blocks, starts)), 0, 1).reshape(bsz, n, SWA_HEADS * SWA_HEAD_DIM)
    yl = ol @ w_o + b_o
    yc = None
    if ctx_out:
        oc = gqa_sink_attend(qc, kc, vc, sink, None).reshape(bsz, qc.shape[1], SWA_HEADS * SWA_HEAD_DIM)
        yc = oc @ w_o + b_o
    return yc, yl


def diff_attend(q, k, v, lam, subln_g, lambda_init):
    s = jnp.einsum('bqhtd,bkhtd->bhtqk', q, k).astype(jnp.float32) * (DIFF_HEAD_DIM ** -0.5)
    p = jax.nn.softmax(s, axis=-1)
    a = p[:, :, 0] - lam * p[:, :, 1]
    o = jnp.einsum('bhqk,bkhe->bqhe', a.astype(v.dtype), v)
    return rms_norm(o, subln_g) * (1.0 - lambda_init)


def diff_mixer(hc, hl, cos, sin, w_qkv, lq1, lk1, lq2, lk2, subln_g, w_o, lambda_init, ctx_out):
    def project(h):
        bsz, n = h.shape[:2]
        q, k, v = jnp.split(h @ w_qkv, 3, axis=-1)
        return (q.reshape(bsz, n, DIFF_HEADS, 2, DIFF_HEAD_DIM),
                k.reshape(bsz, n, DIFF_HEADS, 2, DIFF_HEAD_DIM),
                v.reshape(bsz, n, DIFF_HEADS, DIFF_V_DIM))

    qc, kc, vc = project(hc)
    ql, kl, vl = project(hl)
    ql, kl = apply_rope(ql, cos, sin), apply_rope(kl, cos, sin)
    f32 = jnp.float32
    lam = (jnp.exp(jnp.sum(lq1.astype(f32) * lk1.astype(f32)))
           - jnp.exp(jnp.sum(lq2.astype(f32) * lk2.astype(f32))) + lambda_init)
    bsz, n = hl.shape[:2]
    k_all = jnp.concatenate([kl, kc], axis=1)
    v_all = jnp.concatenate([vl, vc], axis=1)
    q_blocks = jnp.swapaxes(
        ql.reshape(bsz, n // DIFF_BLOCK, DIFF_BLOCK, DIFF_HEADS, 2, DIFF_HEAD_DIM), 0, 1)
    ol = lax.map(lambda qb: diff_attend(qb, k_all, v_all, lam, subln_g, lambda_init), q_blocks)
    ol = jnp.swapaxes(ol, 0, 1).reshape(bsz, n, DIFF_HEADS * DIFF_V_DIM)
    yl = ol @ w_o
    yc = None
    if ctx_out:
        oc = diff_attend(qc, kc, vc, lam, subln_g, lambda_init)
        yc = oc.reshape(bsz, qc.shape[1], DIFF_HEADS * DIFF_V_DIM) @ w_o
    return yc, yl


def moe_ffn(h, w_router, b_router, w_gu, b_gu, w_down, b_down):
    shape = h.shape
    t = h.reshape(-1, shape[-1])
    logits = (t @ w_router + b_router).astype(jnp.float32)
    top_logit, top_e = lax.top_k(logits, TOP_K)
    gate = jax.nn.softmax(top_logit, axis=-1)
    flat_e = top_e.reshape(-1)
    order = jnp.argsort(flat_e)
    e_sorted = flat_e[order]
    tok = order // TOP_K
    xs = t[tok]
    sizes = jnp.bincount(flat_e, length=N_EXPERTS).astype(jnp.int32)
    gu = lax.ragged_dot(xs, w_gu, sizes) + b_gu[e_sorted]
    glu, lin = jnp.split(gu, 2, axis=-1)
    glu = jnp.minimum(glu, SWIGLU_LIMIT)
    lin = jnp.clip(lin, -SWIGLU_LIMIT, SWIGLU_LIMIT)
    act = glu * jax.nn.sigmoid(SWIGLU_ALPHA * glu) * (lin + 1)
    out = lax.ragged_dot(act, w_down, sizes) + b_down[e_sorted]
    out = out * gate.reshape(-1)[order][:, None].astype(out.dtype)
    y = jax.ops.segment_sum(out, tok, num_segments=t.shape[0])
    return y.reshape(shape)


def setup_inputs(seed: int = 0) -> dict:
    key = jax.random.key(seed)
    keys = iter(jax.random.split(key, 64))
    f32 = jnp.float32

    def normal(shape, scale):
        return jax.random.normal(next(keys), shape, f32) * scale

    def gain(shape):
        return 1.0 + normal(shape, 0.02)

    a_log = jnp.log(jax.random.uniform(next(keys), (N_SSM_LAYERS, 2, SSM_HEADS), f32, 1.0, 16.0))
    dt0 = jnp.exp(jax.random.uniform(next(keys), (N_SSM_LAYERS, 2, SSM_HEADS), f32,
                                     math.log(1e-3), math.log(1e-1)))
    dt_bias = dt0 + jnp.log(-jnp.expm1(-dt0))
    d = D_MODEL
    return {
        'x': normal((BATCH, SEQ, d), 1.0),
        'c': normal((BATCH, d), 1.0),
        'ctx': normal((BATCH, CTX_LEN, d), 1.0),
        'c_ctx': normal((d,), 1.0),
        'ada_w': normal((DEPTH, d, ADA_CHUNKS * d), 0.5 * d ** -0.5),
        'ada_b': normal((DEPTH, ADA_CHUNKS * d), 0.02),
        'g_mix': gain((DEPTH, d)),
        'g_ffn': gain((DEPTH, d)),
        'g_final': gain((d,)),
        'conv_w_pw1': normal((N_CONV_LAYERS, d, 2 * d), d ** -0.5),
        'conv_b_pw1': normal((N_CONV_LAYERS, 2 * d), 0.02),
        'conv_w_dw': normal((N_CONV_LAYERS, CONV_WIDTH, d), CONV_WIDTH ** -0.5),
        'conv_b_dw': normal((N_CONV_LAYERS, d), 0.02),
        'conv_ln_g': gain((N_CONV_LAYERS, d)),
        'conv_ln_b': normal((N_CONV_LAYERS, d), 0.02),
        'conv_w_pw2': normal((N_CONV_LAYERS, d, d), d ** -0.5),
        'conv_b_pw2': normal((N_CONV_LAYERS, d), 0.02),
        'ssm_w_in': normal((N_SSM_LAYERS, d, SSM_IN_DIM), d ** -0.5),
        'ssm_w_conv': normal((N_SSM_LAYERS, SSM_CONV, SSM_CONV_DIM), SSM_CONV ** -0.5),
        'ssm_b_conv': normal((N_SSM_LAYERS, SSM_CONV_DIM), 0.02),
        'ssm_a_log': a_log,
        'ssm_dt_bias': dt_bias,
        'ssm_d': 1.0 + normal((N_SSM_LAYERS, SSM_HEADS), 0.1),
        'ssm_norm_g': gain((N_SSM_LAYERS, SSM_D_INNER)),
        'ssm_w_out': normal((N_SSM_LAYERS, SSM_D_INNER, d), SSM_D_INNER ** -0.5),
        'swa_w_qkv': normal((N_SWA_LAYERS, d, SWA_QKV_DIM), d ** -0.5),
        'swa_b_qkv': normal((N_SWA_LAYERS, SWA_QKV_DIM), 0.02),
        'swa_sinks': normal((N_SWA_LAYERS, SWA_HEADS), 0.5),
        'swa_w_o': normal((N_SWA_LAYERS, SWA_HEADS * SWA_HEAD_DIM, d), (SWA_HEADS * SWA_HEAD_DIM) ** -0.5),
        'swa_b_o': normal((N_SWA_LAYERS, d), 0.02),
        'diff_w_qkv': normal((N_DIFF_LAYERS, d, 3 * d), d ** -0.5),
        'diff_lambda_q1': normal((N_DIFF_LAYERS, DIFF_HEAD_DIM), 0.1),
        'diff_lambda_k1': normal((N_DIFF_LAYERS, DIFF_HEAD_DIM), 0.1),
        'diff_lambda_q2': normal((N_DIFF_LAYERS, DIFF_HEAD_DIM), 0.1),
        'diff_lambda_k2': normal((N_DIFF_LAYERS, DIFF_HEAD_DIM), 0.1),
        'diff_subln_g': gain((N_DIFF_LAYERS, DIFF_V_DIM)),
        'diff_w_o': normal((N_DIFF_LAYERS, DIFF_HEADS * DIFF_V_DIM, d), (DIFF_HEADS * DIFF_V_DIM) ** -0.5),
        'moe_w_router': normal((DEPTH, d, N_EXPERTS), d ** -0.5),
        'moe_b_router': normal((DEPTH, N_EXPERTS), 0.01),
        'moe_w_gu': normal((DEPTH, N_EXPERTS, d, 2 * D_EXPERT), d ** -0.5),
        'moe_b_gu': normal((DEPTH, N_EXPERTS, 2 * D_EXPERT), 0.02),
        'moe_w_down': normal((DEPTH, N_EXPERTS, D_EXPERT, d), D_EXPERT ** -0.5),
        'moe_b_down': normal((DEPTH, N_EXPERTS, d), 0.02),
    }


def reference(x, c, ctx, c_ctx, ada_w, ada_b, g_mix, g_ffn, g_final,
              conv_w_pw1, conv_b_pw1, conv_w_dw, conv_b_dw, conv_ln_g, conv_ln_b, conv_w_pw2, conv_b_pw2,
              ssm_w_in, ssm_w_conv, ssm_b_conv, ssm_a_log, ssm_dt_bias, ssm_d, ssm_norm_g, ssm_w_out,
              swa_w_qkv, swa_b_qkv, swa_sinks, swa_w_o, swa_b_o,
              diff_w_qkv, diff_lambda_q1, diff_lambda_k1, diff_lambda_q2, diff_lambda_k2, diff_subln_g, diff_w_o,
              moe_w_router, moe_b_router, moe_w_gu, moe_b_gu, moe_w_down, moe_b_down):
    rows = x.shape[1] // GRID_W
    cos, sin = axial_rope(rows, SWA_HEAD_DIM)
    cond_lat = jax.nn.silu(c)
    cond_ctx = jax.nn.silu(c_ctx)[None]
    xl, xc = x, ctx
    n_ctx = ctx.shape[1]
    for i in range(DEPTH):
        kind, j = i % N_MIXERS, i // N_MIXERS
        ctx_out = i < DEPTH - 1
        ml = jnp.split(cond_lat @ ada_w[i] + ada_b[i], ADA_CHUNKS, axis=-1)
        mc = jnp.split(cond_ctx @ ada_w[i] + ada_b[i], ADA_CHUNKS, axis=-1)
        hl = modulate(rms_norm(xl, g_mix[i]), ml[0], ml[1])
        hc = modulate(rms_norm(xc, g_mix[i]), mc[0], mc[1])
        if kind == 0:
            yc, yl = conformer_mixer(hc, hl, conv_w_pw1[j], conv_b_pw1[j], conv_w_dw[j], conv_b_dw[j],
                                     conv_ln_g[j], conv_ln_b[j], conv_w_pw2[j], conv_b_pw2[j], ctx_out)
        elif kind == 1:
            yc, yl = ssm_mixer(hc, hl, ssm_w_in[j], ssm_w_conv[j], ssm_b_conv[j], ssm_a_log[j],
                               ssm_dt_bias[j], ssm_d[j], ssm_norm_g[j], ssm_w_out[j], ctx_out)
        elif kind == 2:
            yc, yl = swa_mixer(hc, hl, cos, sin, swa_w_qkv[j], swa_b_qkv[j], swa_sinks[j],
                               swa_w_o[j], swa_b_o[j], ctx_out)
        else:
            lambda_init = 0.8 - 0.6 * math.exp(-0.3 * i)
            yc, yl = diff_mixer(hc, hl, cos, sin, diff_w_qkv[j], diff_lambda_q1[j], diff_lambda_k1[j],
                                diff_lambda_q2[j], diff_lambda_k2[j], diff_subln_g[j], diff_w_o[j],
                                lambda_init, ctx_out)
        xl = xl + ml[2][:, None] * yl
        hl = modulate(rms_norm(xl, g_ffn[i]), ml[3], ml[4])
        if ctx_out:
            xc = xc + mc[2][:, None] * yc
            hc = modulate(rms_norm(xc, g_ffn[i]), mc[3], mc[4])
            f = moe_ffn(jnp.concatenate([hc, hl], axis=1), moe_w_router[i], moe_b_router[i],
                        moe_w_gu[i], moe_b_gu[i], moe_w_down[i], moe_b_down[i])
            xc = xc + mc[5][:, None] * f[:, :n_ctx]
            xl = xl + ml[5][:, None] * f[:, n_ctx:]
        else:
            f = moe_ffn(hl, moe_w_router[i], moe_b_router[i], moe_w_gu[i], moe_b_gu[i],
                        moe_w_down[i], moe_b_down[i])
            xl = xl + ml[5][:, None] * f
    return rms_norm(xl, g_final)
```

```python
import functools
import math

import jax
import jax.numpy as jnp
from jax import lax
from jax.experimental import pallas as pl
from jax.experimental.pallas import tpu as pltpu

F32 = jnp.float32
I32 = jnp.int32
MXU_DTYPE = jnp.bfloat16

LANES = 128
VMEM_LIMIT_BYTES = 56 * 1024 * 1024

TILE = 256
CHUNK = 128
HALO = 16
HEAD = 64
NORM_EPS = 1e-6
ROPE_BASE = 10000.0
GRID_W = 64
ADA_CHUNKS = 6
SSM_GROUPS = 4
SSM_STATE = 128
SWA_KV_HEADS = 4
SWA_WINDOW = 128
TOP_K = 4
SWIGLU_LIMIT = 7.0
SWIGLU_ALPHA = 1.702
EXPERT_TILE = 256
DIFF_TQ = 256


def _mx(v):
    return v.astype(MXU_DTYPE)


def _dot(a, b):
    return jnp.dot(a, b, preferred_element_type=F32)


def _dot_nt(a, b):
    return lax.dot_general(a, b, (((1,), (1,)), ((), ())), preferred_element_type=F32)


def _split2(v):
    hi = _mx(v)
    return hi, _mx(v - hi.astype(F32))


def _split3(v):
    hi = _mx(v)
    r = v - hi.astype(F32)
    mid = _mx(r)
    return hi, mid, _mx(r - mid.astype(F32))


def _dot_exact_rhs(parts, m):
    acc = _dot(parts[0], m)
    for p in parts[1:]:
        acc = acc + _dot(p, m)
    return acc


def _sigmoid(v):
    return 1.0 / (1.0 + jnp.exp(-v))


def _softplus(v):
    return jnp.maximum(v, 0.0) + jnp.log(1.0 + jnp.exp(-jnp.abs(v)))


def _rms(v):
    return v * lax.rsqrt(jnp.mean(v * v, axis=-1, keepdims=True) + NORM_EPS)


def _params(*sem):
    return pltpu.CompilerParams(dimension_semantics=sem, vmem_limit_bytes=VMEM_LIMIT_BYTES)


class _Geo:
    def __init__(self, batch, n_ctx, n_lat):
        assert n_ctx % TILE == 0 and n_lat % TILE == 0 and n_lat % n_ctx == 0
        self.batch, self.n_ctx, self.n_lat = batch, n_ctx, n_lat
        self.rpb = n_ctx + n_lat
        self.tpb = self.rpb // TILE
        self.nct = n_ctx // TILE
        self.nlt = n_lat // TILE
        self.cpb = self.rpb // CHUNK
        self.ncc = n_ctx // CHUNK
        self.nlc = n_lat // CHUNK
        self.rows = batch * self.rpb

    def mod_all(self, i):
        return (i // self.tpb) * 2 + (i % self.tpb >= self.nct).astype(I32)

    def lat_tile(self, i):
        return (i // self.nlt) * self.tpb + self.nct + i % self.nlt

    def mod_lat(self, i):
        return (i // self.nlt) * 2 + 1


def _adaln_kernel(c_ref, w_ref, b_ref, o_ref):
    c = c_ref[...]
    s_hi, s_lo = _split2(c * _sigmoid(c))
    w_hi, w_lo = _split2(w_ref[...])
    o_ref[...] = _dot(s_hi, w_hi) + _dot(s_lo, w_hi) + _dot(s_hi, w_lo) + b_ref[...]


def _adaln(cond, ada_w, ada_b):
    depth, d, n = ada_w.shape
    rows = cond.shape[0]
    return pl.pallas_call(
        _adaln_kernel,
        grid=(depth, n // d),
        in_specs=[pl.BlockSpec((rows, d), lambda l, j: (0, 0)),
                  pl.BlockSpec((None, d, d), lambda l, j: (l, 0, j)),
                  pl.BlockSpec((None, 1, d), lambda l, j: (l, 0, j))],
        out_specs=pl.BlockSpec((None, rows, d), lambda l, j: (l, 0, j)),
        out_shape=jax.ShapeDtypeStruct((depth, rows, n), F32),
        compiler_params=_params("arbitrary", "arbitrary"),
        name="adaln",
    )(cond, ada_w, ada_b.reshape(depth, 1, n))


def _norm_mod(x, g, mod, shift_row, scale_row):
    return _rms(x) * g * (1.0 + mod[scale_row:scale_row + 1, :]) + mod[shift_row:shift_row + 1, :]


def _lane_tile(tab, n):
    return jnp.tile(tab, (1, n // LANES))


def _norm_proj_kernel(mode, x_ref, g_ref, mod_ref, *refs):
    o_ref = refs[-1]
    n = o_ref.shape[-1]
    h = _norm_mod(x_ref[...], g_ref[...], mod_ref[...], 0, 1)
    if mode == "precise":
        w_hi, w_lo, b = refs[:3]
        h_hi, h_lo = _split2(h)
        acc = _dot(h_hi, w_hi[...]) + _dot(h_lo, w_hi[...]) + _dot(h_hi, w_lo[...]) + b[...]
    else:
        hb = _mx(h)
        if mode == "plain":
            w, b = refs[:2]
            acc = _dot(hb, w[...]) + b[...]
        elif mode == "glu":
            w1, b1, w2, b2 = refs[:4]
            acc = (_dot(hb, w1[...]) + b1[...]) * _sigmoid(_dot(hb, w2[...]) + b2[...])
        elif mode == "tab1":
            w1, b1, t1 = refs[:3]
            acc = (_dot(hb, w1[...]) + b1[...]) * _lane_tile(t1[...], n)
        else:
            w1, b1, t1, w2, b2, t2 = refs[:6]
            acc = ((_dot(hb, w1[...]) + b1[...]) * _lane_tile(t1[...], n)
                   + (_dot(hb, w2[...]) + b2[...]) * _lane_tile(t2[...], n))
    o_ref[...] = acc.astype(o_ref.dtype)


_PROJ_OPERANDS = {"plain": "wb", "precise": "wwb", "glu": "wbwb", "tab1": "wbt", "tab2": "wbtwbt"}


def _norm_proj(x, gain, modtab, geo, mode, operands, tn, out_dtype, name):
    rows, d = x.shape
    kinds = _PROJ_OPERANDS[mode]
    n = operands[0].shape[1]
    specs = [pl.BlockSpec((TILE, d), lambda j, i: (i, 0)),
             pl.BlockSpec((1, d), lambda j, i: (0, 0)),
             pl.BlockSpec((None, ADA_CHUNKS, d), lambda j, i: (geo.mod_all(i), 0, 0))]
    for kind in kinds:
        if kind == "w":
            specs.append(pl.BlockSpec((d, tn), lambda j, i: (0, j)))
        elif kind == "b":
            specs.append(pl.BlockSpec((1, tn), lambda j, i: (0, j)))
        else:
            specs.append(pl.BlockSpec((TILE, LANES), lambda j, i: (i % geo.tpb, 0)))
    return pl.pallas_call(
        functools.partial(_norm_proj_kernel, mode),
        grid=(n // tn, rows // TILE),
        in_specs=specs,
        out_specs=pl.BlockSpec((TILE, tn), lambda j, i: (i, j)),
        out_shape=jax.ShapeDtypeStruct((rows, n), out_dtype),
        compiler_params=_params("arbitrary", "arbitrary"),
        name=name,
    )(x, gain, modtab, *operands)


def _dwconv_kernel(width, mode, tpb, nct, cur_ref, prev_ref, next_ref, w_ref, b_ref, *refs):
    pad_ref = refs[-1]
    o_ref = refs[-2]
    p = pl.program_id(0) % tpb
    has_prev = jnp.logical_and(p != 0, p != nct)
    has_next = jnp.logical_and(p != nct - 1, p != tpb - 1)
    pad_ref[0:HALO, :] = jnp.where(has_prev, prev_ref[...], 0.0)
    pad_ref[HALO:HALO + TILE, :] = cur_ref[...]
    pad_ref[HALO + TILE:, :] = jnp.where(has_next, next_ref[...], 0.0)
    half = (width - 1) // 2
    acc = jnp.broadcast_to(b_ref[...], o_ref.shape)
    for k in range(width):
        off = HALO - half + k
        acc = acc + w_ref[k:k + 1, :] * pad_ref[off:off + TILE, :]
    if mode == "ln_silu":
        g_ref, beta_ref = refs[:2]
        mu = jnp.mean(acc, axis=-1, keepdims=True)
        cen = acc - mu
        acc = cen * lax.rsqrt(jnp.mean(cen * cen, axis=-1, keepdims=True) + NORM_EPS) * g_ref[...] + beta_ref[...]
    o_ref[...] = (acc * _sigmoid(acc)).astype(o_ref.dtype)


def _dwconv(u, w, b, geo, mode, extra, tc, out_dtype, name):
    rows, c = u.shape
    width = w.shape[0]
    per = TILE // HALO
    last = rows // HALO - 1
    specs = [pl.BlockSpec((TILE, tc), lambda i, j: (i, j)),
             pl.BlockSpec((HALO, tc), lambda i, j: (jnp.maximum(i * per - 1, 0), j)),
             pl.BlockSpec((HALO, tc), lambda i, j: (jnp.minimum((i + 1) * per, last), j)),
             pl.BlockSpec((width, tc), lambda i, j: (0, j)),
             pl.BlockSpec((1, tc), lambda i, j: (0, j))]
    specs += [pl.BlockSpec((1, tc), lambda i, j: (0, j)) for _ in extra]
    return pl.pallas_call(
        functools.partial(_dwconv_kernel, width, mode, geo.tpb, geo.nct),
        grid=(rows // TILE, c // tc),
        in_specs=specs,
        out_specs=pl.BlockSpec((TILE, tc), lambda i, j: (i, j)),
        out_shape=jax.ShapeDtypeStruct((rows, c), out_dtype),
        scratch_shapes=[pltpu.VMEM((TILE + 2 * HALO, tc), F32)],
        compiler_params=_params("arbitrary", "arbitrary"),
        name=name,
    )(u, u, u, w, b.reshape(1, c), *extra)


def _ssd_kernel(groups, heads_per_group, x_ref, b_ref, c_ref, dt_ref, bias_ref, alog_ref, e_ref, y_ref, h_ref):
    direction = pl.program_id(0)
    step = pl.program_id(2)
    n_state = SSM_STATE
    rp = heads_per_group * HEAD

    @pl.when(step == 0)
    def _():
        h_ref[...] = jnp.zeros_like(h_ref)

    fwd = direction == 0
    row = lax.broadcasted_iota(I32, (CHUNK, CHUNK), 0)
    col = lax.broadcasted_iota(I32, (CHUNK, CHUNK), 1)
    tri = (row - col) * jnp.where(fwd, 1, -1) >= 0
    dtv = _softplus(dt_ref[...] + bias_ref[...])
    a = dtv * (-jnp.exp(alog_ref[...]))
    tri_m = _mx(jnp.where(tri, 1.0, 0.0))
    a3 = _split3(a)
    acum = _dot(tri_m, a3[0]) + _dot(tri_m, a3[1]) + _dot(tri_m, a3[2])
    acum_t = acum.T
    expand = e_ref[...]
    dt_x = _dot_exact_rhs(_split3(dtv), expand)
    ac_x = _dot_exact_rhs(_split3(acum), expand)
    tot_x = jnp.where(fwd, ac_x[CHUNK - 1:CHUNK, :], ac_x[0:1, :])
    xdt = x_ref[...] * dt_x
    xdt_b = _mx(xdt)
    xdt_end_b = _mx(xdt * jnp.exp(tot_x - ac_x))
    e_ac = jnp.exp(ac_x)
    decay = jnp.exp(tot_x)
    lo_half = lax.broadcasted_iota(I32, (CHUNK, LANES), 1) < HEAD
    for g in range(groups):
        bg = b_ref[:, g * n_state:(g + 1) * n_state]
        cb_g = _mx(c_ref[:, g * n_state:(g + 1) * n_state])
        cb = _dot_nt(cb_g, _mx(bg))
        bg_t = _mx(bg.T)
        h_t = h_ref[g]
        cols = slice(g * rp, (g + 1) * rp)
        y_off = _dot(cb_g, _mx(h_t)) * e_ac[:, cols]
        blocks = []
        for pair in range(heads_per_group // 2):
            xb = xdt_b[:, g * rp + pair * LANES:g * rp + (pair + 1) * LANES]
            halves = []
            for hh in range(2):
                c = g * heads_per_group + 2 * pair + hh
                seg = acum[:, c:c + 1] - acum_t[c:c + 1, :]
                within = jnp.exp(jnp.where(tri, seg, -jnp.inf))
                halves.append(_dot(_mx(cb * within), xb))
            blocks.append(jnp.where(lo_half, halves[0], halves[1]))
        y_ref[:, cols] = jnp.concatenate(blocks, axis=1) + y_off
        h_ref[g] = h_t * decay[:, cols] + _dot(bg_t, xdt_end_b[:, cols])


def _ssd(xbc, dt_raw, dt_bias, a_log, geo, d_inner, heads):
    rows = xbc.shape[0]
    groups = SSM_GROUPS
    hpg = heads // groups
    gn = groups * SSM_STATE
    assert d_inner % gn == 0 and hpg % 2 == 0 and heads <= LANES
    cpb, ncc = geo.cpb, geo.ncc

    def rb(d, b, s):
        back = jnp.where(s < ncc, ncc - 1 - s, cpb + ncc - 1 - s)
        return b * cpb + jnp.where(d == 0, s, back)

    pad = LANES - heads
    bias = jnp.pad(dt_bias.astype(F32), ((0, 0), (0, pad))).reshape(2, 1, LANES)
    alog = jnp.pad(a_log.astype(F32), ((0, 0), (0, pad))).reshape(2, 1, LANES)
    expand = (jnp.arange(LANES)[:, None] == jnp.arange(d_inner)[None, :] // HEAD).astype(MXU_DTYPE)
    return pl.pallas_call(
        functools.partial(_ssd_kernel, groups, hpg),
        grid=(2, geo.batch, cpb),
        in_specs=[pl.BlockSpec((CHUNK, d_inner), lambda d, b, s: (rb(d, b, s), 0)),
                  pl.BlockSpec((CHUNK, gn), lambda d, b, s: (rb(d, b, s), d_inner // gn)),
                  pl.BlockSpec((CHUNK, gn), lambda d, b, s: (rb(d, b, s), d_inner // gn + 1)),
                  pl.BlockSpec((CHUNK, LANES), lambda d, b, s: (rb(d, b, s), d)),
                  pl.BlockSpec((None, 1, LANES), lambda d, b, s: (d, 0, 0)),
                  pl.BlockSpec((None, 1, LANES), lambda d, b, s: (d, 0, 0)),
                  pl.BlockSpec((LANES, d_inner), lambda d, b, s: (0, 0))],
        out_specs=pl.BlockSpec((None, CHUNK, d_inner), lambda d, b, s: (d, rb(d, b, s), 0)),
        out_shape=jax.ShapeDtypeStruct((2, rows, d_inner), F32),
        scratch_shapes=[pltpu.VMEM((groups, SSM_STATE, hpg * HEAD), F32)],
        compiler_params=_params("arbitrary", "arbitrary", "arbitrary"),
        name="ssd_scan",
    )(xbc, xbc, xbc, dt_raw, bias, alog, expand)


def _ssm_finish_kernel(y_ref, xs_ref, z_ref, dskip_ref, g_ref, o_ref):
    y = y_ref[0] + y_ref[1] + xs_ref[...] * dskip_ref[...]
    z = z_ref[...]
    o_ref[...] = (_rms(y * (z * _sigmoid(z))) * g_ref[...]).astype(o_ref.dtype)


def _ssm_finish(y2, xbc, z, d_skip_cols, norm_g):
    rows, di = z.shape
    return pl.pallas_call(
        _ssm_finish_kernel,
        grid=(rows // TILE,),
        in_specs=[pl.BlockSpec((2, TILE, di), lambda i: (0, i, 0)),
                  pl.BlockSpec((TILE, di), lambda i: (i, 0)),
                  pl.BlockSpec((TILE, di), lambda i: (i, 0)),
                  pl.BlockSpec((1, di), lambda i: (0, 0)),
                  pl.BlockSpec((1, di), lambda i: (0, 0))],
        out_specs=pl.BlockSpec((TILE, di), lambda i: (i, 0)),
        out_shape=jax.ShapeDtypeStruct((rows, di), MXU_DTYPE),
        compiler_params=_params("arbitrary"),
        name="ssm_finish",
    )(y2, xbc, z, d_skip_cols, norm_g)


def _swa_kernel(n_heads, ncc, n_lat, n_ctx, sink_ref, q_ref, kp_ref, kc_ref, kn_ref, kx_ref,
                vp_ref, vc_ref, vn_ref, vx_ref, o_ref):
    j = pl.program_id(1)
    is_ctx = j < ncc
    start = (j - ncc) * CHUNK
    span = 3 * CHUNK
    qi = lax.broadcasted_iota(I32, (CHUNK, span), 0)
    rel = lax.broadcasted_iota(I32, (CHUNK, span), 1) - SWA_WINDOW
    kpos = start + rel
    ninf = -jnp.inf
    band = jnp.where(jnp.abs(qi - rel) <= SWA_WINDOW,
                     jnp.where(kpos >= 0, jnp.where(kpos < n_lat, 0.0, ninf), ninf), ninf)
    band = jnp.where(is_ctx, ninf, band)
    bias = jnp.concatenate([band, jnp.zeros((CHUNK, n_ctx), F32)], axis=1)
    group = n_heads // SWA_KV_HEADS
    outs = []
    for kh in range(SWA_KV_HEADS):
        lanes = slice(kh * LANES, (kh + 1) * LANES)
        k_all = jnp.concatenate([kp_ref[:, lanes], kc_ref[:, lanes], kn_ref[:, lanes], kx_ref[:, lanes]], axis=0)
        v_all = jnp.concatenate([vp_ref[:, lanes], vc_ref[:, lanes], vn_ref[:, lanes], vx_ref[:, lanes]], axis=0)
        for g in range(group):
            h = kh * group + g
            s = _dot_nt(q_ref[:, h * LANES:(h + 1) * LANES], k_all) + bias
            sink = sink_ref[h]
            m = jnp.maximum(jnp.max(s, axis=-1, keepdims=True), sink)
            e = jnp.exp(s - m)
            p = e / (jnp.sum(e, axis=-1, keepdims=True) + jnp.exp(sink - m))
            outs.append(_dot(_mx(p), v_all))
    for c in range(n_heads // 2):
        o_ref[:, c * LANES:(c + 1) * LANES] = (outs[2 * c] + pltpu.roll(outs[2 * c + 1], HEAD, 1)).astype(o_ref.dtype)


def _swa_attention(uq, kd, vp, sinks, geo, n_heads):
    rows = uq.shape[0]
    cpb, ncc, nlc = geo.cpb, geo.ncc, geo.nlc
    kvw = SWA_KV_HEADS * LANES
    ctx_per = geo.rpb // geo.n_ctx

    def lat_block(shift):
        def index(b, j):
            jl = jnp.clip(j - ncc + shift, 0, nlc - 1)
            return (b * cpb + ncc + jl, 0)
        return pl.BlockSpec((CHUNK, kvw), index)

    ctx_spec = pl.BlockSpec((geo.n_ctx, kvw), lambda b, j: (b * ctx_per, 0))
    kv_specs = [lat_block(-1), lat_block(0), lat_block(1), ctx_spec]
    return pl.pallas_call(
        functools.partial(_swa_kernel, n_heads, ncc, geo.n_lat, geo.n_ctx),
        grid=(geo.batch, cpb),
        in_specs=[pl.BlockSpec(memory_space=pltpu.SMEM),
                  pl.BlockSpec((CHUNK, n_heads * LANES), lambda b, j: (b * cpb + j, 0))] + kv_specs + kv_specs,
        out_specs=pl.BlockSpec((CHUNK, n_heads * HEAD), lambda b, j: (b * cpb + j, 0)),
        out_shape=jax.ShapeDtypeStruct((rows, n_heads * HEAD), MXU_DTYPE),
        compiler_params=_params("arbitrary", "arbitrary"),
        name="swa_attention",
    )(sinks, uq, kd, kd, kd, kd, vp, vp, vp, vp)


def _diff_kernel(lambda_init, q_ref, k_ref, v_ref, lam_ref, g_ref, o_ref):
    lp = lam_ref[...]
    lam = (jnp.exp(jnp.sum(lp[0:1] * lp[1:2], axis=-1, keepdims=True))
           - jnp.exp(jnp.sum(lp[2:3] * lp[3:4], axis=-1, keepdims=True)) + lambda_init)
    probs = []
    for t in range(2):
        s = _dot_nt(q_ref[:, t * LANES:(t + 1) * LANES], k_ref[:, t * LANES:(t + 1) * LANES])
        e = jnp.exp(s - jnp.max(s, axis=-1, keepdims=True))
        probs.append(e / jnp.sum(e, axis=-1, keepdims=True))
    o = _dot(_mx(probs[0] - lam * probs[1]), v_ref[...])
    o_ref[...] = (_rms(o) * g_ref[...] * (1.0 - lambda_init)).astype(o_ref.dtype)


def _diff_attention(uq, kd, v, lam_params, subln_g, geo, n_heads, lambda_init):
    tq = DIFF_TQ
    nq = geo.n_lat // tq
    upb = geo.rpb // tq
    ucx = geo.n_ctx // tq
    return pl.pallas_call(
        functools.partial(_diff_kernel, lambda_init),
        grid=(geo.batch, n_heads, nq),
        in_specs=[pl.BlockSpec((tq, 2 * LANES), lambda b, h, j: (b * upb + ucx + j, h)),
                  pl.BlockSpec((geo.rpb, 2 * LANES), lambda b, h, j: (b, h)),
                  pl.BlockSpec((geo.rpb, LANES), lambda b, h, j: (b, h)),
                  pl.BlockSpec(lam_params.shape, lambda b, h, j: (0, 0)),
                  pl.BlockSpec((1, 2 * HEAD), lambda b, h, j: (0, 0))],
        out_specs=pl.BlockSpec((tq, 2 * HEAD), lambda b, h, j: (b * nq + j, h)),
        out_shape=jax.ShapeDtypeStruct((geo.batch * geo.n_lat, n_heads * 2 * HEAD), MXU_DTYPE),
        compiler_params=_params("arbitrary", "arbitrary", "arbitrary"),
        name="diff_attention",
    )(uq, kd, v, lam_params, subln_g)


def _post_kernel(x_ref, a_ref, wo_ref, bo_ref, mod_ref, g_ref, wr_hi_ref, wr_lo_ref, br_ref,
                 xo_ref, h_ref, top_ref, gate_ref):
    mod = mod_ref[...]
    x = x_ref[...] + mod[2:3, :] * (_dot(a_ref[...], wo_ref[...]) + bo_ref[...])
    xo_ref[...] = x
    h = _norm_mod(x, g_ref[...], mod, 3, 4)
    h_ref[...] = h
    h_hi, h_lo = _split2(h)
    logits = _dot(h_hi, wr_hi_ref[...]) + _dot(h_lo, wr_hi_ref[...]) + _dot(h_hi, wr_lo_ref[...]) + br_ref[...]
    lane = lax.broadcasted_iota(I32, logits.shape, 1)
    lane_f = lane.astype(F32)
    top = jnp.zeros(logits.shape, I32)
    gate = jnp.zeros(logits.shape, F32)
    m0 = None
    for k in range(TOP_K):
        m = jnp.max(logits, axis=-1, keepdims=True)
        idx = jnp.min(jnp.where(logits == m, lane_f, float(LANES)), axis=-1, keepdims=True).astype(I32)
        logits = jnp.where(lane == idx, -jnp.inf, logits)
        m0 = m if k == 0 else m0
        top = jnp.where(lane == k, idx, top)
        gate = jnp.where(lane == k, jnp.exp(m - m0), gate)
    top_ref[...] = top
    gate_ref[...] = gate / jnp.sum(gate, axis=-1, keepdims=True)


def _post(x, a, wo, bo, modtab, gain, wr_hi, wr_lo, br, x_tile, mod_index, name):
    rows, din = a.shape
    d = x.shape[1]
    row_spec = pl.BlockSpec((TILE, d), lambda i: (i, 0))
    meta_spec = pl.BlockSpec((TILE, LANES), lambda i: (i, 0))
    const = lambda shape: pl.BlockSpec(shape, lambda i: (0, 0))
    return pl.pallas_call(
        _post_kernel,
        grid=(rows // TILE,),
        in_specs=[pl.BlockSpec((TILE, d), lambda i: (x_tile(i), 0)),
                  pl.BlockSpec((TILE, din), lambda i: (i, 0)),
                  const((din, d)), const((1, d)),
                  pl.BlockSpec((None, ADA_CHUNKS, d), lambda i: (mod_index(i), 0, 0)),
                  const((1, d)), const((d, LANES)), const((d, LANES)), const((1, LANES))],
        out_specs=[row_spec, row_spec, meta_spec, meta_spec],
        out_shape=[jax.ShapeDtypeStruct((rows, d), F32), jax.ShapeDtypeStruct((rows, d), F32),
                   jax.ShapeDtypeStruct((rows, LANES), I32), jax.ShapeDtypeStruct((rows, LANES), F32)],
        compiler_params=_params("arbitrary"),
        name=name,
    )(x, a, wo, bo, modtab, gain, wr_hi, wr_lo, br)


def _moe_pos_kernel(top_ref, pos_ref, meta_ref, count_ref, start_ref):
    phase = pl.program_id(0)
    i = pl.program_id(1)
    top = top_ref[...]
    lane = lax.broadcasted_iota(I32, top.shape, 1)
    onehots = [jnp.where(lane == top[:, k:k + 1], 1.0, 0.0) for k in range(TOP_K)]

    @pl.when(jnp.logical_and(phase == 0, i == 0))
    def _():
        count_ref[...] = jnp.zeros_like(count_ref)

    @pl.when(phase == 0)
    def _():
        tile_count = sum(jnp.sum(o, axis=0, keepdims=True) for o in onehots)
        count_ref[...] = count_ref[...] + tile_count

    @pl.when(jnp.logical_and(phase == 1, i == 0))
    def _():
        counts = count_ref[...]
        padded = jnp.ceil(counts / EXPERT_TILE) * EXPERT_TILE
        lane8 = lax.broadcasted_iota(I32, counts.shape, 1)
        incl = padded
        shift = 1
        while shift < LANES:
            incl = incl + jnp.where(lane8 >= shift, pltpu.roll(incl, shift, 1), 0.0)
            shift *= 2
        start_ref[...] = incl - padded
        sub = lax.broadcasted_iota(I32, counts.shape, 0)
        meta_ref[...] = jnp.where(sub == 0, counts, jnp.where(sub == 1, incl - padded, 0.0))
        count_ref[...] = jnp.zeros_like(count_ref)

    @pl.when(phase == 1)
    def _():
        rows = top.shape[0]
        r = lax.broadcasted_iota(I32, (rows, rows), 0)
        c = lax.broadcasted_iota(I32, (rows, rows), 1)
        strict_lower = _mx(jnp.where(r > c, 1.0, 0.0))
        base = count_ref[0:1, :] + start_ref[0:1, :]
        pos = jnp.zeros(top.shape, I32)
        for k in range(TOP_K):
            before = _dot(strict_lower, _mx(onehots[k])) + base
            slot = jnp.sum(onehots[k] * before, axis=-1, keepdims=True)
            pos = jnp.where(lane == k, slot.astype(I32), pos)
            base = base + jnp.sum(onehots[k], axis=0, keepdims=True)
        pos_ref[...] = pos
        count_ref[...] = jnp.broadcast_to(base - start_ref[0:1, :], count_ref.shape)


def _moe_pos(top):
    rows = top.shape[0]
    return pl.pallas_call(
        _moe_pos_kernel,
        grid=(2, rows // TILE),
        in_specs=[pl.BlockSpec((TILE, LANES), lambda p, i: (i, 0))],
        out_specs=[pl.BlockSpec((TILE, LANES), lambda p, i: (i * p, 0)),
                   pl.BlockSpec((8, LANES), lambda p, i: (0, 0))],
        out_shape=[jax.ShapeDtypeStruct((rows, LANES), I32), jax.ShapeDtypeStruct((8, LANES), F32)],
        scratch_shapes=[pltpu.VMEM((8, LANES), F32), pltpu.VMEM((8, LANES), F32)],
        compiler_params=_params("arbitrary", "arbitrary"),
        name="moe_pos",
    )(top)


def _row_copy(src_ref, src_row, dst_ref, dst_row, sem):
    return pltpu.make_async_copy(src_ref.at[pl.ds(src_row, 1)], dst_ref.at[pl.ds(dst_row, 1)], sem)


def _dispatch_kernel(pos_ref, h_ref, xs_ref, sem):
    base = pl.program_id(0) * (TILE * TOP_K)

    def issue(t, carry):
        for k in range(TOP_K):
            _row_copy(h_ref, t, xs_ref, pos_ref[base + t * TOP_K + k], sem).start()
        return carry

    def drain(t, carry):
        for k in range(TOP_K):
            _row_copy(h_ref, t, xs_ref, pos_ref[base + t * TOP_K + k], sem).wait()
        return carry

    lax.fori_loop(0, TILE, issue, 0)
    lax.fori_loop(0, TILE, drain, 0)


def _dispatch(pos_flat, h, n_slots):
    rows, d = h.shape
    return pl.pallas_call(
        _dispatch_kernel,
        grid_spec=pltpu.PrefetchScalarGridSpec(
            num_scalar_prefetch=1,
            grid=(rows // TILE,),
            in_specs=[pl.BlockSpec((TILE, d), lambda i, pos: (i, 0))],
            out_specs=pl.BlockSpec(memory_space=pl.ANY),
            scratch_shapes=[pltpu.SemaphoreType.DMA(())]),
        out_shape=jax.ShapeDtypeStruct((n_slots, d), F32),
        compiler_params=_params("arbitrary"),
        name="moe_dispatch",
    )(pos_flat, h)


def _expert_kernel(d_expert, te_ref, tv_ref, na_ref, x_ref, wgu_ref, bgu_ref, wd_ref, bd_ref, o_ref,
                   wgu_mx, wd_mx):
    i = pl.program_id(0)
    active = i < na_ref[0]
    new_expert = jnp.logical_or(i == 0, te_ref[i] != te_ref[jnp.maximum(i - 1, 0)])

    @pl.when(jnp.logical_and(active, new_expert))
    def _():
        wgu_mx[...] = _mx(wgu_ref[...])
        wd_mx[...] = _mx(wd_ref[...])

    @pl.when(active)
    def _():
        row = lax.broadcasted_iota(I32, (EXPERT_TILE, 1), 0)
        x = jnp.where(row < tv_ref[i], x_ref[...], 0.0)
        gu = _dot(_mx(x), wgu_mx[...]) + bgu_ref[...]
        glu = jnp.minimum(gu[:, :d_expert], SWIGLU_LIMIT)
        lin = jnp.clip(gu[:, d_expert:], -SWIGLU_LIMIT, SWIGLU_LIMIT)
        act = glu * _sigmoid(SWIGLU_ALPHA * glu) * (lin + 1.0)
        o_ref[...] = _dot(_mx(act), wd_mx[...]) + bd_ref[...]


def _experts(tile_expert, tile_valid, n_active, xs, w_gu, b_gu, w_down, b_down):
    n_slots, d = xs.shape
    n_exp, _, two_de = w_gu.shape
    de = two_de // 2
    n_tiles = n_slots // EXPERT_TILE
    row = lambda i, te, tv, na: (jnp.minimum(i, na[0] - 1), 0)
    by_expert = lambda i, te, tv, na: (te[i], 0, 0)
    return pl.pallas_call(
        functools.partial(_expert_kernel, de),
        grid_spec=pltpu.PrefetchScalarGridSpec(
            num_scalar_prefetch=3,
            grid=(n_tiles,),
            in_specs=[pl.BlockSpec((EXPERT_TILE, d), row),
                      pl.BlockSpec((None, d, two_de), by_expert),
                      pl.BlockSpec((None, 1, two_de), by_expert),
                      pl.BlockSpec((None, de, d), by_expert),
                      pl.BlockSpec((None, 1, d), by_expert)],
            out_specs=pl.BlockSpec((EXPERT_TILE, d), row),
            scratch_shapes=[pltpu.VMEM((d, two_de), MXU_DTYPE), pltpu.VMEM((de, d), MXU_DTYPE)]),
        out_shape=jax.ShapeDtypeStruct((n_slots, d), F32),
        compiler_params=_params("arbitrary"),
        name="moe_experts",
    )(tile_expert, tile_valid, n_active, xs, w_gu, b_gu.reshape(n_exp, 1, two_de), w_down,
      b_down.reshape(n_exp, 1, d))


def _combine_kernel(final, pos_ref, x_ref, gate_ref, mod_ref, gfin_ref, ys_ref, o_ref, buf, sem):
    base = pl.program_id(0) * (TILE * TOP_K)

    def issue(t, carry):
        for k in range(TOP_K):
            _row_copy(ys_ref, pos_ref[base + t * TOP_K + k], buf.at[k], t, sem).start()
        return carry

    def drain(t, carry):
        for k in range(TOP_K):
            _row_copy(ys_ref, pos_ref[base + t * TOP_K + k], buf.at[k], t, sem).wait()
        return carry

    lax.fori_loop(0, TILE, issue, 0)
    lax.fori_loop(0, TILE, drain, 0)
    gates = gate_ref[...]
    f = gates[:, 0:1] * buf[0]
    for k in range(1, TOP_K):
        f = f + gates[:, k:k + 1] * buf[k]
    x = x_ref[...] + mod_ref[5:6, :] * f
    if final:
        x = _rms(x) * gfin_ref[...]
    o_ref[...] = x


def _combine(pos_flat, x, gates, modtab, g_final, ys, mod_index, final):
    rows, d = x.shape
    return pl.pallas_call(
        functools.partial(_combine_kernel, final),
        grid_spec=pltpu.PrefetchScalarGridSpec(
            num_scalar_prefetch=1,
            grid=(rows // TILE,),
            in_specs=[pl.BlockSpec((TILE, d), lambda i, pos: (i, 0)),
                      pl.BlockSpec((TILE, LANES), lambda i, pos: (i, 0)),
                      pl.BlockSpec((None, ADA_CHUNKS, d), lambda i, pos: (mod_index(i), 0, 0)),
                      pl.BlockSpec((1, d), lambda i, pos: (0, 0)),
                      pl.BlockSpec(memory_space=pl.ANY)],
            out_specs=pl.BlockSpec((TILE, d), lambda i, pos: (i, 0)),
            scratch_shapes=[pltpu.VMEM((TOP_K, TILE, d), F32), pltpu.SemaphoreType.DMA(())]),
        out_shape=jax.ShapeDtypeStruct((rows, d), F32),
        compiler_params=_params("arbitrary"),
        name="moe_combine",
    )(pos_flat, x, gates, modtab, g_final, ys)


def _moe(x, h, top, gates, modtab, mod_index, w_gu, b_gu, w_down, b_down, g_final, final):
    rows = h.shape[0]
    n_exp = w_gu.shape[0]
    pos, meta = _moe_pos(top)
    counts = meta[0, :n_exp].astype(I32)
    starts = meta[1, :n_exp].astype(I32)
    n_tiles = rows * TOP_K // EXPERT_TILE + n_exp
    tiles_per = (counts + EXPERT_TILE - 1) // EXPERT_TILE
    tile_start = starts // EXPERT_TILE
    tile_end = tile_start + tiles_per
    n_active = tile_end[-1]
    tile = jnp.minimum(jnp.arange(n_tiles, dtype=I32), n_active - 1)
    tile_expert = jnp.sum(tile_end[None, :] <= tile[:, None], axis=1).astype(I32)
    tile_valid = jnp.clip(counts[tile_expert] - (tile - tile_start[tile_expert]) * EXPERT_TILE, 0, EXPERT_TILE)
    pos_flat = pos[:, :TOP_K].reshape(-1)
    xs = _dispatch(pos_flat, h, n_tiles * EXPERT_TILE)
    ys = _experts(tile_expert, tile_valid.astype(I32), n_active.reshape(1), xs, w_gu, b_gu, w_down, b_down)
    return _combine(pos_flat, x, gates, modtab, g_final, ys, mod_index, final)


def _rot_cols(w):
    lead = w.shape[:-1]
    blocks = w.reshape(lead + (-1, 2, HEAD // 2))
    return jnp.concatenate([-blocks[..., 1:2, :], blocks[..., 0:1, :]], axis=-2).reshape(w.shape)


def _pair_cols(a, b):
    lead = a.shape[:-1]
    a3 = a.reshape(lead + (-1, HEAD))
    b3 = b.reshape(lead + (-1, HEAD))
    return jnp.concatenate([a3, b3], axis=-1).reshape(lead + (-1,))


def _rope_tables(geo):
    t = jnp.arange(geo.n_lat)
    rowp = (t // GRID_W).astype(F32)
    colp = (t % GRID_W).astype(F32)
    quarter = HEAD // 4
    inv_freq = ROPE_BASE ** (-jnp.arange(quarter, dtype=F32) / quarter)
    ang = jnp.concatenate([rowp[:, None] * inv_freq, colp[:, None] * inv_freq], axis=-1)
    cos = jnp.concatenate([jnp.ones((geo.n_ctx, HEAD // 2), F32), jnp.cos(ang)], axis=0)
    sin = jnp.concatenate([jnp.zeros((geo.n_ctx, HEAD // 2), F32), jnp.sin(ang)], axis=0)
    cos64 = jnp.concatenate([cos, cos], axis=-1)
    sin64 = jnp.concatenate([sin, sin], axis=-1)
    scale = HEAD ** -0.5
    q_tab = jnp.concatenate([cos64, sin64], axis=-1) * scale
    k_cos = jnp.concatenate([cos64, cos64], axis=-1)
    k_sin = jnp.concatenate([sin64, sin64], axis=-1)
    return q_tab, k_cos, k_sin


def _row(v):
    return v.reshape(1, -1).astype(F32)


def kernel(x, c, ctx, c_ctx, ada_w, ada_b, g_mix, g_ffn, g_final, conv_w_pw1, conv_b_pw1, conv_w_dw, conv_b_dw, conv_ln_g, conv_ln_b, conv_w_pw2, conv_b_pw2, ssm_w_in, ssm_w_conv, ssm_b_conv, ssm_a_log, ssm_dt_bias, ssm_d, ssm_norm_g, ssm_w_out, swa_w_qkv, swa_b_qkv, swa_sinks, swa_w_o, swa_b_o, diff_w_qkv, diff_lambda_q1, diff_lambda_k1, diff_lambda_q2, diff_lambda_k2, diff_subln_g, diff_w_o, moe_w_router, moe_b_router, moe_w_gu, moe_b_gu, moe_w_down, moe_b_down):
    batch, n_lat, d = x.shape
    n_ctx = ctx.shape[1]
    depth = ada_w.shape[0]
    geo = _Geo(batch, n_ctx, n_lat)
    n_exp = moe_w_router.shape[-1]
    q_tab, k_cos, k_sin = _rope_tables(geo)

    cond_rows = 16
    assert batch + 1 <= cond_rows
    cond = jnp.zeros((cond_rows, d), F32).at[:batch].set(c).at[batch].set(c_ctx)
    ada = _adaln(cond, ada_w, ada_b)
    mod_lat = ada[:, :batch].reshape(depth, batch, ADA_CHUNKS, d)
    mod_ctx = jnp.broadcast_to(ada[:, batch].reshape(depth, 1, ADA_CHUNKS, d), mod_lat.shape)
    modtabs = jnp.stack([mod_ctx, mod_lat], axis=2).reshape(depth, 2 * batch, ADA_CHUNKS, d)

    xs = jnp.concatenate([ctx, x], axis=1).reshape(geo.rows, d)
    zero_bias = jnp.zeros((1, d), F32)
    out = None
    for i in range(depth):
        kind, j = i % 4, i // 4
        ctx_out = i < depth - 1
        modtab = modtabs[i]
        gain = _row(g_mix[i])
        if kind == 0:
            w1 = _mx(conv_w_pw1[j])
            b1 = _row(conv_b_pw1[j])
            u = _norm_proj(xs, gain, modtab, geo, "glu", [w1[:, :d], b1[:, :d], w1[:, d:], b1[:, d:]], d, F32,
                           "conv_pw1_glu")
            a = _dwconv(u, conv_w_dw[j], conv_b_dw[j], geo, "ln_silu", [_row(conv_ln_g[j]), _row(conv_ln_b[j])],
                        d, MXU_DTYPE, "conv_dw_ln")
            wo, bo = _mx(conv_w_pw2[j]), _row(conv_b_pw2[j])
        elif kind == 1:
            di = ssm_norm_g.shape[-1]
            heads = ssm_a_log.shape[-1]
            conv_dim = ssm_w_conv.shape[-1]
            w_in = ssm_w_in[j]
            z = _norm_proj(xs, gain, modtab, geo, "plain", [_mx(w_in[:, :di]), jnp.zeros((1, di), F32)], di, F32,
                           "ssm_in_z")
            tn = 1024 if conv_dim % 1024 == 0 else 512
            xbc = _norm_proj(xs, gain, modtab, geo, "plain",
                             [_mx(w_in[:, di:di + conv_dim]), jnp.zeros((1, conv_dim), F32)], tn, F32, "ssm_in_xbc")
            w_dt = w_in[:, di + conv_dim:].reshape(d, 2, heads)
            w_dt = jnp.pad(w_dt, ((0, 0), (0, 0), (0, LANES - heads))).reshape(d, 2 * LANES)
            w_dt_hi = _mx(w_dt)
            w_dt_lo = _mx(w_dt - w_dt_hi.astype(F32))
            dt_raw = _norm_proj(xs, gain, modtab, geo, "precise", [w_dt_hi, w_dt_lo, jnp.zeros((1, 2 * LANES), F32)],
                                2 * LANES, F32, "ssm_in_dt")
            xbc = _dwconv(xbc, ssm_w_conv[j], ssm_b_conv[j], geo, "silu", [], tn, F32, "ssm_conv")
            y2 = _ssd(xbc, dt_raw, ssm_dt_bias[j], ssm_a_log[j], geo, di, heads)
            a = _ssm_finish(y2, xbc, z, _row(jnp.repeat(ssm_d[j], HEAD)), _row(ssm_norm_g[j]))
            wo, bo = _mx(ssm_w_out[j]), zero_bias
        elif kind == 2:
            nh = swa_sinks.shape[-1]
            nq, nkv = nh * HEAD, SWA_KV_HEADS * HEAD
            w, b = swa_w_qkv[j], swa_b_qkv[j][None, :]
            wq, wk, wv = w[:, :nq], w[:, nq:nq + nkv], w[:, nq + nkv:]
            bq, bk, bv = b[:, :nq], b[:, nq:nq + nkv], b[:, nq + nkv:]
            uq = _norm_proj(xs, gain, modtab, geo, "tab1",
                            [_mx(_pair_cols(wq, _rot_cols(wq))), _pair_cols(bq, _rot_cols(bq)), q_tab],
                            2 * nq, MXU_DTYPE, "swa_q")
            kd = _norm_proj(xs, gain, modtab, geo, "tab2",
                            [_mx(_pair_cols(wk, wk)), _pair_cols(bk, bk), k_cos,
                             _mx(_pair_cols(_rot_cols(wk), _rot_cols(wk))), _pair_cols(_rot_cols(bk), _rot_cols(bk)),
                             k_sin], 2 * nkv, MXU_DTYPE, "swa_k")
            vp = _norm_proj(xs, gain, modtab, geo, "plain",
                            [_mx(_pair_cols(wv, jnp.zeros_like(wv))), _pair_cols(bv, jnp.zeros_like(bv))],
                            2 * nkv, MXU_DTYPE, "swa_v")
            a = _swa_attention(uq, kd, vp, swa_sinks[j].astype(F32), geo, nh)
            wo, bo = _mx(swa_w_o[j]), _row(swa_b_o[j])
        else:
            assert not ctx_out, "differential attention is only built for a layer without context output"
            lambda_init = 0.8 - 0.6 * math.exp(-0.3 * i)
            w = diff_w_qkv[j]
            wq, wk, wv = w[:, :d], w[:, d:2 * d], w[:, 2 * d:]
            zb = jnp.zeros((1, 2 * d), F32)
            uq = _norm_proj(xs, gain, modtab, geo, "tab1", [_mx(_pair_cols(wq, _rot_cols(wq))), zb, q_tab],
                            2 * d, MXU_DTYPE, "diff_q")
            kd = _norm_proj(xs, gain, modtab, geo, "tab2",
                            [_mx(_pair_cols(wk, wk)), zb, k_cos,
                             _mx(_pair_cols(_rot_cols(wk), _rot_cols(wk))), zb, k_sin], 2 * d, MXU_DTYPE, "diff_k")
            v = _norm_proj(xs, gain, modtab, geo, "plain", [_mx(wv), zero_bias], d, MXU_DTYPE, "diff_v")
            lam_params = jnp.stack([diff_lambda_q1[j], diff_lambda_k1[j], diff_lambda_q2[j],
                                    diff_lambda_k2[j]]).astype(F32)
            a = _diff_attention(uq, kd, v, lam_params, _row(diff_subln_g[j]), geo, d // (2 * HEAD), lambda_init)
            wo, bo = _mx(diff_w_o[j]), zero_bias

        wr = jnp.pad(moe_w_router[i], ((0, 0), (0, LANES - n_exp)))
        wr_hi = _mx(wr)
        wr_lo = _mx(wr - wr_hi.astype(F32))
        br = jnp.pad(_row(moe_b_router[i]), ((0, 0), (0, LANES - n_exp)), constant_values=-1e30)
        if ctx_out:
            x_tile, mod_index = (lambda t: t), geo.mod_all
        else:
            x_tile, mod_index = geo.lat_tile, geo.mod_lat
            if a.shape[0] == geo.rows:
                a = a.reshape(batch, geo.rpb, -1)[:, n_ctx:].reshape(batch * n_lat, -1)
        xs, h, top, gates = _post(xs, a, wo, bo, modtab, _row(g_ffn[i]), wr_hi, wr_lo, br, x_tile, mod_index,
                                  "post_mixer")
        final = i == depth - 1
        xs = _moe(xs, h, top, gates, modtab, mod_index, moe_w_gu[i], moe_b_gu[i], moe_w_down[i],
                  moe_b_down[i], _row(g_final), final)
        if not ctx_out and not final:
            raise NotImplementedError("a layer without context output must be the last layer")
        out = xs
    return out.reshape(batch, n_lat, d)
```

```python
import functools
import math

import jax
import jax.numpy as jnp
from jax import lax
from jax.experimental import pallas as pl
from jax.experimental.pallas import tpu as pltpu

F32 = jnp.float32
I32 = jnp.int32
MXU_DTYPE = jnp.bfloat16

LANES = 128
VMEM_LIMIT_BYTES = 56 * 1024 * 1024

TILE = 256
CHUNK = 128
HALO = 16
HEAD = 64
NORM_EPS = 1e-6
ROPE_BASE = 10000.0
GRID_W = 64
ADA_CHUNKS = 6
SSM_GROUPS = 4
SSM_STATE = 128
SWA_KV_HEADS = 4
SWA_WINDOW = 128
TOP_K = 4
SWIGLU_LIMIT = 7.0
SWIGLU_ALPHA = 1.702
EXPERT_TILE = 256
ROW_UNROLL = 4
DIFF_TQ = 256


def _mx(v):
    return v.astype(MXU_DTYPE)


def _dot(a, b):
    return jnp.dot(a, b, preferred_element_type=F32)


def _dot_nt(a, b):
    return lax.dot_general(a, b, (((1,), (1,)), ((), ())), preferred_element_type=F32)


def _split2(v):
    hi = _mx(v)
    return hi, _mx(v - hi.astype(F32))


def _split3(v):
    hi = _mx(v)
    r = v - hi.astype(F32)
    mid = _mx(r)
    return hi, mid, _mx(r - mid.astype(F32))


def _dot_exact_rhs(parts, m):
    acc = _dot(parts[0], m)
    for p in parts[1:]:
        acc = acc + _dot(p, m)
    return acc


def _sigmoid(v):
    return 1.0 / (1.0 + jnp.exp(-v))


def _softplus(v):
    return jnp.maximum(v, 0.0) + jnp.log(1.0 + jnp.exp(-jnp.abs(v)))


def _rms(v):
    return v * lax.rsqrt(jnp.mean(v * v, axis=-1, keepdims=True) + NORM_EPS)


def _params(*sem):
    return pltpu.CompilerParams(dimension_semantics=sem, vmem_limit_bytes=VMEM_LIMIT_BYTES)


class _Geo:
    def __init__(self, batch, n_ctx, n_lat):
        assert n_ctx % TILE == 0 and n_lat % TILE == 0 and n_lat % n_ctx == 0
        self.batch, self.n_ctx, self.n_lat = batch, n_ctx, n_lat
        self.rpb = n_ctx + n_lat
        self.tpb = self.rpb // TILE
        self.nct = n_ctx // TILE
        self.nlt = n_lat // TILE
        self.cpb = self.rpb // CHUNK
        self.ncc = n_ctx // CHUNK
        self.nlc = n_lat // CHUNK
        self.rows = batch * self.rpb

    def mod_all(self, i):
        return (i // self.tpb) * 2 + (i % self.tpb >= self.nct).astype(I32)

    def lat_tile(self, i):
        return (i // self.nlt) * self.tpb + self.nct + i % self.nlt

    def mod_lat(self, i):
        return (i // self.nlt) * 2 + 1


def _adaln_kernel(c_ref, w_ref, b_ref, o_ref):
    c = c_ref[...]
    s_hi, s_lo = _split2(c * _sigmoid(c))
    w_hi, w_lo = _split2(w_ref[...])
    o_ref[...] = _dot(s_hi, w_hi) + _dot(s_lo, w_hi) + _dot(s_hi, w_lo) + b_ref[...]


def _adaln(cond, ada_w, ada_b):
    depth, d, n = ada_w.shape
    rows = cond.shape[0]
    return pl.pallas_call(
        _adaln_kernel,
        grid=(depth, n // d),
        in_specs=[pl.BlockSpec((rows, d), lambda l, j: (0, 0)),
                  pl.BlockSpec((None, d, d), lambda l, j: (l, 0, j)),
                  pl.BlockSpec((None, 1, d), lambda l, j: (l, 0, j))],
        out_specs=pl.BlockSpec((None, rows, d), lambda l, j: (l, 0, j)),
        out_shape=jax.ShapeDtypeStruct((depth, rows, n), F32),
        compiler_params=_params("arbitrary", "arbitrary"),
        name="adaln",
    )(cond, ada_w, ada_b.reshape(depth, 1, n))


def _norm_mod(x, g, mod, shift_row, scale_row):
    return _rms(x) * g * (1.0 + mod[scale_row:scale_row + 1, :]) + mod[shift_row:shift_row + 1, :]


def _lane_tile(tab, n):
    return jnp.tile(tab, (1, n // LANES))


def _norm_proj_kernel(mode, x_ref, g_ref, mod_ref, *refs):
    o_ref = refs[-1]
    n = o_ref.shape[-1]
    h = _norm_mod(x_ref[...], g_ref[...], mod_ref[...], 0, 1)
    if mode == "precise":
        w_hi, w_lo, b = refs[:3]
        h_hi, h_lo = _split2(h)
        acc = _dot(h_hi, w_hi[...]) + _dot(h_lo, w_hi[...]) + _dot(h_hi, w_lo[...]) + b[...]
    else:
        hb = _mx(h)
        if mode == "plain":
            w, b = refs[:2]
            acc = _dot(hb, w[...]) + b[...]
        elif mode == "glu":
            w1, b1, w2, b2 = refs[:4]
            acc = (_dot(hb, w1[...]) + b1[...]) * _sigmoid(_dot(hb, w2[...]) + b2[...])
        elif mode == "tab1":
            w1, b1, t1 = refs[:3]
            acc = (_dot(hb, w1[...]) + b1[...]) * _lane_tile(t1[...], n)
        else:
            w1, b1, t1, w2, b2, t2 = refs[:6]
            acc = ((_dot(hb, w1[...]) + b1[...]) * _lane_tile(t1[...], n)
                   + (_dot(hb, w2[...]) + b2[...]) * _lane_tile(t2[...], n))
    o_ref[...] = acc.astype(o_ref.dtype)


_PROJ_OPERANDS = {"plain": "wb", "precise": "wwb", "glu": "wbwb", "tab1": "wbt", "tab2": "wbtwbt"}


def _norm_proj(x, gain, modtab, geo, mode, operands, tn, out_dtype, name):
    rows, d = x.shape
    kinds = _PROJ_OPERANDS[mode]
    n = operands[0].shape[1]
    specs = [pl.BlockSpec((TILE, d), lambda j, i: (i, 0)),
             pl.BlockSpec((1, d), lambda j, i: (0, 0)),
             pl.BlockSpec((None, ADA_CHUNKS, d), lambda j, i: (geo.mod_all(i), 0, 0))]
    for kind in kinds:
        if kind == "w":
            specs.append(pl.BlockSpec((d, tn), lambda j, i: (0, j)))
        elif kind == "b":
            specs.append(pl.BlockSpec((1, tn), lambda j, i: (0, j)))
        else:
            specs.append(pl.BlockSpec((TILE, LANES), lambda j, i: (i % geo.tpb, 0)))
    return pl.pallas_call(
        functools.partial(_norm_proj_kernel, mode),
        grid=(n // tn, rows // TILE),
        in_specs=specs,
        out_specs=pl.BlockSpec((TILE, tn), lambda j, i: (i, j)),
        out_shape=jax.ShapeDtypeStruct((rows, n), out_dtype),
        compiler_params=_params("arbitrary", "arbitrary"),
        name=name,
    )(x, gain, modtab, *operands)


def _dwconv_kernel(width, mode, tpb, nct, cur_ref, prev_ref, next_ref, w_ref, b_ref, *refs):
    pad_ref = refs[-1]
    o_ref = refs[-2]
    p = pl.program_id(0) % tpb
    has_prev = jnp.logical_and(p != 0, p != nct)
    has_next = jnp.logical_and(p != nct - 1, p != tpb - 1)
    pad_ref[0:HALO, :] = jnp.where(has_prev, prev_ref[...], 0.0)
    pad_ref[HALO:HALO + TILE, :] = cur_ref[...]
    pad_ref[HALO + TILE:, :] = jnp.where(has_next, next_ref[...], 0.0)
    half = (width - 1) // 2
    acc = jnp.broadcast_to(b_ref[...], o_ref.shape)
    for k in range(width):
        off = HALO - half + k
        acc = acc + w_ref[k:k + 1, :] * pad_ref[off:off + TILE, :]
    if mode == "ln_silu":
        g_ref, beta_ref = refs[:2]
        mu = jnp.mean(acc, axis=-1, keepdims=True)
        cen = acc - mu
        acc = cen * lax.rsqrt(jnp.mean(cen * cen, axis=-1, keepdims=True) + NORM_EPS) * g_ref[...] + beta_ref[...]
    o_ref[...] = (acc * _sigmoid(acc)).astype(o_ref.dtype)


def _dwconv(u, w, b, geo, mode, extra, tc, out_dtype, name):
    rows, c = u.shape
    width = w.shape[0]
    per = TILE // HALO
    last = rows // HALO - 1
    specs = [pl.BlockSpec((TILE, tc), lambda i, j: (i, j)),
             pl.BlockSpec((HALO, tc), lambda i, j: (jnp.maximum(i * per - 1, 0), j)),
             pl.BlockSpec((HALO, tc), lambda i, j: (jnp.minimum((i + 1) * per, last), j)),
             pl.BlockSpec((width, tc), lambda i, j: (0, j)),
             pl.BlockSpec((1, tc), lambda i, j: (0, j))]
    specs += [pl.BlockSpec((1, tc), lambda i, j: (0, j)) for _ in extra]
    return pl.pallas_call(
        functools.partial(_dwconv_kernel, width, mode, geo.tpb, geo.nct),
        grid=(rows // TILE, c // tc),
        in_specs=specs,
        out_specs=pl.BlockSpec((TILE, tc), lambda i, j: (i, j)),
        out_shape=jax.ShapeDtypeStruct((rows, c), out_dtype),
        scratch_shapes=[pltpu.VMEM((TILE + 2 * HALO, tc), F32)],
        compiler_params=_params("arbitrary", "arbitrary"),
        name=name,
    )(u, u, u, w, b.reshape(1, c), *extra)


def _ssd_kernel(groups, heads_per_group, x_ref, b_ref, c_ref, dt_ref, bias_ref, alog_ref, e_ref, y_ref, h_ref):
    direction = pl.program_id(0)
    step = pl.program_id(2)
    n_state = SSM_STATE
    rp = heads_per_group * HEAD

    @pl.when(step == 0)
    def _():
        h_ref[...] = jnp.zeros_like(h_ref)

    fwd = direction == 0
    row = lax.broadcasted_iota(I32, (CHUNK, CHUNK), 0)
    col = lax.broadcasted_iota(I32, (CHUNK, CHUNK), 1)
    tri = (row - col) * jnp.where(fwd, 1, -1) >= 0
    dtv = _softplus(dt_ref[...] + bias_ref[...])
    a = dtv * (-jnp.exp(alog_ref[...]))
    tri_m = _mx(jnp.where(tri, 1.0, 0.0))
    a3 = _split3(a)
    acum = _dot(tri_m, a3[0]) + _dot(tri_m, a3[1]) + _dot(tri_m, a3[2])
    acum_t = acum.T
    expand = e_ref[...]
    dt_x = _dot_exact_rhs(_split3(dtv), expand)
    ac_x = _dot_exact_rhs(_split3(acum), expand)
    tot_x = jnp.where(fwd, ac_x[CHUNK - 1:CHUNK, :], ac_x[0:1, :])
    xdt = x_ref[...] * dt_x
    xdt_b = _mx(xdt)
    xdt_end_b = _mx(xdt * jnp.exp(tot_x - ac_x))
    e_ac = jnp.exp(ac_x)
    decay = jnp.exp(tot_x)
    lo_half = lax.broadcasted_iota(I32, (CHUNK, LANES), 1) < HEAD
    for g in range(groups):
        bg = b_ref[:, g * n_state:(g + 1) * n_state]
        cb_g = _mx(c_ref[:, g * n_state:(g + 1) * n_state])
        cb = _dot_nt(cb_g, _mx(bg))
        bg_t = _mx(bg.T)
        h_t = h_ref[g]
        cols = slice(g * rp, (g + 1) * rp)
        y_off = _dot(cb_g, _mx(h_t)) * e_ac[:, cols]
        blocks = []
        for pair in range(heads_per_group // 2):
            xb = xdt_b[:, g * rp + pair * LANES:g * rp + (pair + 1) * LANES]
            halves = []
            for hh in range(2):
                c = g * heads_per_group + 2 * pair + hh
                seg = acum[:, c:c + 1] - acum_t[c:c + 1, :]
                within = jnp.exp(jnp.where(tri, seg, -jnp.inf))
                halves.append(_dot(_mx(cb * within), xb))
            blocks.append(jnp.where(lo_half, halves[0], halves[1]))
        y_ref[:, cols] = jnp.concatenate(blocks, axis=1) + y_off
        h_ref[g] = h_t * decay[:, cols] + _dot(bg_t, xdt_end_b[:, cols])


def _ssd(xbc, dt_raw, dt_bias, a_log, geo, d_inner, heads):
    rows = xbc.shape[0]
    groups = SSM_GROUPS
    hpg = heads // groups
    gn = groups * SSM_STATE
    assert d_inner % gn == 0 and hpg % 2 == 0 and heads <= LANES
    cpb, ncc = geo.cpb, geo.ncc

    def rb(d, b, s):
        back = jnp.where(s < ncc, ncc - 1 - s, cpb + ncc - 1 - s)
        return b * cpb + jnp.where(d == 0, s, back)

    pad = LANES - heads
    bias = jnp.pad(dt_bias.astype(F32), ((0, 0), (0, pad))).reshape(2, 1, LANES)
    alog = jnp.pad(a_log.astype(F32), ((0, 0), (0, pad))).reshape(2, 1, LANES)
    expand = (jnp.arange(LANES)[:, None] == jnp.arange(d_inner)[None, :] // HEAD).astype(MXU_DTYPE)
    return pl.pallas_call(
        functools.partial(_ssd_kernel, groups, hpg),
        grid=(2, geo.batch, cpb),
        in_specs=[pl.BlockSpec((CHUNK, d_inner), lambda d, b, s: (rb(d, b, s), 0)),
                  pl.BlockSpec((CHUNK, gn), lambda d, b, s: (rb(d, b, s), d_inner // gn)),
                  pl.BlockSpec((CHUNK, gn), lambda d, b, s: (rb(d, b, s), d_inner // gn + 1)),
                  pl.BlockSpec((CHUNK, LANES), lambda d, b, s: (rb(d, b, s), d)),
                  pl.BlockSpec((None, 1, LANES), lambda d, b, s: (d, 0, 0)),
                  pl.BlockSpec((None, 1, LANES), lambda d, b, s: (d, 0, 0)),
                  pl.BlockSpec((LANES, d_inner), lambda d, b, s: (0, 0))],
        out_specs=pl.BlockSpec((None, CHUNK, d_inner), lambda d, b, s: (d, rb(d, b, s), 0)),
        out_shape=jax.ShapeDtypeStruct((2, rows, d_inner), F32),
        scratch_shapes=[pltpu.VMEM((groups, SSM_STATE, hpg * HEAD), F32)],
        compiler_params=_params("arbitrary", "arbitrary", "arbitrary"),
        name="ssd_scan",
    )(xbc, xbc, xbc, dt_raw, bias, alog, expand)


def _ssm_finish_kernel(y_ref, xs_ref, z_ref, dskip_ref, g_ref, o_ref):
    y = y_ref[0] + y_ref[1] + xs_ref[...] * dskip_ref[...]
    z = z_ref[...]
    o_ref[...] = (_rms(y * (z * _sigmoid(z))) * g_ref[...]).astype(o_ref.dtype)


def _ssm_finish(y2, xbc, z, d_skip_cols, norm_g):
    rows, di = z.shape
    return pl.pallas_call(
        _ssm_finish_kernel,
        grid=(rows // TILE,),
        in_specs=[pl.BlockSpec((2, TILE, di), lambda i: (0, i, 0)),
                  pl.BlockSpec((TILE, di), lambda i: (i, 0)),
                  pl.BlockSpec((TILE, di), lambda i: (i, 0)),
                  pl.BlockSpec((1, di), lambda i: (0, 0)),
                  pl.BlockSpec((1, di), lambda i: (0, 0))],
        out_specs=pl.BlockSpec((TILE, di), lambda i: (i, 0)),
        out_shape=jax.ShapeDtypeStruct((rows, di), MXU_DTYPE),
        compiler_params=_params("arbitrary"),
        name="ssm_finish",
    )(y2, xbc, z, d_skip_cols, norm_g)


def _swa_kernel(n_heads, ncc, n_lat, n_ctx, sink_ref, q_ref, kp_ref, kc_ref, kn_ref, kx_ref,
                vp_ref, vc_ref, vn_ref, vx_ref, o_ref):
    j = pl.program_id(1)
    is_ctx = j < ncc
    start = (j - ncc) * CHUNK
    span = 3 * CHUNK
    qi = lax.broadcasted_iota(I32, (CHUNK, span), 0)
    rel = lax.broadcasted_iota(I32, (CHUNK, span), 1) - SWA_WINDOW
    kpos = start + rel
    ninf = -jnp.inf
    band = jnp.where(jnp.abs(qi - rel) <= SWA_WINDOW,
                     jnp.where(kpos >= 0, jnp.where(kpos < n_lat, 0.0, ninf), ninf), ninf)
    band = jnp.where(is_ctx, ninf, band)
    group = n_heads // SWA_KV_HEADS
    bias = jnp.concatenate([band, jnp.zeros((CHUNK, n_ctx), F32)], axis=1)
    bias = jnp.concatenate([bias] * group, axis=0)
    head_of_row = lax.broadcasted_iota(I32, (group * CHUNK, 1), 0) // CHUNK
    outs = []
    for kh in range(SWA_KV_HEADS):
        lanes = slice(kh * LANES, (kh + 1) * LANES)
        k_all = jnp.concatenate([kp_ref[:, lanes], kc_ref[:, lanes], kn_ref[:, lanes], kx_ref[:, lanes]], axis=0)
        v_all = jnp.concatenate([vp_ref[:, lanes], vc_ref[:, lanes], vn_ref[:, lanes], vx_ref[:, lanes]], axis=0)
        heads = range(kh * group, (kh + 1) * group)
        q = jnp.concatenate([q_ref[:, h * LANES:(h + 1) * LANES] for h in heads], axis=0)
        sink = jnp.zeros((group * CHUNK, 1), F32)
        for g, h in enumerate(heads):
            sink = jnp.where(head_of_row == g, sink_ref[h], sink)
        s = _dot_nt(q, k_all) + bias
        m = jnp.maximum(jnp.max(s, axis=-1, keepdims=True), sink)
        e = jnp.exp(s - m)
        o = _dot(_mx(e), v_all) / (jnp.sum(e, axis=-1, keepdims=True) + jnp.exp(sink - m))
        outs += [o[g * CHUNK:(g + 1) * CHUNK] for g in range(group)]
    for c in range(n_heads // 2):
        o_ref[:, c * LANES:(c + 1) * LANES] = (outs[2 * c] + pltpu.roll(outs[2 * c + 1], HEAD, 1)).astype(o_ref.dtype)


def _swa_attention(uq, kd, vp, sinks, geo, n_heads):
    rows = uq.shape[0]
    cpb, ncc, nlc = geo.cpb, geo.ncc, geo.nlc
    kvw = SWA_KV_HEADS * LANES
    ctx_per = geo.rpb // geo.n_ctx

    def lat_block(shift):
        def index(b, j):
            jl = jnp.clip(j - ncc + shift, 0, nlc - 1)
            return (b * cpb + ncc + jl, 0)
        return pl.BlockSpec((CHUNK, kvw), index)

    ctx_spec = pl.BlockSpec((geo.n_ctx, kvw), lambda b, j: (b * ctx_per, 0))
    kv_specs = [lat_block(-1), lat_block(0), lat_block(1), ctx_spec]
    return pl.pallas_call(
        functools.partial(_swa_kernel, n_heads, ncc, geo.n_lat, geo.n_ctx),
        grid=(geo.batch, cpb),
        in_specs=[pl.BlockSpec(memory_space=pltpu.SMEM),
                  pl.BlockSpec((CHUNK, n_heads * LANES), lambda b, j: (b * cpb + j, 0))] + kv_specs + kv_specs,
        out_specs=pl.BlockSpec((CHUNK, n_heads * HEAD), lambda b, j: (b * cpb + j, 0)),
        out_shape=jax.ShapeDtypeStruct((rows, n_heads * HEAD), MXU_DTYPE),
        compiler_params=_params("arbitrary", "arbitrary"),
        name="swa_attention",
    )(sinks, uq, kd, kd, kd, kd, vp, vp, vp, vp)


def _diff_kernel(lambda_init, q_ref, k_ref, v_ref, lam_ref, g_ref, o_ref):
    lp = lam_ref[...]
    lam = (jnp.exp(jnp.sum(lp[0:1] * lp[1:2], axis=-1, keepdims=True))
           - jnp.exp(jnp.sum(lp[2:3] * lp[3:4], axis=-1, keepdims=True)) + lambda_init)
    parts = []
    for t in range(2):
        s = _dot_nt(q_ref[:, t * LANES:(t + 1) * LANES], k_ref[:, t * LANES:(t + 1) * LANES])
        e = jnp.exp(s - jnp.max(s, axis=-1, keepdims=True))
        parts.append(_dot(_mx(e), v_ref[...]) / jnp.sum(e, axis=-1, keepdims=True))
    o = parts[0] - lam * parts[1]
    o_ref[...] = (_rms(o) * g_ref[...] * (1.0 - lambda_init)).astype(o_ref.dtype)


def _diff_attention(uq, kd, v, lam_params, subln_g, geo, n_heads, lambda_init):
    tq = DIFF_TQ
    nq = geo.n_lat // tq
    upb = geo.rpb // tq
    ucx = geo.n_ctx // tq
    return pl.pallas_call(
        functools.partial(_diff_kernel, lambda_init),
        grid=(geo.batch, n_heads, nq),
        in_specs=[pl.BlockSpec((tq, 2 * LANES), lambda b, h, j: (b * upb + ucx + j, h)),
                  pl.BlockSpec((geo.rpb, 2 * LANES), lambda b, h, j: (b, h)),
                  pl.BlockSpec((geo.rpb, LANES), lambda b, h, j: (b, h)),
                  pl.BlockSpec(lam_params.shape, lambda b, h, j: (0, 0)),
                  pl.BlockSpec((1, 2 * HEAD), lambda b, h, j: (0, 0))],
        out_specs=pl.BlockSpec((tq, 2 * HEAD), lambda b, h, j: (b * nq + j, h)),
        out_shape=jax.ShapeDtypeStruct((geo.batch * geo.n_lat, n_heads * 2 * HEAD), MXU_DTYPE),
        compiler_params=_params("arbitrary", "arbitrary", "arbitrary"),
        name="diff_attention",
    )(uq, kd, v, lam_params, subln_g)


def _post_kernel(x_ref, a_ref, wo_ref, bo_ref, mod_ref, g_ref, wr_hi_ref, wr_lo_ref, br_ref,
                 xo_ref, h_ref, top_ref, gate_ref):
    mod = mod_ref[...]
    x = x_ref[...] + mod[2:3, :] * (_dot(a_ref[...], wo_ref[...]) + bo_ref[...])
    xo_ref[...] = x
    h = _norm_mod(x, g_ref[...], mod, 3, 4)
    h_ref[...] = h
    h_hi, h_lo = _split2(h)
    logits = _dot(h_hi, wr_hi_ref[...]) + _dot(h_lo, wr_hi_ref[...]) + _dot(h_hi, wr_lo_ref[...]) + br_ref[...]
    lane = lax.broadcasted_iota(I32, logits.shape, 1)
    lane_f = lane.astype(F32)
    top = jnp.zeros(logits.shape, I32)
    gate = jnp.zeros(logits.shape, F32)
    m0 = None
    for k in range(TOP_K):
        m = jnp.max(logits, axis=-1, keepdims=True)
        idx = jnp.min(jnp.where(logits == m, lane_f, float(LANES)), axis=-1, keepdims=True).astype(I32)
        logits = jnp.where(lane == idx, -jnp.inf, logits)
        m0 = m if k == 0 else m0
        top = jnp.where(lane == k, idx, top)
        gate = jnp.where(lane == k, jnp.exp(m - m0), gate)
    top_ref[...] = top
    gate_ref[...] = gate / jnp.sum(gate, axis=-1, keepdims=True)


def _post(x, a, wo, bo, modtab, gain, wr_hi, wr_lo, br, x_tile, mod_index, name):
    rows, din = a.shape
    d = x.shape[1]
    row_spec = pl.BlockSpec((TILE, d), lambda i: (i, 0))
    meta_spec = pl.BlockSpec((TILE, LANES), lambda i: (i, 0))
    const = lambda shape: pl.BlockSpec(shape, lambda i: (0, 0))
    return pl.pallas_call(
        _post_kernel,
        grid=(rows // TILE,),
        in_specs=[pl.BlockSpec((TILE, d), lambda i: (x_tile(i), 0)),
                  pl.BlockSpec((TILE, din), lambda i: (i, 0)),
                  const((din, d)), const((1, d)),
                  pl.BlockSpec((None, ADA_CHUNKS, d), lambda i: (mod_index(i), 0, 0)),
                  const((1, d)), const((d, LANES)), const((d, LANES)), const((1, LANES))],
        out_specs=[row_spec, row_spec, meta_spec, meta_spec],
        out_shape=[jax.ShapeDtypeStruct((rows, d), F32), jax.ShapeDtypeStruct((rows, d), F32),
                   jax.ShapeDtypeStruct((rows, LANES), I32), jax.ShapeDtypeStruct((rows, LANES), F32)],
        compiler_params=_params("arbitrary"),
        name=name,
    )(x, a, wo, bo, modtab, gain, wr_hi, wr_lo, br)


def _moe_pos_kernel(top_ref, pos_ref, meta_ref, count_ref, start_ref):
    phase = pl.program_id(0)
    i = pl.program_id(1)
    top = top_ref[...]
    lane = lax.broadcasted_iota(I32, top.shape, 1)
    onehots = [jnp.where(lane == top[:, k:k + 1], 1.0, 0.0) for k in range(TOP_K)]

    @pl.when(jnp.logical_and(phase == 0, i == 0))
    def _():
        count_ref[...] = jnp.zeros_like(count_ref)

    @pl.when(phase == 0)
    def _():
        tile_count = sum(jnp.sum(o, axis=0, keepdims=True) for o in onehots)
        count_ref[...] = count_ref[...] + tile_count

    @pl.when(jnp.logical_and(phase == 1, i == 0))
    def _():
        counts = count_ref[...]
        padded = jnp.ceil(counts / EXPERT_TILE) * EXPERT_TILE
        lane8 = lax.broadcasted_iota(I32, counts.shape, 1)
        incl = padded
        shift = 1
        while shift < LANES:
            incl = incl + jnp.where(lane8 >= shift, pltpu.roll(incl, shift, 1), 0.0)
            shift *= 2
        start_ref[...] = incl - padded
        sub = lax.broadcasted_iota(I32, counts.shape, 0)
        meta_ref[...] = jnp.where(sub == 0, counts, jnp.where(sub == 1, incl - padded, 0.0))
        count_ref[...] = jnp.zeros_like(count_ref)

    @pl.when(phase == 1)
    def _():
        rows = top.shape[0]
        r = lax.broadcasted_iota(I32, (rows, rows), 0)
        c = lax.broadcasted_iota(I32, (rows, rows), 1)
        strict_lower = _mx(jnp.where(r > c, 1.0, 0.0))
        base = count_ref[0:1, :] + start_ref[0:1, :]
        pos = jnp.zeros(top.shape, I32)
        for k in range(TOP_K):
            before = _dot(strict_lower, _mx(onehots[k])) + base
            slot = jnp.sum(onehots[k] * before, axis=-1, keepdims=True)
            pos = jnp.where(lane == k, slot.astype(I32), pos)
            base = base + jnp.sum(onehots[k], axis=0, keepdims=True)
        pos_ref[...] = pos
        count_ref[...] = jnp.broadcast_to(base - start_ref[0:1, :], count_ref.shape)


def _moe_pos(top):
    rows = top.shape[0]
    return pl.pallas_call(
        _moe_pos_kernel,
        grid=(2, rows // TILE),
        in_specs=[pl.BlockSpec((TILE, LANES), lambda p, i: (i, 0))],
        out_specs=[pl.BlockSpec((TILE, LANES), lambda p, i: (i * p, 0)),
                   pl.BlockSpec((8, LANES), lambda p, i: (0, 0))],
        out_shape=[jax.ShapeDtypeStruct((rows, LANES), I32), jax.ShapeDtypeStruct((8, LANES), F32)],
        scratch_shapes=[pltpu.VMEM((8, LANES), F32), pltpu.VMEM((8, LANES), F32)],
        compiler_params=_params("arbitrary", "arbitrary"),
        name="moe_pos",
    )(top)


def _row_copy(src_ref, src_row, dst_ref, dst_row, sem):
    return pltpu.make_async_copy(src_ref.at[pl.ds(src_row, 1)], dst_ref.at[pl.ds(dst_row, 1)], sem)


def _dispatch_kernel(pos_ref, fill_ref, h_ref, xs_ref, zero_ref, sem):
    base = pl.program_id(0) * (TILE * TOP_K)

    @pl.when(pl.program_id(0) == 0)
    def _():
        zero_ref[...] = jnp.zeros_like(zero_ref)

        def fill_copy(j):
            start = pl.multiple_of(jnp.maximum(fill_ref[j], 0), EXPERT_TILE)
            return pltpu.make_async_copy(zero_ref, xs_ref.at[pl.ds(start, EXPERT_TILE)], sem)

        for j in range(fill_ref.shape[0]):
            pl.when(fill_ref[j] >= 0)(lambda j=j: fill_copy(j).start())
        for j in range(fill_ref.shape[0]):
            pl.when(fill_ref[j] >= 0)(lambda j=j: fill_copy(j).wait())

    def issue(t, carry):
        for k in range(TOP_K):
            _row_copy(h_ref, t, xs_ref, pos_ref[base + t * TOP_K + k], sem).start(priority=k % 2)
        return carry

    lax.fori_loop(0, TILE, issue, 0, unroll=ROW_UNROLL)
    for k in range(TOP_K):
        pltpu.make_async_copy(h_ref, xs_ref.at[pl.ds(0, TILE)], sem).wait()


def _dispatch(pos_flat, fill_rows, h, n_slots):
    rows, d = h.shape
    return pl.pallas_call(
        _dispatch_kernel,
        grid_spec=pltpu.PrefetchScalarGridSpec(
            num_scalar_prefetch=2,
            grid=(rows // TILE,),
            in_specs=[pl.BlockSpec((TILE, d), lambda i, pos, fill: (i, 0))],
            out_specs=pl.BlockSpec(memory_space=pl.ANY),
            scratch_shapes=[pltpu.VMEM((EXPERT_TILE, d), F32), pltpu.SemaphoreType.DMA(())]),
        out_shape=jax.ShapeDtypeStruct((n_slots, d), F32),
        compiler_params=_params("arbitrary"),
        name="moe_dispatch",
    )(pos_flat, fill_rows, h)


def _expert_kernel(d_expert, te_ref, na_ref, x_ref, wgu_ref, bgu_ref, wd_ref, bd_ref, o_ref, wgu_mx, wd_mx):
    i = pl.program_id(0)
    active = i < na_ref[0]
    new_expert = jnp.logical_or(i == 0, te_ref[i] != te_ref[jnp.maximum(i - 1, 0)])

    @pl.when(jnp.logical_and(active, new_expert))
    def _():
        wgu_mx[...] = _mx(wgu_ref[...])
        wd_mx[...] = _mx(wd_ref[...])

    @pl.when(active)
    def _():
        gu = _dot(_mx(x_ref[...]), wgu_mx[...]) + bgu_ref[...]
        glu = jnp.minimum(gu[:, :d_expert], SWIGLU_LIMIT)
        lin = jnp.clip(gu[:, d_expert:], -SWIGLU_LIMIT, SWIGLU_LIMIT)
        act = glu * _sigmoid(SWIGLU_ALPHA * glu) * (lin + 1.0)
        o_ref[...] = _dot(_mx(act), wd_mx[...]) + bd_ref[...]

    @pl.when(jnp.logical_not(active))
    def _():
        o_ref[...] = jnp.zeros_like(o_ref)


def _experts(layer, tile_expert, n_active, xs, w_gu, b_gu, w_down, b_down):
    n_slots, d = xs.shape
    depth, n_exp, _, two_de = w_gu.shape
    de = two_de // 2
    n_tiles = n_slots // EXPERT_TILE
    row = lambda i, te, na: (jnp.minimum(i, na[0] - 1), 0)
    by_expert = lambda i, te, na: (layer, te[i], 0, 0)
    return pl.pallas_call(
        functools.partial(_expert_kernel, de),
        grid_spec=pltpu.PrefetchScalarGridSpec(
            num_scalar_prefetch=2,
            grid=(n_tiles,),
            in_specs=[pl.BlockSpec((EXPERT_TILE, d), row),
                      pl.BlockSpec((None, None, d, two_de), by_expert),
                      pl.BlockSpec((None, None, 1, two_de), by_expert),
                      pl.BlockSpec((None, None, de, d), by_expert),
                      pl.BlockSpec((None, None, 1, d), by_expert)],
            out_specs=pl.BlockSpec((EXPERT_TILE, d), lambda i, te, na: (i, 0)),
            scratch_shapes=[pltpu.VMEM((d, two_de), MXU_DTYPE), pltpu.VMEM((de, d), MXU_DTYPE)]),
        out_shape=jax.ShapeDtypeStruct((n_slots, d), F32),
        compiler_params=_params("arbitrary"),
        name="moe_experts",
    )(tile_expert, n_active, xs, w_gu, b_gu.reshape(depth, n_exp, 1, two_de), w_down,
      b_down.reshape(depth, n_exp, 1, d))


def _combine_kernel(final, pos_ref, x_ref, gate_ref, mod_ref, gfin_ref, ys_ref, o_ref, buf, sem):
    base = pl.program_id(0) * (TILE * TOP_K)

    def issue(t, carry):
        for k in range(TOP_K):
            _row_copy(ys_ref, pos_ref[base + t * TOP_K + k], buf.at[k], t, sem).start(priority=k % 2)
        return carry

    lax.fori_loop(0, TILE, issue, 0, unroll=ROW_UNROLL)
    for k in range(TOP_K):
        pltpu.make_async_copy(ys_ref.at[pl.ds(0, TILE)], buf.at[k], sem).wait()
    gates = gate_ref[...]
    f = gates[:, 0:1] * buf[0]
    for k in range(1, TOP_K):
        f = f + gates[:, k:k + 1] * buf[k]
    x = x_ref[...] + mod_ref[5:6, :] * f
    if final:
        x = _rms(x) * gfin_ref[...]
    o_ref[...] = x


def _combine(pos_flat, x, gates, modtab, g_final, ys, mod_index, final):
    rows, d = x.shape
    return pl.pallas_call(
        functools.partial(_combine_kernel, final),
        grid_spec=pltpu.PrefetchScalarGridSpec(
            num_scalar_prefetch=1,
            grid=(rows // TILE,),
            in_specs=[pl.BlockSpec((TILE, d), lambda i, pos: (i, 0)),
                      pl.BlockSpec((TILE, LANES), lambda i, pos: (i, 0)),
                      pl.BlockSpec((None, ADA_CHUNKS, d), lambda i, pos: (mod_index(i), 0, 0)),
                      pl.BlockSpec((1, d), lambda i, pos: (0, 0)),
                      pl.BlockSpec(memory_space=pl.ANY)],
            out_specs=pl.BlockSpec((TILE, d), lambda i, pos: (i, 0)),
            scratch_shapes=[pltpu.VMEM((TOP_K, TILE, d), F32), pltpu.SemaphoreType.DMA(())]),
        out_shape=jax.ShapeDtypeStruct((rows, d), F32),
        compiler_params=_params("arbitrary"),
        name="moe_combine",
    )(pos_flat, x, gates, modtab, g_final, ys)


def _moe(layer, x, h, top, gates, modtab, mod_index, w_gu, b_gu, w_down, b_down, g_final, final):
    rows = h.shape[0]
    n_exp = w_gu.shape[1]
    pos, meta = _moe_pos(top)
    counts = meta[0, :n_exp].astype(I32)
    starts = meta[1, :n_exp].astype(I32)
    n_tiles = rows * TOP_K // EXPERT_TILE + n_exp
    tiles_per = (counts + EXPERT_TILE - 1) // EXPERT_TILE
    tile_start = starts // EXPERT_TILE
    tile_end = tile_start + tiles_per
    n_active = tile_end[-1]
    tile = jnp.minimum(jnp.arange(n_tiles, dtype=I32), n_active - 1)
    tile_expert = jnp.sum(tile_end[None, :] <= tile[:, None], axis=1).astype(I32)
    pos_flat = pos[:, :TOP_K].reshape(-1)
    last_tile = jnp.where(tiles_per > 0, tile_end - 1, -1)
    tail_tile = n_active + jnp.arange(n_exp, dtype=I32)
    tail_tile = jnp.where(tail_tile < n_tiles, tail_tile, -1)
    fill_tiles = jnp.concatenate([last_tile, tail_tile])
    fill_rows = jnp.where(fill_tiles >= 0, fill_tiles * EXPERT_TILE, -1).astype(I32)
    xs = _dispatch(pos_flat, fill_rows, h, n_tiles * EXPERT_TILE)
    ys = _experts(layer, tile_expert, n_active.reshape(1), xs, w_gu, b_gu, w_down, b_down)
    return _combine(pos_flat, x, gates, modtab, g_final, ys, mod_index, final)


def _rot_cols(w):
    lead = w.shape[:-1]
    blocks = w.reshape(lead + (-1, 2, HEAD // 2))
    return jnp.concatenate([-blocks[..., 1:2, :], blocks[..., 0:1, :]], axis=-2).reshape(w.shape)


def _pair_cols(a, b):
    lead = a.shape[:-1]
    a3 = a.reshape(lead + (-1, HEAD))
    b3 = b.reshape(lead + (-1, HEAD))
    return jnp.concatenate([a3, b3], axis=-1).reshape(lead + (-1,))


def _rope_tables(geo):
    t = jnp.arange(geo.n_lat)
    rowp = (t // GRID_W).astype(F32)
    colp = (t % GRID_W).astype(F32)
    quarter = HEAD // 4
    inv_freq = ROPE_BASE ** (-jnp.arange(quarter, dtype=F32) / quarter)
    ang = jnp.concatenate([rowp[:, None] * inv_freq, colp[:, None] * inv_freq], axis=-1)
    cos = jnp.concatenate([jnp.ones((geo.n_ctx, HEAD // 2), F32), jnp.cos(ang)], axis=0)
    sin = jnp.concatenate([jnp.zeros((geo.n_ctx, HEAD // 2), F32), jnp.sin(ang)], axis=0)
    cos64 = jnp.concatenate([cos, cos], axis=-1)
    sin64 = jnp.concatenate([sin, sin], axis=-1)
    scale = HEAD ** -0.5
    q_tab = jnp.concatenate([cos64, sin64], axis=-1) * scale
    k_cos = jnp.concatenate([cos64, cos64], axis=-1)
    k_sin = jnp.concatenate([sin64, sin64], axis=-1)
    return q_tab, k_cos, k_sin


def _row(v):
    return v.reshape(1, -1).astype(F32)


def kernel(x, c, ctx, c_ctx, ada_w, ada_b, g_mix, g_ffn, g_final, conv_w_pw1, conv_b_pw1, conv_w_dw, conv_b_dw, conv_ln_g, conv_ln_b, conv_w_pw2, conv_b_pw2, ssm_w_in, ssm_w_conv, ssm_b_conv, ssm_a_log, ssm_dt_bias, ssm_d, ssm_norm_g, ssm_w_out, swa_w_qkv, swa_b_qkv, swa_sinks, swa_w_o, swa_b_o, diff_w_qkv, diff_lambda_q1, diff_lambda_k1, diff_lambda_q2, diff_lambda_k2, diff_subln_g, diff_w_o, moe_w_router, moe_b_router, moe_w_gu, moe_b_gu, moe_w_down, moe_b_down):
    batch, n_lat, d = x.shape
    n_ctx = ctx.shape[1]
    depth = ada_w.shape[0]
    geo = _Geo(batch, n_ctx, n_lat)
    n_exp = moe_w_router.shape[-1]
    q_tab, k_cos, k_sin = _rope_tables(geo)

    cond_rows = 16
    assert batch + 1 <= cond_rows
    cond = jnp.zeros((cond_rows, d), F32).at[:batch].set(c).at[batch].set(c_ctx)
    ada = _adaln(cond, ada_w, ada_b)
    mod_lat = ada[:, :batch].reshape(depth, batch, ADA_CHUNKS, d)
    mod_ctx = jnp.broadcast_to(ada[:, batch].reshape(depth, 1, ADA_CHUNKS, d), mod_lat.shape)
    modtabs = jnp.stack([mod_ctx, mod_lat], axis=2).reshape(depth, 2 * batch, ADA_CHUNKS, d)

    xs = jnp.concatenate([ctx, x], axis=1).reshape(geo.rows, d)
    zero_bias = jnp.zeros((1, d), F32)
    out = None
    for i in range(depth):
        kind, j = i % 4, i // 4
        ctx_out = i < depth - 1
        modtab = modtabs[i]
        gain = _row(g_mix[i])
        if kind == 0:
            w1 = _mx(conv_w_pw1[j])
            b1 = _row(conv_b_pw1[j])
            u = _norm_proj(xs, gain, modtab, geo, "glu", [w1[:, :d], b1[:, :d], w1[:, d:], b1[:, d:]], d, F32,
                           "conv_pw1_glu")
            a = _dwconv(u, conv_w_dw[j], conv_b_dw[j], geo, "ln_silu", [_row(conv_ln_g[j]), _row(conv_ln_b[j])],
                        d, MXU_DTYPE, "conv_dw_ln")
            wo, bo = _mx(conv_w_pw2[j]), _row(conv_b_pw2[j])
        elif kind == 1:
            di = ssm_norm_g.shape[-1]
            heads = ssm_a_log.shape[-1]
            conv_dim = ssm_w_conv.shape[-1]
            w_in = ssm_w_in[j]
            z = _norm_proj(xs, gain, modtab, geo, "plain", [_mx(w_in[:, :di]), jnp.zeros((1, di), F32)], di, F32,
                           "ssm_in_z")
            tn = 1024 if conv_dim % 1024 == 0 else 512
            xbc = _norm_proj(xs, gain, modtab, geo, "plain",
                             [_mx(w_in[:, di:di + conv_dim]), jnp.zeros((1, conv_dim), F32)], tn, F32, "ssm_in_xbc")
            w_dt = w_in[:, di + conv_dim:].reshape(d, 2, heads)
            w_dt = jnp.pad(w_dt, ((0, 0), (0, 0), (0, LANES - heads))).reshape(d, 2 * LANES)
            w_dt_hi = _mx(w_dt)
            w_dt_lo = _mx(w_dt - w_dt_hi.astype(F32))
            dt_raw = _norm_proj(xs, gain, modtab, geo, "precise", [w_dt_hi, w_dt_lo, jnp.zeros((1, 2 * LANES), F32)],
                                2 * LANES, F32, "ssm_in_dt")
            xbc = _dwconv(xbc, ssm_w_conv[j], ssm_b_conv[j], geo, "silu", [], tn, F32, "ssm_conv")
            y2 = _ssd(xbc, dt_raw, ssm_dt_bias[j], ssm_a_log[j], geo, di, heads)
            a = _ssm_finish(y2, xbc, z, _row(jnp.repeat(ssm_d[j], HEAD)), _row(ssm_norm_g[j]))
            wo, bo = _mx(ssm_w_out[j]), zero_bias
        elif kind == 2:
            nh = swa_sinks.shape[-1]
            nq, nkv = nh * HEAD, SWA_KV_HEADS * HEAD
            w, b = swa_w_qkv[j], swa_b_qkv[j][None, :]
            wq, wk, wv = w[:, :nq], w[:, nq:nq + nkv], w[:, nq + nkv:]
            bq, bk, bv = b[:, :nq], b[:, nq:nq + nkv], b[:, nq + nkv:]
            uq = _norm_proj(xs, gain, modtab, geo, "tab1",
                            [_mx(_pair_cols(wq, _rot_cols(wq))), _pair_cols(bq, _rot_cols(bq)), q_tab],
                            2 * nq, MXU_DTYPE, "swa_q")
            kd = _norm_proj(xs, gain, modtab, geo, "tab2",
                            [_mx(_pair_cols(wk, wk)), _pair_cols(bk, bk), k_cos,
                             _mx(_pair_cols(_rot_cols(wk), _rot_cols(wk))), _pair_cols(_rot_cols(bk), _rot_cols(bk)),
                             k_sin], 2 * nkv, MXU_DTYPE, "swa_k")
            vp = _norm_proj(xs, gain, modtab, geo, "plain",
                            [_mx(_pair_cols(wv, jnp.zeros_like(wv))), _pair_cols(bv, jnp.zeros_like(bv))],
                            2 * nkv, MXU_DTYPE, "swa_v")
            a = _swa_attention(uq, kd, vp, swa_sinks[j].astype(F32), geo, nh)
            wo, bo = _mx(swa_w_o[j]), _row(swa_b_o[j])
        else:
            assert not ctx_out, "differential attention is only built for a layer without context output"
            lambda_init = 0.8 - 0.6 * math.exp(-0.3 * i)
            w = diff_w_qkv[j]
            wq, wk, wv = w[:, :d], w[:, d:2 * d], w[:, 2 * d:]
            zb = jnp.zeros((1, 2 * d), F32)
            uq = _norm_proj(xs, gain, modtab, geo, "tab1", [_mx(_pair_cols(wq, _rot_cols(wq))), zb, q_tab],
                            2 * d, MXU_DTYPE, "diff_q")
            kd = _norm_proj(xs, gain, modtab, geo, "tab2",
                            [_mx(_pair_cols(wk, wk)), zb, k_cos,
                             _mx(_pair_cols(_rot_cols(wk), _rot_cols(wk))), zb, k_sin], 2 * d, MXU_DTYPE, "diff_k")
            v = _norm_proj(xs, gain, modtab, geo, "plain", [_mx(wv), zero_bias], d, MXU_DTYPE, "diff_v")
            lam_params = jnp.stack([diff_lambda_q1[j], diff_lambda_k1[j], diff_lambda_q2[j],
                                    diff_lambda_k2[j]]).astype(F32)
            a = _diff_attention(uq, kd, v, lam_params, _row(diff_subln_g[j]), geo, d // (2 * HEAD), lambda_init)
            wo, bo = _mx(diff_w_o[j]), zero_bias

        wr = jnp.pad(moe_w_router[i], ((0, 0), (0, LANES - n_exp)))
        wr_hi = _mx(wr)
        wr_lo = _mx(wr - wr_hi.astype(F32))
        br = jnp.pad(_row(moe_b_router[i]), ((0, 0), (0, LANES - n_exp)), constant_values=-1e30)
        if ctx_out:
            x_tile, mod_index = (lambda t: t), geo.mod_all
        else:
            x_tile, mod_index = geo.lat_tile, geo.mod_lat
            if a.shape[0] == geo.rows:
                a = a.reshape(batch, geo.rpb, -1)[:, n_ctx:].reshape(batch * n_lat, -1)
        xs, h, top, gates = _post(xs, a, wo, bo, modtab, _row(g_ffn[i]), wr_hi, wr_lo, br, x_tile, mod_index,
                                  "post_mixer")
        final = i == depth - 1
        xs = _moe(i, xs, h, top, gates, modtab, mod_index, moe_w_gu, moe_b_gu, moe_w_down, moe_b_down,
                  _row(g_final), final)
        if not ctx_out and not final:
            raise NotImplementedError("a layer without context output must be the last layer")
        out = xs
    return out.reshape(batch, n_lat, d)
```

```python
import functools
import math

import jax
import jax.numpy as jnp
from jax import lax
from jax.experimental import pallas as pl
from jax.experimental.pallas import tpu as pltpu

F32 = jnp.float32
I32 = jnp.int32
MXU_DTYPE = jnp.bfloat16

LANES = 128
VMEM_LIMIT_BYTES = 56 * 1024 * 1024

TILE = 256
CHUNK = 128
HALO = 16
HEAD = 64
NORM_EPS = 1e-6
ROPE_BASE = 10000.0
GRID_W = 64
ADA_CHUNKS = 6
SSM_GROUPS = 4
SSM_STATE = 128
SWA_KV_HEADS = 4
SWA_WINDOW = 128
TOP_K = 4
SWIGLU_LIMIT = 7.0
SWIGLU_ALPHA = 1.702
EXPERT_TILE = 512
DISPATCH_ROWS = 512
DIFF_TQ = 256
DIFF_HEADS_PER_STEP = 2


def _mx(v):
    return v.astype(MXU_DTYPE)


def _dot(a, b):
    return jnp.dot(a, b, preferred_element_type=F32)


def _dot_nt(a, b):
    return lax.dot_general(a, b, (((1,), (1,)), ((), ())), preferred_element_type=F32)


def _split2(v):
    hi = _mx(v)
    return hi, _mx(v - hi.astype(F32))


def _split3(v):
    hi = _mx(v)
    r = v - hi.astype(F32)
    mid = _mx(r)
    return hi, mid, _mx(r - mid.astype(F32))


def _dot_exact_rhs(parts, m):
    acc = _dot(parts[0], m)
    for p in parts[1:]:
        acc = acc + _dot(p, m)
    return acc


def _sigmoid(v):
    return 1.0 / (1.0 + jnp.exp(-v))


def _softplus(v):
    return jnp.maximum(v, 0.0) + jnp.log(1.0 + jnp.exp(-jnp.abs(v)))


def _rms(v):
    return v * lax.rsqrt(jnp.mean(v * v, axis=-1, keepdims=True) + NORM_EPS)


def _params(*sem):
    return pltpu.CompilerParams(dimension_semantics=sem, vmem_limit_bytes=VMEM_LIMIT_BYTES)


class _Geo:
    def __init__(self, batch, n_ctx, n_lat):
        assert n_ctx % TILE == 0 and n_lat % TILE == 0 and n_lat % n_ctx == 0
        self.batch, self.n_ctx, self.n_lat = batch, n_ctx, n_lat
        self.rpb = n_ctx + n_lat
        self.tpb = self.rpb // TILE
        self.nct = n_ctx // TILE
        self.nlt = n_lat // TILE
        self.cpb = self.rpb // CHUNK
        self.ncc = n_ctx // CHUNK
        self.nlc = n_lat // CHUNK
        self.rows = batch * self.rpb

    def mod_all(self, i):
        return (i // self.tpb) * 2 + (i % self.tpb >= self.nct).astype(I32)

    def lat_tile(self, i):
        return (i // self.nlt) * self.tpb + self.nct + i % self.nlt

    def mod_lat(self, i):
        return (i // self.nlt) * 2 + 1


def _adaln_kernel(c_ref, w_ref, b_ref, o_ref):
    c = c_ref[...]
    s_hi, s_lo = _split2(c * _sigmoid(c))
    w_hi, w_lo = _split2(w_ref[...])
    o_ref[...] = _dot(s_hi, w_hi) + _dot(s_lo, w_hi) + _dot(s_hi, w_lo) + b_ref[...]


def _adaln(cond, ada_w, ada_b):
    depth, d, n = ada_w.shape
    rows = cond.shape[0]
    return pl.pallas_call(
        _adaln_kernel,
        grid=(depth, n // d),
        in_specs=[pl.BlockSpec((rows, d), lambda l, j: (0, 0)),
                  pl.BlockSpec((None, d, d), lambda l, j: (l, 0, j)),
                  pl.BlockSpec((None, 1, d), lambda l, j: (l, 0, j))],
        out_specs=pl.BlockSpec((None, rows, d), lambda l, j: (l, 0, j)),
        out_shape=jax.ShapeDtypeStruct((depth, rows, n), F32),
        compiler_params=_params("arbitrary", "arbitrary"),
        name="adaln",
    )(cond, ada_w, ada_b.reshape(depth, 1, n))


def _norm_mod(x, g, mod, shift_row, scale_row):
    return _rms(x) * g * (1.0 + mod[scale_row:scale_row + 1, :]) + mod[shift_row:shift_row + 1, :]


def _lane_tile(tab, n):
    return jnp.tile(tab, (1, n // LANES))


def _norm_proj_kernel(mode, x_ref, g_ref, mod_ref, *refs):
    o_ref = refs[-1]
    n = o_ref.shape[-1]
    h = _norm_mod(x_ref[...], g_ref[...], mod_ref[...], 0, 1)
    if mode == "precise":
        w_hi, w_lo, b = refs[:3]
        h_hi, h_lo = _split2(h)
        acc = _dot(h_hi, w_hi[...]) + _dot(h_lo, w_hi[...]) + _dot(h_hi, w_lo[...]) + b[...]
    else:
        hb = _mx(h)
        if mode == "plain":
            w, b = refs[:2]
            acc = _dot(hb, w[...]) + b[...]
        elif mode == "glu":
            w1, b1, w2, b2 = refs[:4]
            acc = (_dot(hb, w1[...]) + b1[...]) * _sigmoid(_dot(hb, w2[...]) + b2[...])
        elif mode == "tab1":
            w1, b1, t1 = refs[:3]
            acc = (_dot(hb, w1[...]) + b1[...]) * _lane_tile(t1[...], n)
        else:
            w1, b1, t1, w2, b2, t2 = refs[:6]
            acc = ((_dot(hb, w1[...]) + b1[...]) * _lane_tile(t1[...], n)
                   + (_dot(hb, w2[...]) + b2[...]) * _lane_tile(t2[...], n))
    o_ref[...] = acc.astype(o_ref.dtype)


_PROJ_OPERANDS = {"plain": "wb", "precise": "wwb", "glu": "wbwb", "tab1": "wbt", "tab2": "wbtwbt"}


def _norm_proj(x, gain, modtab, geo, mode, operands, tn, out_dtype, name):
    rows, d = x.shape
    kinds = _PROJ_OPERANDS[mode]
    n = operands[0].shape[1]
    specs = [pl.BlockSpec((TILE, d), lambda j, i: (i, 0)),
             pl.BlockSpec((1, d), lambda j, i: (0, 0)),
             pl.BlockSpec((None, ADA_CHUNKS, d), lambda j, i: (geo.mod_all(i), 0, 0))]
    for kind in kinds:
        if kind == "w":
            specs.append(pl.BlockSpec((d, tn), lambda j, i: (0, j)))
        elif kind == "b":
            specs.append(pl.BlockSpec((1, tn), lambda j, i: (0, j)))
        else:
            specs.append(pl.BlockSpec((TILE, LANES), lambda j, i: (i % geo.tpb, 0)))
    return pl.pallas_call(
        functools.partial(_norm_proj_kernel, mode),
        grid=(n // tn, rows // TILE),
        in_specs=specs,
        out_specs=pl.BlockSpec((TILE, tn), lambda j, i: (i, j)),
        out_shape=jax.ShapeDtypeStruct((rows, n), out_dtype),
        compiler_params=_params("arbitrary", "arbitrary"),
        name=name,
    )(x, gain, modtab, *operands)


def _dwconv_kernel(width, mode, tpb, nct, cur_ref, prev_ref, next_ref, w_ref, b_ref, *refs):
    pad_ref = refs[-1]
    o_ref = refs[-2]
    p = pl.program_id(0) % tpb
    has_prev = jnp.logical_and(p != 0, p != nct)
    has_next = jnp.logical_and(p != nct - 1, p != tpb - 1)
    pad_ref[0:HALO, :] = jnp.where(has_prev, prev_ref[...], 0.0)
    pad_ref[HALO:HALO + TILE, :] = cur_ref[...]
    pad_ref[HALO + TILE:, :] = jnp.where(has_next, next_ref[...], 0.0)
    half = (width - 1) // 2
    acc = jnp.broadcast_to(b_ref[...], o_ref.shape)
    for k in range(width):
        off = HALO - half + k
        acc = acc + w_ref[k:k + 1, :] * pad_ref[off:off + TILE, :]
    if mode == "ln_silu":
        g_ref, beta_ref = refs[:2]
        cen = acc - jnp.mean(acc, axis=-1, keepdims=True)
        acc = cen * lax.rsqrt(jnp.mean(cen * cen, axis=-1, keepdims=True) + NORM_EPS) * g_ref[...] + beta_ref[...]
    o_ref[...] = (acc * _sigmoid(acc)).astype(o_ref.dtype)


def _dwconv(u, w, b, geo, mode, extra, tc, out_dtype, name):
    rows, c = u.shape
    width = w.shape[0]
    per = TILE // HALO
    last = rows // HALO - 1
    specs = [pl.BlockSpec((TILE, tc), lambda i, j: (i, j)),
             pl.BlockSpec((HALO, tc), lambda i, j: (jnp.maximum(i * per - 1, 0), j)),
             pl.BlockSpec((HALO, tc), lambda i, j: (jnp.minimum((i + 1) * per, last), j)),
             pl.BlockSpec((width, tc), lambda i, j: (0, j)),
             pl.BlockSpec((1, tc), lambda i, j: (0, j))]
    specs += [pl.BlockSpec((1, tc), lambda i, j: (0, j)) for _ in extra]
    return pl.pallas_call(
        functools.partial(_dwconv_kernel, width, mode, geo.tpb, geo.nct),
        grid=(rows // TILE, c // tc),
        in_specs=specs,
        out_specs=pl.BlockSpec((TILE, tc), lambda i, j: (i, j)),
        out_shape=jax.ShapeDtypeStruct((rows, c), out_dtype),
        scratch_shapes=[pltpu.VMEM((TILE + 2 * HALO, tc), F32)],
        compiler_params=_params("arbitrary", "arbitrary"),
        name=name,
    )(u, u, u, w, b.reshape(1, c), *extra)


def _ssd_kernel(groups, heads_per_group, x_ref, b_ref, c_ref, dt_ref, bias_ref, alog_ref, e_ref, y_ref, h_ref):
    direction = pl.program_id(0)
    step = pl.program_id(2)
    n_state = SSM_STATE
    rp = heads_per_group * HEAD

    @pl.when(step == 0)
    def _():
        h_ref[...] = jnp.zeros_like(h_ref)

    fwd = direction == 0
    row = lax.broadcasted_iota(I32, (CHUNK, CHUNK), 0)
    col = lax.broadcasted_iota(I32, (CHUNK, CHUNK), 1)
    tri = (row - col) * jnp.where(fwd, 1, -1) >= 0
    dtv = _softplus(dt_ref[...] + bias_ref[...])
    a = dtv * (-jnp.exp(alog_ref[...]))
    tri_m = _mx(jnp.where(tri, 1.0, 0.0))
    a3 = _split3(a)
    acum = _dot(tri_m, a3[0]) + _dot(tri_m, a3[1]) + _dot(tri_m, a3[2])
    acum_t = acum.T
    expand = e_ref[...]
    dt_x = _dot_exact_rhs(_split3(dtv), expand)
    ac_x = _dot_exact_rhs(_split3(acum), expand)
    tot_x = jnp.where(fwd, ac_x[CHUNK - 1:CHUNK, :], ac_x[0:1, :])
    xdt = x_ref[...] * dt_x
    xdt_b = _mx(xdt)
    xdt_end_b = _mx(xdt * jnp.exp(tot_x - ac_x))
    e_ac = jnp.exp(ac_x)
    decay = jnp.exp(tot_x)
    lo_half = lax.broadcasted_iota(I32, (CHUNK, LANES), 1) < HEAD
    for g in range(groups):
        bg = b_ref[:, g * n_state:(g + 1) * n_state]
        cb_g = _mx(c_ref[:, g * n_state:(g + 1) * n_state])
        cb = _dot_nt(cb_g, _mx(bg))
        bg_t = _mx(bg.T)
        h_t = h_ref[g]
        cols = slice(g * rp, (g + 1) * rp)
        y_off = _dot(cb_g, _mx(h_t)) * e_ac[:, cols]
        blocks = []
        for pair in range(heads_per_group // 2):
            xb = xdt_b[:, g * rp + pair * LANES:g * rp + (pair + 1) * LANES]
            halves = []
            for hh in range(2):
                c = g * heads_per_group + 2 * pair + hh
                seg = acum[:, c:c + 1] - acum_t[c:c + 1, :]
                within = jnp.exp(jnp.where(tri, seg, -jnp.inf))
                halves.append(_dot(_mx(cb * within), xb))
            blocks.append(jnp.where(lo_half, halves[0], halves[1]))
        y_ref[:, cols] = jnp.concatenate(blocks, axis=1) + y_off
        h_ref[g] = h_t * decay[:, cols] + _dot(bg_t, xdt_end_b[:, cols])


def _ssd(xbc, dt_raw, dt_bias, a_log, geo, d_inner, heads):
    rows = xbc.shape[0]
    groups = SSM_GROUPS
    hpg = heads // groups
    gn = groups * SSM_STATE
    assert d_inner % gn == 0 and hpg % 2 == 0 and heads <= LANES
    cpb, ncc = geo.cpb, geo.ncc

    def rb(d, b, s):
        back = jnp.where(s < ncc, ncc - 1 - s, cpb + ncc - 1 - s)
        return b * cpb + jnp.where(d == 0, s, back)

    pad = LANES - heads
    bias = jnp.pad(dt_bias.astype(F32), ((0, 0), (0, pad))).reshape(2, 1, LANES)
    alog = jnp.pad(a_log.astype(F32), ((0, 0), (0, pad))).reshape(2, 1, LANES)
    expand = (jnp.arange(LANES)[:, None] == jnp.arange(d_inner)[None, :] // HEAD).astype(MXU_DTYPE)
    return pl.pallas_call(
        functools.partial(_ssd_kernel, groups, hpg),
        grid=(2, geo.batch, cpb),
        in_specs=[pl.BlockSpec((CHUNK, d_inner), lambda d, b, s: (rb(d, b, s), 0)),
                  pl.BlockSpec((CHUNK, gn), lambda d, b, s: (rb(d, b, s), d_inner // gn)),
                  pl.BlockSpec((CHUNK, gn), lambda d, b, s: (rb(d, b, s), d_inner // gn + 1)),
                  pl.BlockSpec((CHUNK, LANES), lambda d, b, s: (rb(d, b, s), d)),
                  pl.BlockSpec((None, 1, LANES), lambda d, b, s: (d, 0, 0)),
                  pl.BlockSpec((None, 1, LANES), lambda d, b, s: (d, 0, 0)),
                  pl.BlockSpec((LANES, d_inner), lambda d, b, s: (0, 0))],
        out_specs=pl.BlockSpec((None, CHUNK, d_inner), lambda d, b, s: (d, rb(d, b, s), 0)),
        out_shape=jax.ShapeDtypeStruct((2, rows, d_inner), F32),
        scratch_shapes=[pltpu.VMEM((groups, SSM_STATE, hpg * HEAD), F32)],
        compiler_params=_params("arbitrary", "arbitrary", "arbitrary"),
        name="ssd_scan",
    )(xbc, xbc, xbc, dt_raw, bias, alog, expand)


def _ssm_finish_kernel(y_ref, xs_ref, z_ref, dskip_ref, g_ref, o_ref):
    y = y_ref[0] + y_ref[1] + xs_ref[...] * dskip_ref[...]
    z = z_ref[...]
    o_ref[...] = (_rms(y * (z * _sigmoid(z))) * g_ref[...]).astype(o_ref.dtype)


def _ssm_finish(y2, xbc, z, d_skip_cols, norm_g):
    rows, di = z.shape
    return pl.pallas_call(
        _ssm_finish_kernel,
        grid=(rows // TILE,),
        in_specs=[pl.BlockSpec((2, TILE, di), lambda i: (0, i, 0)),
                  pl.BlockSpec((TILE, di), lambda i: (i, 0)),
                  pl.BlockSpec((TILE, di), lambda i: (i, 0)),
                  pl.BlockSpec((1, di), lambda i: (0, 0)),
                  pl.BlockSpec((1, di), lambda i: (0, 0))],
        out_specs=pl.BlockSpec((TILE, di), lambda i: (i, 0)),
        out_shape=jax.ShapeDtypeStruct((rows, di), MXU_DTYPE),
        compiler_params=_params("arbitrary"),
        name="ssm_finish",
    )(y2, xbc, z, d_skip_cols, norm_g)


def _swa_kernel(n_heads, ncc, n_lat, n_ctx, sink_ref, q_ref, kp_ref, kc_ref, kn_ref, kx_ref,
                vp_ref, vc_ref, vn_ref, vx_ref, o_ref):
    j = pl.program_id(1)
    is_ctx = j < ncc
    start = (j - ncc) * CHUNK
    span = 3 * CHUNK
    qi = lax.broadcasted_iota(I32, (CHUNK, span), 0)
    rel = lax.broadcasted_iota(I32, (CHUNK, span), 1) - SWA_WINDOW
    kpos = start + rel
    ninf = -jnp.inf
    band = jnp.where(jnp.abs(qi - rel) <= SWA_WINDOW,
                     jnp.where(kpos >= 0, jnp.where(kpos < n_lat, 0.0, ninf), ninf), ninf)
    band = jnp.where(is_ctx, ninf, band)
    group = n_heads // SWA_KV_HEADS
    bias = jnp.concatenate([band, jnp.zeros((CHUNK, n_ctx), F32)], axis=1)
    bias = jnp.concatenate([bias] * group, axis=0)
    head_of_row = lax.broadcasted_iota(I32, (group * CHUNK, 1), 0) // CHUNK
    outs = []
    for kh in range(SWA_KV_HEADS):
        lanes = slice(kh * LANES, (kh + 1) * LANES)
        k_all = jnp.concatenate([kp_ref[:, lanes], kc_ref[:, lanes], kn_ref[:, lanes], kx_ref[:, lanes]], axis=0)
        v_all = jnp.concatenate([vp_ref[:, lanes], vc_ref[:, lanes], vn_ref[:, lanes], vx_ref[:, lanes]], axis=0)
        heads = range(kh * group, (kh + 1) * group)
        q = jnp.concatenate([q_ref[:, h * LANES:(h + 1) * LANES] for h in heads], axis=0)
        sink = jnp.zeros((group * CHUNK, 1), F32)
        for g, h in enumerate(heads):
            sink = jnp.where(head_of_row == g, sink_ref[h], sink)
        s = _dot_nt(q, k_all) + bias
        m = jnp.maximum(jnp.max(s, axis=-1, keepdims=True), sink)
        e = jnp.exp(s - m)
        o = _dot(_mx(e), v_all) / (jnp.sum(e, axis=-1, keepdims=True) + jnp.exp(sink - m))
        outs += [o[g * CHUNK:(g + 1) * CHUNK] for g in range(group)]
    for c in range(n_heads // 2):
        o_ref[:, c * LANES:(c + 1) * LANES] = (outs[2 * c] + pltpu.roll(outs[2 * c + 1], HEAD, 1)).astype(o_ref.dtype)


def _swa_attention(uq, kd, vp, sinks, geo, n_heads):
    rows = uq.shape[0]
    cpb, ncc, nlc = geo.cpb, geo.ncc, geo.nlc
    kvw = SWA_KV_HEADS * LANES
    ctx_per = geo.rpb // geo.n_ctx

    def lat_block(shift):
        def index(b, j):
            jl = jnp.clip(j - ncc + shift, 0, nlc - 1)
            return (b * cpb + ncc + jl, 0)
        return pl.BlockSpec((CHUNK, kvw), index)

    ctx_spec = pl.BlockSpec((geo.n_ctx, kvw), lambda b, j: (b * ctx_per, 0))
    kv_specs = [lat_block(-1), lat_block(0), lat_block(1), ctx_spec]
    return pl.pallas_call(
        functools.partial(_swa_kernel, n_heads, ncc, geo.n_lat, geo.n_ctx),
        grid=(geo.batch, cpb),
        in_specs=[pl.BlockSpec(memory_space=pltpu.SMEM),
                  pl.BlockSpec((CHUNK, n_heads * LANES), lambda b, j: (b * cpb + j, 0))] + kv_specs + kv_specs,
        out_specs=pl.BlockSpec((CHUNK, n_heads * HEAD), lambda b, j: (b * cpb + j, 0)),
        out_shape=jax.ShapeDtypeStruct((rows, n_heads * HEAD), MXU_DTYPE),
        compiler_params=_params("arbitrary", "arbitrary"),
        name="swa_attention",
    )(sinks, uq, kd, kd, kd, kd, vp, vp, vp, vp)


def _diff_kernel(lambda_init, q_ref, k_ref, v_ref, lam_ref, g_ref, o_ref):
    lp = lam_ref[...]
    lam = (jnp.exp(jnp.sum(lp[0:1] * lp[1:2], axis=-1, keepdims=True))
           - jnp.exp(jnp.sum(lp[2:3] * lp[3:4], axis=-1, keepdims=True)) + lambda_init)
    for hh in range(DIFF_HEADS_PER_STEP):
        v = v_ref[:, hh * LANES:(hh + 1) * LANES]
        parts = []
        for t in range(2):
            lanes = slice((2 * hh + t) * LANES, (2 * hh + t + 1) * LANES)
            s = _dot_nt(q_ref[:, lanes], k_ref[:, lanes])
            e = jnp.exp(s - jnp.max(s, axis=-1, keepdims=True))
            parts.append(_dot(_mx(e), v) / jnp.sum(e, axis=-1, keepdims=True))
        o = parts[0] - lam * parts[1]
        o_ref[:, hh * LANES:(hh + 1) * LANES] = (_rms(o) * g_ref[...] * (1.0 - lambda_init)).astype(o_ref.dtype)


def _diff_attention(uq, kd, v, lam_params, subln_g, geo, n_heads, lambda_init):
    tq = DIFF_TQ
    hp = DIFF_HEADS_PER_STEP
    nq = geo.n_lat // tq
    upb = geo.rpb // tq
    ucx = geo.n_ctx // tq
    assert n_heads % hp == 0 and 2 * HEAD == LANES
    return pl.pallas_call(
        functools.partial(_diff_kernel, lambda_init),
        grid=(geo.batch, n_heads // hp, nq),
        in_specs=[pl.BlockSpec((tq, 2 * hp * LANES), lambda b, h, j: (b * upb + ucx + j, h)),
                  pl.BlockSpec((geo.rpb, 2 * hp * LANES), lambda b, h, j: (b, h)),
                  pl.BlockSpec((geo.rpb, hp * LANES), lambda b, h, j: (b, h)),
                  pl.BlockSpec(lam_params.shape, lambda b, h, j: (0, 0)),
                  pl.BlockSpec((1, 2 * HEAD), lambda b, h, j: (0, 0))],
        out_specs=pl.BlockSpec((tq, hp * 2 * HEAD), lambda b, h, j: (b * nq + j, h)),
        out_shape=jax.ShapeDtypeStruct((geo.batch * geo.n_lat, n_heads * 2 * HEAD), MXU_DTYPE),
        compiler_params=_params("arbitrary", "arbitrary", "arbitrary"),
        name="diff_attention",
    )(uq, kd, v, lam_params, subln_g)


def _post_kernel(x_ref, a_ref, wo_ref, bo_ref, mod_ref, g_ref, wr_hi_ref, wr_lo_ref, br_ref,
                 xo_ref, h_ref, top_ref, gate_ref):
    mod = mod_ref[...]
    x = x_ref[...] + mod[2:3, :] * (_dot(a_ref[...], wo_ref[...]) + bo_ref[...])
    xo_ref[...] = x
    h = _norm_mod(x, g_ref[...], mod, 3, 4)
    h_ref[...] = h
    h_hi, h_lo = _split2(h)
    logits = _dot(h_hi, wr_hi_ref[...]) + _dot(h_lo, wr_hi_ref[...]) + _dot(h_hi, wr_lo_ref[...]) + br_ref[...]
    lane = lax.broadcasted_iota(I32, logits.shape, 1)
    lane_f = lane.astype(F32)
    top = jnp.zeros(logits.shape, I32)
    gate = jnp.zeros(logits.shape, F32)
    m0 = None
    for k in range(TOP_K):
        m = jnp.max(logits, axis=-1, keepdims=True)
        idx = jnp.min(jnp.where(logits == m, lane_f, float(LANES)), axis=-1, keepdims=True).astype(I32)
        logits = jnp.where(lane == idx, -jnp.inf, logits)
        m0 = m if k == 0 else m0
        top = jnp.where(lane == k, idx, top)
        gate = jnp.where(lane == k, jnp.exp(m - m0), gate)
    top_ref[...] = top
    gate_ref[...] = gate / jnp.sum(gate, axis=-1, keepdims=True)


def _post(x, a, wo, bo, modtab, gain, wr_hi, wr_lo, br, x_tile, mod_index, name):
    rows, din = a.shape
    d = x.shape[1]
    row_spec = pl.BlockSpec((TILE, d), lambda i: (i, 0))
    meta_spec = pl.BlockSpec((TILE, LANES), lambda i: (i, 0))
    const = lambda shape: pl.BlockSpec(shape, lambda i: (0, 0))
    return pl.pallas_call(
        _post_kernel,
        grid=(rows // TILE,),
        in_specs=[pl.BlockSpec((TILE, d), lambda i: (x_tile(i), 0)),
                  pl.BlockSpec((TILE, din), lambda i: (i, 0)),
                  const((din, d)), const((1, d)),
                  pl.BlockSpec((None, ADA_CHUNKS, d), lambda i: (mod_index(i), 0, 0)),
                  const((1, d)), const((d, LANES)), const((d, LANES)), const((1, LANES))],
        out_specs=[row_spec, row_spec, meta_spec, meta_spec],
        out_shape=[jax.ShapeDtypeStruct((rows, d), F32), jax.ShapeDtypeStruct((rows, d), F32),
                   jax.ShapeDtypeStruct((rows, LANES), I32), jax.ShapeDtypeStruct((rows, LANES), F32)],
        compiler_params=_params("arbitrary"),
        name=name,
    )(x, a, wo, bo, modtab, gain, wr_hi, wr_lo, br)


def _moe_pos_kernel(top_ref, pos_ref, meta_ref, count_ref, start_ref):
    phase = pl.program_id(0)
    i = pl.program_id(1)
    top = top_ref[...]
    lane = lax.broadcasted_iota(I32, top.shape, 1)
    onehots = [jnp.where(lane == top[:, k:k + 1], 1.0, 0.0) for k in range(TOP_K)]

    @pl.when(jnp.logical_and(phase == 0, i == 0))
    def _():
        count_ref[...] = jnp.zeros_like(count_ref)

    @pl.when(phase == 0)
    def _():
        tile_count = sum(jnp.sum(o, axis=0, keepdims=True) for o in onehots)
        count_ref[...] = count_ref[...] + tile_count

    @pl.when(jnp.logical_and(phase == 1, i == 0))
    def _():
        counts = count_ref[...]
        padded = jnp.ceil(counts / EXPERT_TILE) * EXPERT_TILE
        lane8 = lax.broadcasted_iota(I32, counts.shape, 1)
        incl = padded
        shift = 1
        while shift < LANES:
            incl = incl + jnp.where(lane8 >= shift, pltpu.roll(incl, shift, 1), 0.0)
            shift *= 2
        start_ref[...] = incl - padded
        sub = lax.broadcasted_iota(I32, counts.shape, 0)
        meta_ref[...] = jnp.where(sub == 0, counts, jnp.where(sub == 1, incl - padded, 0.0))
        count_ref[...] = jnp.zeros_like(count_ref)

    @pl.when(phase == 1)
    def _():
        rows = top.shape[0]
        r = lax.broadcasted_iota(I32, (rows, rows), 0)
        c = lax.broadcasted_iota(I32, (rows, rows), 1)
        strict_lower = _mx(jnp.where(r > c, 1.0, 0.0))
        base = count_ref[0:1, :] + start_ref[0:1, :]
        pos = jnp.zeros(top.shape, I32)
        for k in range(TOP_K):
            before = _dot(strict_lower, _mx(onehots[k])) + base
            slot = jnp.sum(onehots[k] * before, axis=-1, keepdims=True)
            pos = jnp.where(lane == k, slot.astype(I32), pos)
            base = base + jnp.sum(onehots[k], axis=0, keepdims=True)
        pos_ref[...] = pos
        count_ref[...] = jnp.broadcast_to(base - start_ref[0:1, :], count_ref.shape)


def _moe_pos(top):
    rows = top.shape[0]
    return pl.pallas_call(
        _moe_pos_kernel,
        grid=(2, rows // TILE),
        in_specs=[pl.BlockSpec((TILE, LANES), lambda p, i: (i, 0))],
        out_specs=[pl.BlockSpec((TILE, LANES), lambda p, i: (i * p, 0)),
                   pl.BlockSpec((8, LANES), lambda p, i: (0, 0))],
        out_shape=[jax.ShapeDtypeStruct((rows, LANES), I32), jax.ShapeDtypeStruct((8, LANES), F32)],
        scratch_shapes=[pltpu.VMEM((8, LANES), F32), pltpu.VMEM((8, LANES), F32)],
        compiler_params=_params("arbitrary", "arbitrary"),
        name="moe_pos",
    )(top)


def _row_copy(src_ref, src_row, dst_ref, dst_row, sem):
    return pltpu.make_async_copy(src_ref.at[pl.ds(src_row, 1)], dst_ref.at[pl.ds(dst_row, 1)], sem)


def _dispatch_kernel(pos_ref, fill_ref, h_ref, xs_ref, zero_ref, sem):
    rows = h_ref.shape[0]
    base = pl.program_id(0) * (rows * TOP_K)

    @pl.when(pl.program_id(0) == 0)
    def _():
        zero_ref[...] = jnp.zeros_like(zero_ref)

        def fill_copy(j):
            start = pl.multiple_of(jnp.maximum(fill_ref[j], 0), EXPERT_TILE)
            return pltpu.make_async_copy(zero_ref, xs_ref.at[pl.ds(start, EXPERT_TILE)], sem)

        for j in range(fill_ref.shape[0]):
            pl.when(fill_ref[j] >= 0)(lambda j=j: fill_copy(j).start())
        for j in range(fill_ref.shape[0]):
            pl.when(fill_ref[j] >= 0)(lambda j=j: fill_copy(j).wait())

    for t in range(rows):
        for k in range(TOP_K):
            _row_copy(h_ref, t, xs_ref, pos_ref[base + t * TOP_K + k], sem).start(priority=k % 2)
    for k in range(TOP_K):
        pltpu.make_async_copy(h_ref, xs_ref.at[pl.ds(0, rows)], sem).wait()


def _dispatch(pos_flat, fill_rows, h, n_slots):
    rows, d = h.shape
    return pl.pallas_call(
        _dispatch_kernel,
        grid_spec=pltpu.PrefetchScalarGridSpec(
            num_scalar_prefetch=2,
            grid=(rows // DISPATCH_ROWS,),
            in_specs=[pl.BlockSpec((DISPATCH_ROWS, d), lambda i, pos, fill: (i, 0))],
            out_specs=pl.BlockSpec(memory_space=pl.ANY),
            scratch_shapes=[pltpu.VMEM((EXPERT_TILE, d), F32), pltpu.SemaphoreType.DMA(())]),
        out_shape=jax.ShapeDtypeStruct((n_slots, d), F32),
        compiler_params=_params("arbitrary"),
        name="moe_dispatch",
    )(pos_flat, fill_rows, h)


def _expert_kernel(d_expert, te_ref, na_ref, x_ref, wgu_ref, bgu_ref, wd_ref, bd_ref, o_ref, wgu_mx, wd_mx):
    i = pl.program_id(0)
    active = i < na_ref[0]
    new_expert = jnp.logical_or(i == 0, te_ref[i] != te_ref[jnp.maximum(i - 1, 0)])

    @pl.when(jnp.logical_and(active, new_expert))
    def _():
        wgu_mx[...] = _mx(wgu_ref[...])
        wd_mx[...] = _mx(wd_ref[...])

    @pl.when(active)
    def _():
        gu = _dot(_mx(x_ref[...]), wgu_mx[...]) + bgu_ref[...]
        glu = jnp.minimum(gu[:, :d_expert], SWIGLU_LIMIT)
        lin = jnp.clip(gu[:, d_expert:], -SWIGLU_LIMIT, SWIGLU_LIMIT)
        act = glu * _sigmoid(SWIGLU_ALPHA * glu) * (lin + 1.0)
        o_ref[...] = (_dot(_mx(act), wd_mx[...]) + bd_ref[...])[:, None, :]

    @pl.when(jnp.logical_not(active))
    def _():
        o_ref[...] = jnp.zeros_like(o_ref)


def _experts(layer, tile_expert, n_active, xs, w_gu, b_gu, w_down, b_down):
    n_slots, d = xs.shape
    depth, n_exp, _, two_de = w_gu.shape
    de = two_de // 2
    n_tiles = n_slots // EXPERT_TILE
    row = lambda i, te, na: (jnp.minimum(i, na[0] - 1), 0)
    by_expert = lambda i, te, na: (layer, te[i], 0, 0)
    return pl.pallas_call(
        functools.partial(_expert_kernel, de),
        grid_spec=pltpu.PrefetchScalarGridSpec(
            num_scalar_prefetch=2,
            grid=(n_tiles,),
            in_specs=[pl.BlockSpec((EXPERT_TILE, d), row),
                      pl.BlockSpec((None, None, d, two_de), by_expert),
                      pl.BlockSpec((None, None, 1, two_de), by_expert),
                      pl.BlockSpec((None, None, de, d), by_expert),
                      pl.BlockSpec((None, None, 1, d), by_expert)],
            out_specs=pl.BlockSpec((EXPERT_TILE, 1, d), lambda i, te, na: (i, 0, 0)),
            scratch_shapes=[pltpu.VMEM((d, two_de), MXU_DTYPE), pltpu.VMEM((de, d), MXU_DTYPE)]),
        out_shape=jax.ShapeDtypeStruct((n_slots, 1, d), F32),
        compiler_params=_params("arbitrary"),
        name="moe_experts",
    )(tile_expert, n_active, xs, w_gu, b_gu.reshape(depth, n_exp, 1, two_de), w_down,
      b_down.reshape(depth, n_exp, 1, d))


def _combine_kernel(final, n_tiles, pos_ref, x_ref, gate_ref, mod_ref, gfin_ref, ys_ref, o_ref, buf, sem):
    i = pl.program_id(0)

    def issue(tile, slot):
        base = tile * (TILE * TOP_K)
        for t in range(TILE):
            for k in range(TOP_K):
                _row_copy(ys_ref, pos_ref[base + t * TOP_K + k], buf.at[slot, k], t,
                          sem.at[slot]).start(priority=k % 2)

    pl.when(i == 0)(lambda: issue(0, 0))
    for slot in range(2):
        pl.when(jnp.logical_and(i + 1 < n_tiles, (i + 1) % 2 == slot))(lambda slot=slot: issue(i + 1, slot))

    slot = i % 2
    for k in range(TOP_K):
        pltpu.make_async_copy(ys_ref.at[pl.ds(0, TILE)], buf.at[slot, k], sem.at[slot]).wait()
    gates = gate_ref[...]
    f = gates[:, 0:1] * buf[slot, 0].reshape(x_ref.shape)
    for k in range(1, TOP_K):
        f = f + gates[:, k:k + 1] * buf[slot, k].reshape(x_ref.shape)
    x = x_ref[...] + mod_ref[5:6, :] * f
    if final:
        x = _rms(x) * gfin_ref[...]
    o_ref[...] = x


def _combine(pos_flat, x, gates, modtab, g_final, ys, mod_index, final):
    rows, d = x.shape
    return pl.pallas_call(
        functools.partial(_combine_kernel, final, rows // TILE),
        grid_spec=pltpu.PrefetchScalarGridSpec(
            num_scalar_prefetch=1,
            grid=(rows // TILE,),
            in_specs=[pl.BlockSpec((TILE, d), lambda i, pos: (i, 0)),
                      pl.BlockSpec((TILE, LANES), lambda i, pos: (i, 0)),
                      pl.BlockSpec((None, ADA_CHUNKS, d), lambda i, pos: (mod_index(i), 0, 0)),
                      pl.BlockSpec((1, d), lambda i, pos: (0, 0)),
                      pl.BlockSpec(memory_space=pl.ANY)],
            out_specs=pl.BlockSpec((TILE, d), lambda i, pos: (i, 0)),
            scratch_shapes=[pltpu.VMEM((2, TOP_K, TILE, 1, d), F32), pltpu.SemaphoreType.DMA((2,))]),
        out_shape=jax.ShapeDtypeStruct((rows, d), F32),
        compiler_params=_params("arbitrary"),
        name="moe_combine",
    )(pos_flat, x, gates, modtab, g_final, ys)


def _moe(layer, x, h, top, gates, modtab, mod_index, w_gu, b_gu, w_down, b_down, g_final, final):
    rows = h.shape[0]
    n_exp = w_gu.shape[1]
    pos, meta = _moe_pos(top)
    counts = meta[0, :n_exp].astype(I32)
    starts = meta[1, :n_exp].astype(I32)
    n_tiles = rows * TOP_K // EXPERT_TILE + n_exp
    tiles_per = (counts + EXPERT_TILE - 1) // EXPERT_TILE
    tile_start = starts // EXPERT_TILE
    tile_end = tile_start + tiles_per
    n_active = tile_end[-1]
    tile = jnp.minimum(jnp.arange(n_tiles, dtype=I32), n_active - 1)
    tile_expert = jnp.sum(tile_end[None, :] <= tile[:, None], axis=1).astype(I32)
    pos_flat = pos[:, :TOP_K].reshape(-1)
    last_tile = jnp.where(tiles_per > 0, tile_end - 1, -1)
    tail_tile = n_active + jnp.arange(n_exp, dtype=I32)
    tail_tile = jnp.where(tail_tile < n_tiles, tail_tile, -1)
    fill_tiles = jnp.concatenate([last_tile, tail_tile])
    fill_rows = jnp.where(fill_tiles >= 0, fill_tiles * EXPERT_TILE, -1).astype(I32)
    xs = _dispatch(pos_flat, fill_rows, h, n_tiles * EXPERT_TILE)
    ys = _experts(layer, tile_expert, n_active.reshape(1), xs, w_gu, b_gu, w_down, b_down)
    return _combine(pos_flat, x, gates, modtab, g_final, ys, mod_index, final)


def _rot_cols(w):
    lead = w.shape[:-1]
    blocks = w.reshape(lead + (-1, 2, HEAD // 2))
    return jnp.concatenate([-blocks[..., 1:2, :], blocks[..., 0:1, :]], axis=-2).reshape(w.shape)


def _pair_cols(a, b):
    lead = a.shape[:-1]
    a3 = a.reshape(lead + (-1, HEAD))
    b3 = b.reshape(lead + (-1, HEAD))
    return jnp.concatenate([a3, b3], axis=-1).reshape(lead + (-1,))


def _rope_tables(geo):
    t = jnp.arange(geo.n_lat)
    rowp = (t // GRID_W).astype(F32)
    colp = (t % GRID_W).astype(F32)
    quarter = HEAD // 4
    inv_freq = ROPE_BASE ** (-jnp.arange(quarter, dtype=F32) / quarter)
    ang = jnp.concatenate([rowp[:, None] * inv_freq, colp[:, None] * inv_freq], axis=-1)
    cos = jnp.concatenate([jnp.ones((geo.n_ctx, HEAD // 2), F32), jnp.cos(ang)], axis=0)
    sin = jnp.concatenate([jnp.zeros((geo.n_ctx, HEAD // 2), F32), jnp.sin(ang)], axis=0)
    cos64 = jnp.concatenate([cos, cos], axis=-1)
    sin64 = jnp.concatenate([sin, sin], axis=-1)
    scale = HEAD ** -0.5
    q_tab = jnp.concatenate([cos64, sin64], axis=-1) * scale
    k_cos = jnp.concatenate([cos64, cos64], axis=-1)
    k_sin = jnp.concatenate([sin64, sin64], axis=-1)
    return q_tab, k_cos, k_sin


def _row(v):
    return v.reshape(1, -1).astype(F32)


def kernel(x, c, ctx, c_ctx, ada_w, ada_b, g_mix, g_ffn, g_final, conv_w_pw1, conv_b_pw1, conv_w_dw, conv_b_dw, conv_ln_g, conv_ln_b, conv_w_pw2, conv_b_pw2, ssm_w_in, ssm_w_conv, ssm_b_conv, ssm_a_log, ssm_dt_bias, ssm_d, ssm_norm_g, ssm_w_out, swa_w_qkv, swa_b_qkv, swa_sinks, swa_w_o, swa_b_o, diff_w_qkv, diff_lambda_q1, diff_lambda_k1, diff_lambda_q2, diff_lambda_k2, diff_subln_g, diff_w_o, moe_w_router, moe_b_router, moe_w_gu, moe_b_gu, moe_w_down, moe_b_down):
    batch, n_lat, d = x.shape
    n_ctx = ctx.shape[1]
    depth = ada_w.shape[0]
    geo = _Geo(batch, n_ctx, n_lat)
    n_exp = moe_w_router.shape[-1]
    q_tab, k_cos, k_sin = _rope_tables(geo)

    cond_rows = 16
    assert batch + 1 <= cond_rows
    cond = jnp.zeros((cond_rows, d), F32).at[:batch].set(c).at[batch].set(c_ctx)
    ada = _adaln(cond, ada_w, ada_b)
    mod_lat = ada[:, :batch].reshape(depth, batch, ADA_CHUNKS, d)
    mod_ctx = jnp.broadcast_to(ada[:, batch].reshape(depth, 1, ADA_CHUNKS, d), mod_lat.shape)
    modtabs = jnp.stack([mod_ctx, mod_lat], axis=2).reshape(depth, 2 * batch, ADA_CHUNKS, d)

    xs = jnp.concatenate([ctx, x], axis=1).reshape(geo.rows, d)
    zero_bias = jnp.zeros((1, d), F32)
    out = None
    for i in range(depth):
        kind, j = i % 4, i // 4
        ctx_out = i < depth - 1
        modtab = modtabs[i]
        gain = _row(g_mix[i])
        if kind == 0:
            w1 = _mx(conv_w_pw1[j])
            b1 = _row(conv_b_pw1[j])
            u = _norm_proj(xs, gain, modtab, geo, "glu", [w1[:, :d], b1[:, :d], w1[:, d:], b1[:, d:]], d, F32,
                           "conv_pw1_glu")
            a = _dwconv(u, conv_w_dw[j], conv_b_dw[j], geo, "ln_silu", [_row(conv_ln_g[j]), _row(conv_ln_b[j])],
                        d, MXU_DTYPE, "conv_dw_ln")
            wo, bo = _mx(conv_w_pw2[j]), _row(conv_b_pw2[j])
        elif kind == 1:
            di = ssm_norm_g.shape[-1]
            heads = ssm_a_log.shape[-1]
            conv_dim = ssm_w_conv.shape[-1]
            w_in = ssm_w_in[j]
            z = _norm_proj(xs, gain, modtab, geo, "plain", [_mx(w_in[:, :di]), jnp.zeros((1, di), F32)], di, F32,
                           "ssm_in_z")
            tn = 1024 if conv_dim % 1024 == 0 else 512
            xbc = _norm_proj(xs, gain, modtab, geo, "plain",
                             [_mx(w_in[:, di:di + conv_dim]), jnp.zeros((1, conv_dim), F32)], conv_dim, F32,
                             "ssm_in_xbc")
            w_dt = w_in[:, di + conv_dim:].reshape(d, 2, heads)
            w_dt = jnp.pad(w_dt, ((0, 0), (0, 0), (0, LANES - heads))).reshape(d, 2 * LANES)
            w_dt_hi = _mx(w_dt)
            w_dt_lo = _mx(w_dt - w_dt_hi.astype(F32))
            dt_raw = _norm_proj(xs, gain, modtab, geo, "precise", [w_dt_hi, w_dt_lo, jnp.zeros((1, 2 * LANES), F32)],
                                2 * LANES, F32, "ssm_in_dt")
            xbc = _dwconv(xbc, ssm_w_conv[j], ssm_b_conv[j], geo, "silu", [], tn, F32, "ssm_conv")
            y2 = _ssd(xbc, dt_raw, ssm_dt_bias[j], ssm_a_log[j], geo, di, heads)
            a = _ssm_finish(y2, xbc, z, _row(jnp.repeat(ssm_d[j], HEAD)), _row(ssm_norm_g[j]))
            wo, bo = _mx(ssm_w_out[j]), zero_bias
        elif kind == 2:
            nh = swa_sinks.shape[-1]
            nq, nkv = nh * HEAD, SWA_KV_HEADS * HEAD
            w, b = swa_w_qkv[j], swa_b_qkv[j][None, :]
            wq, wk, wv = w[:, :nq], w[:, nq:nq + nkv], w[:, nq + nkv:]
            bq, bk, bv = b[:, :nq], b[:, nq:nq + nkv], b[:, nq + nkv:]
            uq = _norm_proj(xs, gain, modtab, geo, "tab1",
                            [_mx(_pair_cols(wq, _rot_cols(wq))), _pair_cols(bq, _rot_cols(bq)), q_tab],
                            2 * nq, MXU_DTYPE, "swa_q")
            kd = _norm_proj(xs, gain, modtab, geo, "tab2",
                            [_mx(_pair_cols(wk, wk)), _pair_cols(bk, bk), k_cos,
                             _mx(_pair_cols(_rot_cols(wk), _rot_cols(wk))), _pair_cols(_rot_cols(bk), _rot_cols(bk)),
                             k_sin], 2 * nkv, MXU_DTYPE, "swa_k")
            vp = _norm_proj(xs, gain, modtab, geo, "plain",
                            [_mx(_pair_cols(wv, jnp.zeros_like(wv))), _pair_cols(bv, jnp.zeros_like(bv))],
                            2 * nkv, MXU_DTYPE, "swa_v")
            a = _swa_attention(uq, kd, vp, swa_sinks[j].astype(F32), geo, nh)
            wo, bo = _mx(swa_w_o[j]), _row(swa_b_o[j])
        else:
            assert not ctx_out, "differential attention is only built for a layer without context output"
            lambda_init = 0.8 - 0.6 * math.exp(-0.3 * i)
            w = diff_w_qkv[j]
            wq, wk, wv = w[:, :d], w[:, d:2 * d], w[:, 2 * d:]
            zb = jnp.zeros((1, 2 * d), F32)
            uq = _norm_proj(xs, gain, modtab, geo, "tab1", [_mx(_pair_cols(wq, _rot_cols(wq))), zb, q_tab],
                            2 * d, MXU_DTYPE, "diff_q")
            kd = _norm_proj(xs, gain, modtab, geo, "tab2",
                            [_mx(_pair_cols(wk, wk)), zb, k_cos,
                             _mx(_pair_cols(_rot_cols(wk), _rot_cols(wk))), zb, k_sin], 2 * d, MXU_DTYPE, "diff_k")
            v = _norm_proj(xs, gain, modtab, geo, "plain", [_mx(wv), zero_bias], d, MXU_DTYPE, "diff_v")
            lam_params = jnp.stack([diff_lambda_q1[j], diff_lambda_k1[j], diff_lambda_q2[j],
                                    diff_lambda_k2[j]]).astype(F32)
            a = _diff_attention(uq, kd, v, lam_params, _row(diff_subln_g[j]), geo, d // (2 * HEAD), lambda_init)
            wo, bo = _mx(diff_w_o[j]), zero_bias

        wr = jnp.pad(moe_w_router[i], ((0, 0), (0, LANES - n_exp)))
        wr_hi = _mx(wr)
        wr_lo = _mx(wr - wr_hi.astype(F32))
        br = jnp.pad(_row(moe_b_router[i]), ((0, 0), (0, LANES - n_exp)), constant_values=-1e30)
        if ctx_out:
            x_tile, mod_index = (lambda t: t), geo.mod_all
        else:
            x_tile, mod_index = geo.lat_tile, geo.mod_lat
            if a.shape[0] == geo.rows:
                a = a.reshape(batch, geo.rpb, -1)[:, n_ctx:].reshape(batch * n_lat, -1)
        xs, h, top, gates = _post(xs, a, wo, bo, modtab, _row(g_ffn[i]), wr_hi, wr_lo, br, x_tile, mod_index,
                                  "post_mixer")
        final = i == depth - 1
        xs = _moe(i, xs, h, top, gates, modtab, mod_index, moe_w_gu, moe_b_gu, moe_w_down, moe_b_down,
                  _row(g_final), final)
        if not ctx_out and not final:
            raise NotImplementedError("a layer without context output must be the last layer")
        out = xs
    return out.reshape(batch, n_lat, d)
```

```python
import functools
import math

import jax
import jax.numpy as jnp
from jax import lax
from jax.experimental import pallas as pl
from jax.experimental.pallas import tpu as pltpu

F32 = jnp.float32
I32 = jnp.int32
MXU_DTYPE = jnp.bfloat16

LANES = 128
SUBLANES = 8
VMEM_LIMIT_BYTES = 56 * 1024 * 1024

TILE = 256
CHUNK = 128
HALO = 16
HEAD = 64
NORM_EPS = 1e-6
ROPE_BASE = 10000.0
GRID_W = 64
ADA_CHUNKS = 6
SSM_GROUPS = 4
SSM_STATE = 128
SWA_KV_HEADS = 4
SWA_WINDOW = 128
TOP_K = 4
SWIGLU_LIMIT = 7.0
SWIGLU_ALPHA = 1.702
EXPERT_TILE = 512
POS_ROWS = 1024
DISPATCH_ROWS = 512
DIFF_TQ = 256
DIFF_HEADS_PER_STEP = 2


def _mx(v):
    return v.astype(MXU_DTYPE)


def _dot(a, b):
    return jnp.dot(a, b, preferred_element_type=F32)


def _dot_nt(a, b):
    return lax.dot_general(a, b, (((1,), (1,)), ((), ())), preferred_element_type=F32)


def _split2(v):
    hi = _mx(v)
    return hi, _mx(v - hi.astype(F32))


def _split3(v):
    hi = _mx(v)
    r = v - hi.astype(F32)
    mid = _mx(r)
    return hi, mid, _mx(r - mid.astype(F32))


def _dot_exact_rhs(parts, m):
    acc = _dot(parts[0], m)
    for p in parts[1:]:
        acc = acc + _dot(p, m)
    return acc


def _sigmoid(v):
    return 1.0 / (1.0 + jnp.exp(-v))


def _softplus(v):
    return jnp.maximum(v, 0.0) + jnp.log(1.0 + jnp.exp(-jnp.abs(v)))


def _rms(v):
    return v * lax.rsqrt(jnp.mean(v * v, axis=-1, keepdims=True) + NORM_EPS)


def _params(*sem):
    return pltpu.CompilerParams(dimension_semantics=sem, vmem_limit_bytes=VMEM_LIMIT_BYTES)


class _Geo:
    def __init__(self, batch, n_ctx, n_lat):
        assert n_ctx % TILE == 0 and n_lat % TILE == 0 and n_lat % n_ctx == 0
        self.batch, self.n_ctx, self.n_lat = batch, n_ctx, n_lat
        self.rpb = n_ctx + n_lat
        self.tpb = self.rpb // TILE
        self.nct = n_ctx // TILE
        self.nlt = n_lat // TILE
        self.cpb = self.rpb // CHUNK
        self.ncc = n_ctx // CHUNK
        self.nlc = n_lat // CHUNK
        self.rows = batch * self.rpb

    def mod_all(self, i):
        return (i // self.tpb) * 2 + (i % self.tpb >= self.nct).astype(I32)

    def lat_tile(self, i):
        return (i // self.nlt) * self.tpb + self.nct + i % self.nlt

    def mod_lat(self, i):
        return (i // self.nlt) * 2 + 1


def _adaln_kernel(c_ref, w_ref, b_ref, o_ref):
    c = c_ref[...]
    s_hi, s_lo = _split2(c * _sigmoid(c))
    w_hi, w_lo = _split2(w_ref[...])
    o_ref[...] = _dot(s_hi, w_hi) + _dot(s_lo, w_hi) + _dot(s_hi, w_lo) + b_ref[...]


def _adaln(cond, ada_w, ada_b):
    depth, d, n = ada_w.shape
    rows = cond.shape[0]
    return pl.pallas_call(
        _adaln_kernel,
        grid=(depth, n // d),
        in_specs=[pl.BlockSpec((rows, d), lambda l, j: (0, 0)),
                  pl.BlockSpec((None, d, d), lambda l, j: (l, 0, j)),
                  pl.BlockSpec((None, 1, d), lambda l, j: (l, 0, j))],
        out_specs=pl.BlockSpec((None, rows, d), lambda l, j: (l, 0, j)),
        out_shape=jax.ShapeDtypeStruct((depth, rows, n), F32),
        compiler_params=_params("arbitrary", "arbitrary"),
        name="adaln",
    )(cond, ada_w, ada_b.reshape(depth, 1, n))


def _norm_mod(x, g, mod, shift_row, scale_row):
    return _rms(x) * g * (1.0 + mod[scale_row:scale_row + 1, :]) + mod[shift_row:shift_row + 1, :]


def _lane_tile(tab, n):
    return jnp.tile(tab, (1, n // LANES))


_PROJ_OPERANDS = {"plain": "wb", "precise": "wwb", "glu": "wbwb", "tab1": "wbt", "tab2": "wbtwbt"}


def _norm_proj_kernel(modes, x_ref, g_ref, mod_ref, *refs):
    out_refs = refs[len(refs) - len(modes):]
    h = _norm_mod(x_ref[...], g_ref[...], mod_ref[...], 0, 1)
    hb = _mx(h)
    at = 0
    for mode, o_ref in zip(modes, out_refs):
        ops = refs[at:at + len(_PROJ_OPERANDS[mode])]
        at += len(ops)
        n = o_ref.shape[-1]
        if mode == "precise":
            w_hi, w_lo, b = ops
            h_lo = _mx(h - hb.astype(F32))
            acc = _dot(hb, w_hi[...]) + _dot(h_lo, w_hi[...]) + _dot(hb, w_lo[...]) + b[...]
        elif mode == "plain":
            w, b = ops
            acc = _dot(hb, w[...]) + b[...]
        elif mode == "glu":
            w1, b1, w2, b2 = ops
            acc = (_dot(hb, w1[...]) + b1[...]) * _sigmoid(_dot(hb, w2[...]) + b2[...])
        elif mode == "tab1":
            w1, b1, t1 = ops
            acc = (_dot(hb, w1[...]) + b1[...]) * _lane_tile(t1[...], n)
        else:
            w1, b1, t1, w2, b2, t2 = ops
            acc = ((_dot(hb, w1[...]) + b1[...]) * _lane_tile(t1[...], n)
                   + (_dot(hb, w2[...]) + b2[...]) * _lane_tile(t2[...], n))
        o_ref[...] = acc.astype(o_ref.dtype)


def _norm_proj(x, gain, modtab, geo, groups, name):
    rows, d = x.shape
    specs = [pl.BlockSpec((TILE, d), lambda i: (i, 0)),
             pl.BlockSpec((1, d), lambda i: (0, 0)),
             pl.BlockSpec((None, ADA_CHUNKS, d), lambda i: (geo.mod_all(i), 0, 0))]
    operands, out_specs, out_shapes = [], [], []
    for mode, ops, out_dtype in groups:
        n = ops[0].shape[1]
        for kind, op in zip(_PROJ_OPERANDS[mode], ops):
            if kind == "t":
                specs.append(pl.BlockSpec((TILE, LANES), lambda i: (i % geo.tpb, 0)))
            else:
                specs.append(pl.BlockSpec(op.shape, lambda i: (0, 0)))
        operands += ops
        out_specs.append(pl.BlockSpec((TILE, n), lambda i: (i, 0)))
        out_shapes.append(jax.ShapeDtypeStruct((rows, n), out_dtype))
    return pl.pallas_call(
        functools.partial(_norm_proj_kernel, tuple(g[0] for g in groups)),
        grid=(rows // TILE,),
        in_specs=specs,
        out_specs=out_specs,
        out_shape=out_shapes,
        compiler_params=_params("arbitrary"),
        name=name,
    )(x, gain, modtab, *operands)


def _dwconv_kernel(width, mode, tpb, nct, cur_ref, prev_ref, next_ref, w_ref, b_ref, *refs):
    shifted = width > SUBLANES
    pad_ref = refs[-2] if shifted else refs[-1]
    o_ref = refs[-3] if shifted else refs[-2]
    p = pl.program_id(0) % tpb
    has_prev = jnp.logical_and(p != 0, p != nct)
    has_next = jnp.logical_and(p != nct - 1, p != tpb - 1)
    pad_ref[0:HALO, :] = jnp.where(has_prev, prev_ref[...], 0.0)
    pad_ref[HALO:HALO + TILE, :] = cur_ref[...]
    pad_ref[HALO + TILE:, :] = jnp.where(has_next, next_ref[...], 0.0)
    half = (width - 1) // 2
    acc = jnp.broadcast_to(b_ref[...], o_ref.shape)
    if shifted:
        sh_ref = refs[-1]
        span = sh_ref.shape[1]
        for s in range(1, SUBLANES):
            sh_ref[s - 1] = pad_ref[s:s + span, :]
    for k in range(width):
        off = HALO - half + k
        if shifted and off % SUBLANES:
            base = off - off % SUBLANES
            tap = sh_ref[off % SUBLANES - 1, base:base + TILE, :]
        else:
            tap = pad_ref[off:off + TILE, :]
        acc = acc + w_ref[k:k + 1, :] * tap
    if mode == "ln_silu":
        g_ref, beta_ref = refs[:2]
        cen = acc - jnp.mean(acc, axis=-1, keepdims=True)
        acc = cen * lax.rsqrt(jnp.mean(cen * cen, axis=-1, keepdims=True) + NORM_EPS) * g_ref[...] + beta_ref[...]
    o_ref[...] = (acc * _sigmoid(acc)).astype(o_ref.dtype)


def _dwconv(u, w, b, geo, mode, extra, tc, out_dtype, name):
    rows, c = u.shape
    width = w.shape[0]
    per = TILE // HALO
    last = rows // HALO - 1
    specs = [pl.BlockSpec((TILE, tc), lambda i, j: (i, j)),
             pl.BlockSpec((HALO, tc), lambda i, j: (jnp.maximum(i * per - 1, 0), j)),
             pl.BlockSpec((HALO, tc), lambda i, j: (jnp.minimum((i + 1) * per, last), j)),
             pl.BlockSpec((width, tc), lambda i, j: (0, j)),
             pl.BlockSpec((1, tc), lambda i, j: (0, j))]
    specs += [pl.BlockSpec((1, tc), lambda i, j: (0, j)) for _ in extra]
    scratch = [pltpu.VMEM((TILE + 2 * HALO, tc), F32)]
    if width > SUBLANES:
        scratch.append(pltpu.VMEM((SUBLANES - 1, TILE + 2 * HALO - SUBLANES, tc), F32))
    return pl.pallas_call(
        functools.partial(_dwconv_kernel, width, mode, geo.tpb, geo.nct),
        grid=(rows // TILE, c // tc),
        in_specs=specs,
        out_specs=pl.BlockSpec((TILE, tc), lambda i, j: (i, j)),
        out_shape=jax.ShapeDtypeStruct((rows, c), out_dtype),
        scratch_shapes=scratch,
        compiler_params=_params("arbitrary", "arbitrary"),
        name=name,
    )(u, u, u, w, b.reshape(1, c), *extra)


def _ssd_kernel(groups, heads_per_group, x_ref, b_ref, c_ref, dt_ref, bias_ref, alog_ref, e_ref, y_ref, h_ref):
    direction = pl.program_id(0)
    step = pl.program_id(2)
    n_state = SSM_STATE
    rp = heads_per_group * HEAD

    @pl.when(step == 0)
    def _():
        h_ref[...] = jnp.zeros_like(h_ref)

    fwd = direction == 0
    row = lax.broadcasted_iota(I32, (CHUNK, CHUNK), 0)
    col = lax.broadcasted_iota(I32, (CHUNK, CHUNK), 1)
    tri = (row - col) * jnp.where(fwd, 1, -1) >= 0
    dtv = _softplus(dt_ref[...] + bias_ref[...])
    a = dtv * (-jnp.exp(alog_ref[...]))
    tri_m = _mx(jnp.where(tri, 1.0, 0.0))
    a3 = _split3(a)
    acum = _dot(tri_m, a3[0]) + _dot(tri_m, a3[1]) + _dot(tri_m, a3[2])
    acum_t = acum.T
    expand = e_ref[...]
    dt_x = _dot_exact_rhs(_split2(dtv), expand)
    ac_x = _dot_exact_rhs(_split3(acum), expand)
    tot_x = jnp.where(fwd, ac_x[CHUNK - 1:CHUNK, :], ac_x[0:1, :])
    xdt = x_ref[...] * dt_x
    xdt_b = _mx(xdt)
    xdt_end_b = _mx(xdt * jnp.exp(tot_x - ac_x))
    e_ac = jnp.exp(ac_x)
    decay = jnp.exp(tot_x)
    lo_half = lax.broadcasted_iota(I32, (CHUNK, LANES), 1) < HEAD
    for g in range(groups):
        bg = b_ref[:, g * n_state:(g + 1) * n_state]
        cb_g = _mx(c_ref[:, g * n_state:(g + 1) * n_state])
        cb = _dot_nt(cb_g, _mx(bg))
        bg_t = _mx(bg.T)
        h_t = h_ref[g]
        cols = slice(g * rp, (g + 1) * rp)
        y_off = _dot(cb_g, _mx(h_t)) * e_ac[:, cols]
        blocks = []
        for pair in range(heads_per_group // 2):
            xb = xdt_b[:, g * rp + pair * LANES:g * rp + (pair + 1) * LANES]
            halves = []
            for hh in range(2):
                c = g * heads_per_group + 2 * pair + hh
                seg = acum[:, c:c + 1] - acum_t[c:c + 1, :]
                within = jnp.exp(jnp.where(tri, seg, -jnp.inf))
                halves.append(_dot(_mx(cb * within), xb))
            blocks.append(jnp.where(lo_half, halves[0], halves[1]))
        y_ref[:, cols] = jnp.concatenate(blocks, axis=1) + y_off
        h_ref[g] = h_t * decay[:, cols] + _dot(bg_t, xdt_end_b[:, cols])


def _ssd(xbc, dt_raw, dt_bias, a_log, geo, d_inner, heads):
    rows = xbc.shape[0]
    groups = SSM_GROUPS
    hpg = heads // groups
    gn = groups * SSM_STATE
    assert d_inner % gn == 0 and hpg % 2 == 0 and heads <= LANES
    cpb, ncc = geo.cpb, geo.ncc

    def rb(d, b, s):
        back = jnp.where(s < ncc, ncc - 1 - s, cpb + ncc - 1 - s)
        return b * cpb + jnp.where(d == 0, s, back)

    pad = LANES - heads
    bias = jnp.pad(dt_bias.astype(F32), ((0, 0), (0, pad))).reshape(2, 1, LANES)
    alog = jnp.pad(a_log.astype(F32), ((0, 0), (0, pad))).reshape(2, 1, LANES)
    expand = (jnp.arange(LANES)[:, None] == jnp.arange(d_inner)[None, :] // HEAD).astype(MXU_DTYPE)
    return pl.pallas_call(
        functools.partial(_ssd_kernel, groups, hpg),
        grid=(2, geo.batch, cpb),
        in_specs=[pl.BlockSpec((CHUNK, d_inner), lambda d, b, s: (rb(d, b, s), 0)),
                  pl.BlockSpec((CHUNK, gn), lambda d, b, s: (rb(d, b, s), d_inner // gn)),
                  pl.BlockSpec((CHUNK, gn), lambda d, b, s: (rb(d, b, s), d_inner // gn + 1)),
                  pl.BlockSpec((CHUNK, LANES), lambda d, b, s: (rb(d, b, s), d)),
                  pl.BlockSpec((None, 1, LANES), lambda d, b, s: (d, 0, 0)),
                  pl.BlockSpec((None, 1, LANES), lambda d, b, s: (d, 0, 0)),
                  pl.BlockSpec((LANES, d_inner), lambda d, b, s: (0, 0))],
        out_specs=pl.BlockSpec((None, CHUNK, d_inner), lambda d, b, s: (d, rb(d, b, s), 0)),
        out_shape=jax.ShapeDtypeStruct((2, rows, d_inner), F32),
        scratch_shapes=[pltpu.VMEM((groups, SSM_STATE, hpg * HEAD), F32)],
        compiler_params=_params("arbitrary", "arbitrary", "arbitrary"),
        name="ssd_scan",
    )(xbc, xbc, xbc, dt_raw, bias, alog, expand)


def _ssm_finish_kernel(y_ref, xs_ref, z_ref, dskip_ref, g_ref, o_ref):
    y = y_ref[0] + y_ref[1] + xs_ref[...] * dskip_ref[...]
    z = z_ref[...]
    o_ref[...] = (_rms(y * (z * _sigmoid(z))) * g_ref[...]).astype(o_ref.dtype)


def _ssm_finish(y2, xbc, z, d_skip_cols, norm_g):
    rows, di = z.shape
    return pl.pallas_call(
        _ssm_finish_kernel,
        grid=(rows // TILE,),
        in_specs=[pl.BlockSpec((2, TILE, di), lambda i: (0, i, 0)),
                  pl.BlockSpec((TILE, di), lambda i: (i, 0)),
                  pl.BlockSpec((TILE, di), lambda i: (i, 0)),
                  pl.BlockSpec((1, di), lambda i: (0, 0)),
                  pl.BlockSpec((1, di), lambda i: (0, 0))],
        out_specs=pl.BlockSpec((TILE, di), lambda i: (i, 0)),
        out_shape=jax.ShapeDtypeStruct((rows, di), MXU_DTYPE),
        compiler_params=_params("arbitrary"),
        name="ssm_finish",
    )(y2, xbc, z, d_skip_cols, norm_g)


def _swa_kernel(n_heads, ncc, n_lat, n_ctx, sink_ref, q_ref, kp_ref, kc_ref, kn_ref, kx_ref,
                vp_ref, vc_ref, vn_ref, vx_ref, o_ref):
    j = pl.program_id(1)
    is_ctx = j < ncc
    start = (j - ncc) * CHUNK
    span = 3 * CHUNK
    qi = lax.broadcasted_iota(I32, (CHUNK, span), 0)
    rel = lax.broadcasted_iota(I32, (CHUNK, span), 1) - SWA_WINDOW
    kpos = start + rel
    ninf = -jnp.inf
    band = jnp.where(jnp.abs(qi - rel) <= SWA_WINDOW,
                     jnp.where(kpos >= 0, jnp.where(kpos < n_lat, 0.0, ninf), ninf), ninf)
    band = jnp.where(is_ctx, ninf, band)
    group = n_heads // SWA_KV_HEADS
    bias = jnp.concatenate([band, jnp.zeros((CHUNK, n_ctx), F32)], axis=1)
    bias = jnp.concatenate([bias] * group, axis=0)
    head_of_row = lax.broadcasted_iota(I32, (group * CHUNK, 1), 0) // CHUNK
    outs = []
    for kh in range(SWA_KV_HEADS):
        lanes = slice(kh * LANES, (kh + 1) * LANES)
        k_all = jnp.concatenate([kp_ref[:, lanes], kc_ref[:, lanes], kn_ref[:, lanes], kx_ref[:, lanes]], axis=0)
        v_all = jnp.concatenate([vp_ref[:, lanes], vc_ref[:, lanes], vn_ref[:, lanes], vx_ref[:, lanes]], axis=0)
        heads = range(kh * group, (kh + 1) * group)
        q = jnp.concatenate([q_ref[:, h * LANES:(h + 1) * LANES] for h in heads], axis=0)
        sink = jnp.zeros((group * CHUNK, 1), F32)
        for g, h in enumerate(heads):
            sink = jnp.where(head_of_row == g, sink_ref[h], sink)
        s = _dot_nt(q, k_all) + bias
        m = jnp.maximum(jnp.max(s, axis=-1, keepdims=True), sink)
        e = jnp.exp(s - m)
        o = _dot(_mx(e), v_all) / (jnp.sum(e, axis=-1, keepdims=True) + jnp.exp(sink - m))
        outs += [o[g * CHUNK:(g + 1) * CHUNK] for g in range(group)]
    for c in range(n_heads // 2):
        o_ref[:, c * LANES:(c + 1) * LANES] = (outs[2 * c] + pltpu.roll(outs[2 * c + 1], HEAD, 1)).astype(o_ref.dtype)


def _swa_attention(uq, kd, vp, sinks, geo, n_heads):
    rows = uq.shape[0]
    cpb, ncc, nlc = geo.cpb, geo.ncc, geo.nlc
    kvw = SWA_KV_HEADS * LANES
    ctx_per = geo.rpb // geo.n_ctx

    def lat_block(shift):
        def index(b, j):
            jl = jnp.clip(j - ncc + shift, 0, nlc - 1)
            return (b * cpb + ncc + jl, 0)
        return pl.BlockSpec((CHUNK, kvw), index)

    ctx_spec = pl.BlockSpec((geo.n_ctx, kvw), lambda b, j: (b * ctx_per, 0))
    kv_specs = [lat_block(-1), lat_block(0), lat_block(1), ctx_spec]
    return pl.pallas_call(
        functools.partial(_swa_kernel, n_heads, ncc, geo.n_lat, geo.n_ctx),
        grid=(geo.batch, cpb),
        in_specs=[pl.BlockSpec(memory_space=pltpu.SMEM),
                  pl.BlockSpec((CHUNK, n_heads * LANES), lambda b, j: (b * cpb + j, 0))] + kv_specs + kv_specs,
        out_specs=pl.BlockSpec((CHUNK, n_heads * HEAD), lambda b, j: (b * cpb + j, 0)),
        out_shape=jax.ShapeDtypeStruct((rows, n_heads * HEAD), MXU_DTYPE),
        compiler_params=_params("arbitrary", "arbitrary"),
        name="swa_attention",
    )(sinks, uq, kd, kd, kd, kd, vp, vp, vp, vp)


def _diff_kernel(lambda_init, q_ref, k_ref, v_ref, lam_ref, g_ref, o_ref):
    lp = lam_ref[...]
    lam = (jnp.exp(jnp.sum(lp[0:1] * lp[1:2], axis=-1, keepdims=True))
           - jnp.exp(jnp.sum(lp[2:3] * lp[3:4], axis=-1, keepdims=True)) + lambda_init)
    for hh in range(DIFF_HEADS_PER_STEP):
        v = v_ref[:, hh * LANES:(hh + 1) * LANES]
        parts = []
        for t in range(2):
            lanes = slice((2 * hh + t) * LANES, (2 * hh + t + 1) * LANES)
            s = _dot_nt(q_ref[:, lanes], k_ref[:, lanes])
            e = jnp.exp(s - jnp.max(s, axis=-1, keepdims=True))
            parts.append(_dot(_mx(e), v) / jnp.sum(e, axis=-1, keepdims=True))
        o = parts[0] - lam * parts[1]
        o_ref[:, hh * LANES:(hh + 1) * LANES] = (_rms(o) * g_ref[...] * (1.0 - lambda_init)).astype(o_ref.dtype)


def _diff_attention(uq, kd, v, lam_params, subln_g, geo, n_heads, lambda_init):
    tq = DIFF_TQ
    hp = DIFF_HEADS_PER_STEP
    nq = geo.n_lat // tq
    upb = geo.rpb // tq
    ucx = geo.n_ctx // tq
    assert n_heads % hp == 0 and 2 * HEAD == LANES
    return pl.pallas_call(
        functools.partial(_diff_kernel, lambda_init),
        grid=(geo.batch, n_heads // hp, nq),
        in_specs=[pl.BlockSpec((tq, 2 * hp * LANES), lambda b, h, j: (b * upb + ucx + j, h)),
                  pl.BlockSpec((geo.rpb, 2 * hp * LANES), lambda b, h, j: (b, h)),
                  pl.BlockSpec((geo.rpb, hp * LANES), lambda b, h, j: (b, h)),
                  pl.BlockSpec(lam_params.shape, lambda b, h, j: (0, 0)),
                  pl.BlockSpec((1, 2 * HEAD), lambda b, h, j: (0, 0))],
        out_specs=pl.BlockSpec((tq, hp * 2 * HEAD), lambda b, h, j: (b * nq + j, h)),
        out_shape=jax.ShapeDtypeStruct((geo.batch * geo.n_lat, n_heads * 2 * HEAD), MXU_DTYPE),
        compiler_params=_params("arbitrary", "arbitrary", "arbitrary"),
        name="diff_attention",
    )(uq, kd, v, lam_params, subln_g)


def _post_kernel(x_ref, a_ref, wo_ref, bo_ref, mod_ref, g_ref, wr_hi_ref, wr_lo_ref, br_ref,
                 xo_ref, h_ref, top_ref, gate_ref):
    mod = mod_ref[...]
    x = x_ref[...] + mod[2:3, :] * (_dot(a_ref[...], wo_ref[...]) + bo_ref[...])
    xo_ref[...] = x
    h = _norm_mod(x, g_ref[...], mod, 3, 4)
    h_ref[...] = h
    h_hi, h_lo = _split2(h)
    logits = _dot(h_hi, wr_hi_ref[...]) + _dot(h_lo, wr_hi_ref[...]) + _dot(h_hi, wr_lo_ref[...]) + br_ref[...]
    lane = lax.broadcasted_iota(I32, logits.shape, 1)
    lane_f = lane.astype(F32)
    top = jnp.zeros(logits.shape, I32)
    gate = jnp.zeros(logits.shape, F32)
    m0 = None
    for k in range(TOP_K):
        m = jnp.max(logits, axis=-1, keepdims=True)
        idx = jnp.min(jnp.where(logits == m, lane_f, float(LANES)), axis=-1, keepdims=True).astype(I32)
        logits = jnp.where(lane == idx, -jnp.inf, logits)
        m0 = m if k == 0 else m0
        top = jnp.where(lane == k, idx, top)
        gate = jnp.where(lane == k, jnp.exp(m - m0), gate)
    top_ref[...] = top
    gate_ref[...] = gate / jnp.sum(gate, axis=-1, keepdims=True)


def _post(x, a, wo, bo, modtab, gain, wr_hi, wr_lo, br, x_tile, mod_index, name):
    rows, din = a.shape
    d = x.shape[1]
    row_spec = pl.BlockSpec((TILE, d), lambda i: (i, 0))
    meta_spec = pl.BlockSpec((TILE, LANES), lambda i: (i, 0))
    const = lambda shape: pl.BlockSpec(shape, lambda i: (0, 0))
    return pl.pallas_call(
        _post_kernel,
        grid=(rows // TILE,),
        in_specs=[pl.BlockSpec((TILE, d), lambda i: (x_tile(i), 0)),
                  pl.BlockSpec((TILE, din), lambda i: (i, 0)),
                  const((din, d)), const((1, d)),
                  pl.BlockSpec((None, ADA_CHUNKS, d), lambda i: (mod_index(i), 0, 0)),
                  const((1, d)), const((d, LANES)), const((d, LANES)), const((1, LANES))],
        out_specs=[row_spec, row_spec, meta_spec, meta_spec],
        out_shape=[jax.ShapeDtypeStruct((rows, d), F32), jax.ShapeDtypeStruct((rows, d), F32),
                   jax.ShapeDtypeStruct((rows, LANES), I32), jax.ShapeDtypeStruct((rows, LANES), F32)],
        compiler_params=_params("arbitrary"),
        name=name,
    )(x, a, wo, bo, modtab, gain, wr_hi, wr_lo, br)


def _moe_pos_kernel(top_ref, pos_ref, meta_ref, count_ref, start_ref):
    phase = pl.program_id(0)
    i = pl.program_id(1)

    @pl.when(jnp.logical_and(phase == 0, i == 0))
    def _():
        count_ref[...] = jnp.zeros_like(count_ref)

    @pl.when(phase == 0)
    def _():
        top = top_ref[...]
        lane = lax.broadcasted_iota(I32, top.shape, 1)
        tile_count = sum(jnp.sum(jnp.where(lane == top[:, k:k + 1], 1.0, 0.0), axis=0, keepdims=True)
                         for k in range(TOP_K))
        count_ref[...] = count_ref[...] + tile_count

    @pl.when(jnp.logical_and(phase == 1, i == 0))
    def _():
        counts = count_ref[...]
        padded = jnp.ceil(counts / EXPERT_TILE) * EXPERT_TILE
        lane8 = lax.broadcasted_iota(I32, counts.shape, 1)
        incl = padded
        shift = 1
        while shift < LANES:
            incl = incl + jnp.where(lane8 >= shift, pltpu.roll(incl, shift, 1), 0.0)
            shift *= 2
        start_ref[...] = incl - padded
        sub = lax.broadcasted_iota(I32, counts.shape, 0)
        meta_ref[...] = jnp.where(sub == 0, counts, jnp.where(sub == 1, incl - padded, 0.0))
        count_ref[...] = jnp.zeros_like(count_ref)

    @pl.when(phase == 1)
    def _():
        r = lax.broadcasted_iota(I32, (TILE, TILE), 0)
        c = lax.broadcasted_iota(I32, (TILE, TILE), 1)
        strict_lower = _mx(jnp.where(r > c, 1.0, 0.0))
        base = count_ref[0:1, :] + start_ref[0:1, :]
        lane_t = lax.broadcasted_iota(I32, (TILE, LANES), 1)
        all_ones = jnp.ones((LANES, LANES), MXU_DTYPE)
        for sub in range(top_ref.shape[0] // TILE):
            rows = slice(sub * TILE, (sub + 1) * TILE)
            top_t = top_ref[rows, :]
            pos = jnp.zeros((TILE, LANES), I32)
            for k in range(TOP_K):
                onehot = jnp.where(lane_t == top_t[:, k:k + 1], 1.0, 0.0)
                before = _dot(strict_lower, _mx(onehot)) + base
                slot = _dot_exact_rhs(_split3(onehot * before), all_ones)
                pos = jnp.where(lane_t == k, slot.astype(I32), pos)
                base = base + jnp.sum(onehot, axis=0, keepdims=True)
            pos_ref[rows, :] = pos
        count_ref[...] = jnp.broadcast_to(base - start_ref[0:1, :], count_ref.shape)


def _moe_pos(top):
    rows = top.shape[0]
    step_rows = POS_ROWS if rows % POS_ROWS == 0 else TILE
    return pl.pallas_call(
        _moe_pos_kernel,
        grid=(2, rows // step_rows),
        in_specs=[pl.BlockSpec((step_rows, LANES), lambda p, i: (i, 0))],
        out_specs=[pl.BlockSpec((step_rows, LANES), lambda p, i: (i * p, 0)),
                   pl.BlockSpec((8, LANES), lambda p, i: (0, 0))],
        out_shape=[jax.ShapeDtypeStruct((rows, LANES), I32), jax.ShapeDtypeStruct((8, LANES), F32)],
        scratch_shapes=[pltpu.VMEM((8, LANES), F32), pltpu.VMEM((8, LANES), F32)],
        compiler_params=_params("arbitrary", "arbitrary"),
        name="moe_pos",
    )(top)


def _row_copy(src_ref, src_row, dst_ref, dst_row, sem):
    return pltpu.make_async_copy(src_ref.at[pl.ds(src_row, 1)], dst_ref.at[pl.ds(dst_row, 1)], sem)


def _dispatch_kernel(pos_ref, fill_ref, h_ref, xs_ref, zero_ref, sem):
    rows = h_ref.shape[0]
    base = pl.program_id(0) * (rows * TOP_K)

    @pl.when(pl.program_id(0) == 0)
    def _():
        zero_ref[...] = jnp.zeros_like(zero_ref)

        def fill_copy(j):
            start = pl.multiple_of(jnp.maximum(fill_ref[j], 0), EXPERT_TILE)
            return pltpu.make_async_copy(zero_ref, xs_ref.at[pl.ds(start, EXPERT_TILE)], sem)

        for j in range(fill_ref.shape[0]):
            pl.when(fill_ref[j] >= 0)(lambda j=j: fill_copy(j).start())
        for j in range(fill_ref.shape[0]):
            pl.when(fill_ref[j] >= 0)(lambda j=j: fill_copy(j).wait())

    for t in range(rows):
        for k in range(TOP_K):
            _row_copy(h_ref, t, xs_ref, pos_ref[base + t * TOP_K + k], sem).start(priority=k % 2)
    for k in range(TOP_K):
        pltpu.make_async_copy(h_ref, xs_ref.at[pl.ds(0, rows)], sem).wait()


def _dispatch(pos_flat, fill_rows, h, n_slots):
    rows, d = h.shape
    return pl.pallas_call(
        _dispatch_kernel,
        grid_spec=pltpu.PrefetchScalarGridSpec(
            num_scalar_prefetch=2,
            grid=(rows // DISPATCH_ROWS,),
            in_specs=[pl.BlockSpec((DISPATCH_ROWS, d), lambda i, pos, fill: (i, 0))],
            out_specs=pl.BlockSpec(memory_space=pl.ANY),
            scratch_shapes=[pltpu.VMEM((EXPERT_TILE, d), F32), pltpu.SemaphoreType.DMA(())]),
        out_shape=jax.ShapeDtypeStruct((n_slots, d), F32),
        compiler_params=_params("arbitrary"),
        name="moe_dispatch",
    )(pos_flat, fill_rows, h)


def _expert_kernel(d_expert, te_ref, na_ref, x_ref, wgu_ref, bgu_ref, wd_ref, bd_ref, o_ref, wgu_mx, wd_mx):
    i = pl.program_id(0)
    active = i < na_ref[0]
    new_expert = jnp.logical_or(i == 0, te_ref[i] != te_ref[jnp.maximum(i - 1, 0)])

    @pl.when(jnp.logical_and(active, new_expert))
    def _():
        wgu_mx[...] = _mx(wgu_ref[...])
        wd_mx[...] = _mx(wd_ref[...])

    @pl.when(active)
    def _():
        gu = _dot(_mx(x_ref[...]), wgu_mx[...]) + bgu_ref[...]
        glu = jnp.minimum(gu[:, :d_expert], SWIGLU_LIMIT)
        lin = jnp.clip(gu[:, d_expert:], -SWIGLU_LIMIT, SWIGLU_LIMIT)
        act = glu * _sigmoid(SWIGLU_ALPHA * glu) * (lin + 1.0)
        o_ref[...] = (_dot(_mx(act), wd_mx[...]) + bd_ref[...])[:, None, :]

    @pl.when(jnp.logical_not(active))
    def _():
        o_ref[...] = jnp.zeros_like(o_ref)


def _experts(layer, tile_expert, n_active, xs, w_gu, b_gu, w_down, b_down):
    n_slots, d = xs.shape
    depth, n_exp, _, two_de = w_gu.shape
    de = two_de // 2
    n_tiles = n_slots // EXPERT_TILE
    row = lambda i, te, na: (jnp.minimum(i, na[0] - 1), 0)
    by_expert = lambda i, te, na: (layer, te[i], 0, 0)
    return pl.pallas_call(
        functools.partial(_expert_kernel, de),
        grid_spec=pltpu.PrefetchScalarGridSpec(
            num_scalar_prefetch=2,
            grid=(n_tiles,),
            in_specs=[pl.BlockSpec((EXPERT_TILE, d), row),
                      pl.BlockSpec((None, None, d, two_de), by_expert),
                      pl.BlockSpec((None, None, 1, two_de), by_expert),
                      pl.BlockSpec((None, None, de, d), by_expert),
                      pl.BlockSpec((None, None, 1, d), by_expert)],
            out_specs=pl.BlockSpec((EXPERT_TILE, 1, d), lambda i, te, na: (i, 0, 0)),
            scratch_shapes=[pltpu.VMEM((d, two_de), MXU_DTYPE), pltpu.VMEM((de, d), MXU_DTYPE)]),
        out_shape=jax.ShapeDtypeStruct((n_slots, 1, d), F32),
        compiler_params=_params("arbitrary"),
        name="moe_experts",
    )(tile_expert, n_active, xs, w_gu, b_gu.reshape(depth, n_exp, 1, two_de), w_down,
      b_down.reshape(depth, n_exp, 1, d))


def _combine_kernel(final, n_tiles, pos_ref, x_ref, gate_ref, mod_ref, gfin_ref, ys_ref, o_ref, buf, sem):
    i = pl.program_id(0)

    def issue(tile, slot):
        base = tile * (TILE * TOP_K)
        for t in range(TILE):
            for k in range(TOP_K):
                _row_copy(ys_ref, pos_ref[base + t * TOP_K + k], buf.at[slot, k], t,
                          sem.at[slot]).start(priority=k % 2)

    pl.when(i == 0)(lambda: issue(0, 0))
    for slot in range(2):
        pl.when(jnp.logical_and(i + 1 < n_tiles, (i + 1) % 2 == slot))(lambda slot=slot: issue(i + 1, slot))

    slot = i % 2
    for k in range(TOP_K):
        pltpu.make_async_copy(ys_ref.at[pl.ds(0, TILE)], buf.at[slot, k], sem.at[slot]).wait()
    gates = gate_ref[...]
    f = gates[:, 0:1] * buf[slot, 0].reshape(x_ref.shape)
    for k in range(1, TOP_K):
        f = f + gates[:, k:k + 1] * buf[slot, k].reshape(x_ref.shape)
    x = x_ref[...] + mod_ref[5:6, :] * f
    if final:
        x = _rms(x) * gfin_ref[...]
    o_ref[...] = x


def _combine(pos_flat, x, gates, modtab, g_final, ys, mod_index, final):
    rows, d = x.shape
    return pl.pallas_call(
        functools.partial(_combine_kernel, final, rows // TILE),
        grid_spec=pltpu.PrefetchScalarGridSpec(
            num_scalar_prefetch=1,
            grid=(rows // TILE,),
            in_specs=[pl.BlockSpec((TILE, d), lambda i, pos: (i, 0)),
                      pl.BlockSpec((TILE, LANES), lambda i, pos: (i, 0)),
                      pl.BlockSpec((None, ADA_CHUNKS, d), lambda i, pos: (mod_index(i), 0, 0)),
                      pl.BlockSpec((1, d), lambda i, pos: (0, 0)),
                      pl.BlockSpec(memory_space=pl.ANY)],
            out_specs=pl.BlockSpec((TILE, d), lambda i, pos: (i, 0)),
            scratch_shapes=[pltpu.VMEM((2, TOP_K, TILE, 1, d), F32), pltpu.SemaphoreType.DMA((2,))]),
        out_shape=jax.ShapeDtypeStruct((rows, d), F32),
        compiler_params=_params("arbitrary"),
        name="moe_combine",
    )(pos_flat, x, gates, modtab, g_final, ys)


def _moe(layer, x, h, top, gates, modtab, mod_index, w_gu, b_gu, w_down, b_down, g_final, final):
    rows = h.shape[0]
    n_exp = w_gu.shape[1]
    pos, meta = _moe_pos(top)
    counts = meta[0, :n_exp].astype(I32)
    starts = meta[1, :n_exp].astype(I32)
    n_tiles = rows * TOP_K // EXPERT_TILE + n_exp
    tiles_per = (counts + EXPERT_TILE - 1) // EXPERT_TILE
    tile_start = starts // EXPERT_TILE
    tile_end = tile_start + tiles_per
    n_active = tile_end[-1]
    tile = jnp.minimum(jnp.arange(n_tiles, dtype=I32), n_active - 1)
    tile_expert = jnp.sum(tile_end[None, :] <= tile[:, None], axis=1).astype(I32)
    pos_flat = pos[:, :TOP_K].reshape(-1)
    last_tile = jnp.where(tiles_per > 0, tile_end - 1, -1)
    tail_tile = n_active + jnp.arange(n_exp, dtype=I32)
    tail_tile = jnp.where(tail_tile < n_tiles, tail_tile, -1)
    fill_tiles = jnp.concatenate([last_tile, tail_tile])
    fill_rows = jnp.where(fill_tiles >= 0, fill_tiles * EXPERT_TILE, -1).astype(I32)
    xs = _dispatch(pos_flat, fill_rows, h, n_tiles * EXPERT_TILE)
    ys = _experts(layer, tile_expert, n_active.reshape(1), xs, w_gu, b_gu, w_down, b_down)
    return _combine(pos_flat, x, gates, modtab, g_final, ys, mod_index, final)


def _rot_cols(w):
    lead = w.shape[:-1]
    blocks = w.reshape(lead + (-1, 2, HEAD // 2))
    return jnp.concatenate([-blocks[..., 1:2, :], blocks[..., 0:1, :]], axis=-2).reshape(w.shape)


def _pair_cols(a, b):
    lead = a.shape[:-1]
    a3 = a.reshape(lead + (-1, HEAD))
    b3 = b.reshape(lead + (-1, HEAD))
    return jnp.concatenate([a3, b3], axis=-1).reshape(lead + (-1,))


def _rope_tables(geo):
    t = jnp.arange(geo.n_lat)
    rowp = (t // GRID_W).astype(F32)
    colp = (t % GRID_W).astype(F32)
    quarter = HEAD // 4
    inv_freq = ROPE_BASE ** (-jnp.arange(quarter, dtype=F32) / quarter)
    ang = jnp.concatenate([rowp[:, None] * inv_freq, colp[:, None] * inv_freq], axis=-1)
    cos = jnp.concatenate([jnp.ones((geo.n_ctx, HEAD // 2), F32), jnp.cos(ang)], axis=0)
    sin = jnp.concatenate([jnp.zeros((geo.n_ctx, HEAD // 2), F32), jnp.sin(ang)], axis=0)
    cos64 = jnp.concatenate([cos, cos], axis=-1)
    sin64 = jnp.concatenate([sin, sin], axis=-1)
    scale = HEAD ** -0.5
    q_tab = jnp.concatenate([cos64, sin64], axis=-1) * scale
    k_cos = jnp.concatenate([cos64, cos64], axis=-1)
    k_sin = jnp.concatenate([sin64, sin64], axis=-1)
    return q_tab, k_cos, k_sin


def _row(v):
    return v.reshape(1, -1).astype(F32)


def kernel(x, c, ctx, c_ctx, ada_w, ada_b, g_mix, g_ffn, g_final, conv_w_pw1, conv_b_pw1, conv_w_dw, conv_b_dw, conv_ln_g, conv_ln_b, conv_w_pw2, conv_b_pw2, ssm_w_in, ssm_w_conv, ssm_b_conv, ssm_a_log, ssm_dt_bias, ssm_d, ssm_norm_g, ssm_w_out, swa_w_qkv, swa_b_qkv, swa_sinks, swa_w_o, swa_b_o, diff_w_qkv, diff_lambda_q1, diff_lambda_k1, diff_lambda_q2, diff_lambda_k2, diff_subln_g, diff_w_o, moe_w_router, moe_b_router, moe_w_gu, moe_b_gu, moe_w_down, moe_b_down):
    batch, n_lat, d = x.shape
    n_ctx = ctx.shape[1]
    depth = ada_w.shape[0]
    geo = _Geo(batch, n_ctx, n_lat)
    n_exp = moe_w_router.shape[-1]
    q_tab, k_cos, k_sin = _rope_tables(geo)

    cond_rows = 16
    assert batch + 1 <= cond_rows
    cond = jnp.zeros((cond_rows, d), F32).at[:batch].set(c).at[batch].set(c_ctx)
    ada = _adaln(cond, ada_w, ada_b)
    mod_lat = ada[:, :batch].reshape(depth, batch, ADA_CHUNKS, d)
    mod_ctx = jnp.broadcast_to(ada[:, batch].reshape(depth, 1, ADA_CHUNKS, d), mod_lat.shape)
    modtabs = jnp.stack([mod_ctx, mod_lat], axis=2).reshape(depth, 2 * batch, ADA_CHUNKS, d)

    xs = jnp.concatenate([ctx, x], axis=1).reshape(geo.rows, d)
    zero_bias = jnp.zeros((1, d), F32)
    out = None
    for i in range(depth):
        kind, j = i % 4, i // 4
        ctx_out = i < depth - 1
        modtab = modtabs[i]
        gain = _row(g_mix[i])
        if kind == 0:
            w1 = _mx(conv_w_pw1[j])
            b1 = _row(conv_b_pw1[j])
            u, = _norm_proj(xs, gain, modtab, geo, [("glu", [w1[:, :d], b1[:, :d], w1[:, d:], b1[:, d:]], F32)],
                            "conv_pw1_glu")
            a = _dwconv(u, conv_w_dw[j], conv_b_dw[j], geo, "ln_silu", [_row(conv_ln_g[j]), _row(conv_ln_b[j])],
                        d, MXU_DTYPE, "conv_dw_ln")
            wo, bo = _mx(conv_w_pw2[j]), _row(conv_b_pw2[j])
        elif kind == 1:
            di = ssm_norm_g.shape[-1]
            heads = ssm_a_log.shape[-1]
            conv_dim = ssm_w_conv.shape[-1]
            w_in = ssm_w_in[j]
            w_dt = w_in[:, di + conv_dim:].reshape(d, 2, heads)
            w_dt = jnp.pad(w_dt, ((0, 0), (0, 0), (0, LANES - heads))).reshape(d, 2 * LANES)
            w_dt_hi = _mx(w_dt)
            w_dt_lo = _mx(w_dt - w_dt_hi.astype(F32))
            z, xbc, dt_raw = _norm_proj(xs, gain, modtab, geo, [
                ("plain", [_mx(w_in[:, :di]), jnp.zeros((1, di), F32)], F32),
                ("plain", [_mx(w_in[:, di:di + conv_dim]), jnp.zeros((1, conv_dim), F32)], F32),
                ("precise", [w_dt_hi, w_dt_lo, jnp.zeros((1, 2 * LANES), F32)], F32)], "ssm_in")
            tc = 1024 if conv_dim % 1024 == 0 else 512
            xbc = _dwconv(xbc, ssm_w_conv[j], ssm_b_conv[j], geo, "silu", [], tc, F32, "ssm_conv")
            y2 = _ssd(xbc, dt_raw, ssm_dt_bias[j], ssm_a_log[j], geo, di, heads)
            a = _ssm_finish(y2, xbc, z, _row(jnp.repeat(ssm_d[j], HEAD)), _row(ssm_norm_g[j]))
            wo, bo = _mx(ssm_w_out[j]), zero_bias
        elif kind == 2:
            nh = swa_sinks.shape[-1]
            nq, nkv = nh * HEAD, SWA_KV_HEADS * HEAD
            w, b = swa_w_qkv[j], swa_b_qkv[j][None, :]
            wq, wk, wv = w[:, :nq], w[:, nq:nq + nkv], w[:, nq + nkv:]
            bq, bk, bv = b[:, :nq], b[:, nq:nq + nkv], b[:, nq + nkv:]
            uq, kd, vp = _norm_proj(xs, gain, modtab, geo, [
                ("tab1", [_mx(_pair_cols(wq, _rot_cols(wq))), _pair_cols(bq, _rot_cols(bq)), q_tab], MXU_DTYPE),
                ("tab2", [_mx(_pair_cols(wk, wk)), _pair_cols(bk, bk), k_cos,
                          _mx(_pair_cols(_rot_cols(wk), _rot_cols(wk))), _pair_cols(_rot_cols(bk), _rot_cols(bk)),
                          k_sin], MXU_DTYPE),
                ("plain", [_mx(_pair_cols(wv, jnp.zeros_like(wv))), _pair_cols(bv, jnp.zeros_like(bv))],
                 MXU_DTYPE)], "swa_qkv")
            a = _swa_attention(uq, kd, vp, swa_sinks[j].astype(F32), geo, nh)
            wo, bo = _mx(swa_w_o[j]), _row(swa_b_o[j])
        else:
            assert not ctx_out, "differential attention is only built for a layer without context output"
            lambda_init = 0.8 - 0.6 * math.exp(-0.3 * i)
            w = diff_w_qkv[j]
            wq, wk, wv = w[:, :d], w[:, d:2 * d], w[:, 2 * d:]
            zb = jnp.zeros((1, 2 * d), F32)
            uq, kd, v = _norm_proj(xs, gain, modtab, geo, [
                ("tab1", [_mx(_pair_cols(wq, _rot_cols(wq))), zb, q_tab], MXU_DTYPE),
                ("tab2", [_mx(_pair_cols(wk, wk)), zb, k_cos,
                          _mx(_pair_cols(_rot_cols(wk), _rot_cols(wk))), zb, k_sin], MXU_DTYPE),
                ("plain", [_mx(wv), zero_bias], MXU_DTYPE)], "diff_qkv")
            lam_params = jnp.stack([diff_lambda_q1[j], diff_lambda_k1[j], diff_lambda_q2[j],
                                    diff_lambda_k2[j]]).astype(F32)
            a = _diff_attention(uq, kd, v, lam_params, _row(diff_subln_g[j]), geo, d // (2 * HEAD), lambda_init)
            wo, bo = _mx(diff_w_o[j]), zero_bias

        wr = jnp.pad(moe_w_router[i], ((0, 0), (0, LANES - n_exp)))
        wr_hi = _mx(wr)
        wr_lo = _mx(wr - wr_hi.astype(F32))
        br = jnp.pad(_row(moe_b_router[i]), ((0, 0), (0, LANES - n_exp)), constant_values=-1e30)
        if ctx_out:
            x_tile, mod_index = (lambda t: t), geo.mod_all
        else:
            x_tile, mod_index = geo.lat_tile, geo.mod_lat
            if a.shape[0] == geo.rows:
                a = a.reshape(batch, geo.rpb, -1)[:, n_ctx:].reshape(batch * n_lat, -1)
        xs, h, top, gates = _post(xs, a, wo, bo, modtab, _row(g_ffn[i]), wr_hi, wr_lo, br, x_tile, mod_index,
                                  "post_mixer")
        final = i == depth - 1
        xs = _moe(i, xs, h, top, gates, modtab, mod_index, moe_w_gu, moe_b_gu, moe_w_down, moe_b_down,
                  _row(g_final), final)
        if not ctx_out and not final:
            raise NotImplementedError("a layer without context output must be the last layer")
        out = xs
    return out.reshape(batch, n_lat, d)
```

```python
import functools
import math

import jax
import jax.numpy as jnp
from jax import lax
from jax.experimental import pallas as pl
from jax.experimental.pallas import tpu as pltpu

F32 = jnp.float32
I32 = jnp.int32
MXU_DTYPE = jnp.bfloat16

LANES = 128
SUBLANES = 8
VMEM_LIMIT_BYTES = 56 * 1024 * 1024

TILE = 256
CHUNK = 128
HALO = 16
HEAD = 64
NORM_EPS = 1e-6
ROPE_BASE = 10000.0
GRID_W = 64
ADA_CHUNKS = 6
SSM_GROUPS = 4
SSM_STATE = 128
SWA_KV_HEADS = 4
SWA_WINDOW = 128
TOP_K = 4
SWIGLU_LIMIT = 7.0
SWIGLU_ALPHA = 1.702
EXPERT_TILE = 512
POS_ROWS = 1024
DISPATCH_ROWS = 512
DIFF_TQ = 256
DIFF_HEADS_PER_STEP = 4


def _mx(v):
    return v.astype(MXU_DTYPE)


def _dot(a, b):
    return jnp.dot(a, b, preferred_element_type=F32)


def _dot_nt(a, b):
    return lax.dot_general(a, b, (((1,), (1,)), ((), ())), preferred_element_type=F32)


def _split2(v):
    hi = _mx(v)
    return hi, _mx(v - hi.astype(F32))


def _split3(v):
    hi = _mx(v)
    r = v - hi.astype(F32)
    mid = _mx(r)
    return hi, mid, _mx(r - mid.astype(F32))


def _dot_exact_rhs(parts, m):
    acc = _dot(parts[0], m)
    for p in parts[1:]:
        acc = acc + _dot(p, m)
    return acc


def _sigmoid(v):
    return 1.0 / (1.0 + jnp.exp(-v))


def _softplus(v):
    return jnp.maximum(v, 0.0) + jnp.log(1.0 + jnp.exp(-jnp.abs(v)))


def _rms(v):
    return v * lax.rsqrt(jnp.mean(v * v, axis=-1, keepdims=True) + NORM_EPS)


def _store_row_tiles(ref, value):
    nc = value.shape[1] // LANES
    for c in range(nc):
        ref[pl.ds(c, value.shape[0], stride=nc), :] = value[:, c * LANES:(c + 1) * LANES]


def _load_row_tiles(ref, rows, nc):
    return jnp.concatenate([ref[pl.ds(c, rows, stride=nc), :] for c in range(nc)], axis=1)


def _params(*sem):
    return pltpu.CompilerParams(dimension_semantics=sem, vmem_limit_bytes=VMEM_LIMIT_BYTES)


class _Geo:
    def __init__(self, batch, n_ctx, n_lat):
        assert n_ctx % TILE == 0 and n_lat % TILE == 0 and n_lat % n_ctx == 0
        self.batch, self.n_ctx, self.n_lat = batch, n_ctx, n_lat
        self.rpb = n_ctx + n_lat
        self.tpb = self.rpb // TILE
        self.nct = n_ctx // TILE
        self.nlt = n_lat // TILE
        self.cpb = self.rpb // CHUNK
        self.ncc = n_ctx // CHUNK
        self.nlc = n_lat // CHUNK
        self.rows = batch * self.rpb

    def mod_all(self, i):
        return (i // self.tpb) * 2 + (i % self.tpb >= self.nct).astype(I32)

    def lat_tile(self, i):
        return (i // self.nlt) * self.tpb + self.nct + i % self.nlt

    def mod_lat(self, i):
        return (i // self.nlt) * 2 + 1


def _adaln_kernel(c_ref, w_ref, b_ref, o_ref):
    c = c_ref[...]
    s_hi, s_lo = _split2(c * _sigmoid(c))
    w_hi, w_lo = _split2(w_ref[...])
    o_ref[...] = _dot(s_hi, w_hi) + _dot(s_lo, w_hi) + _dot(s_hi, w_lo) + b_ref[...]


def _adaln(cond, ada_w, ada_b):
    depth, d, n = ada_w.shape
    rows = cond.shape[0]
    return pl.pallas_call(
        _adaln_kernel,
        grid=(depth, n // d),
        in_specs=[pl.BlockSpec((rows, d), lambda l, j: (0, 0)),
                  pl.BlockSpec((None, d, d), lambda l, j: (l, 0, j)),
                  pl.BlockSpec((None, 1, d), lambda l, j: (l, 0, j))],
        out_specs=pl.BlockSpec((None, rows, d), lambda l, j: (l, 0, j)),
        out_shape=jax.ShapeDtypeStruct((depth, rows, n), F32),
        compiler_params=_params("arbitrary", "arbitrary"),
        name="adaln",
    )(cond, ada_w, ada_b.reshape(depth, 1, n))


def _norm_mod(x, g, mod, shift_row, scale_row):
    return _rms(x) * g * (1.0 + mod[scale_row:scale_row + 1, :]) + mod[shift_row:shift_row + 1, :]


def _lane_tile(tab, n):
    return jnp.tile(tab, (1, n // LANES))


_PROJ_OPERANDS = {"plain": "wb", "precise": "wwb", "glu": "wbwb", "tab1": "wbt", "tab2": "wbtwbt"}


def _norm_proj_kernel(modes, x_ref, g_ref, mod_ref, *refs):
    out_refs = refs[len(refs) - len(modes):]
    h = _norm_mod(x_ref[...], g_ref[...], mod_ref[...], 0, 1)
    hb = _mx(h)
    at = 0
    for mode, o_ref in zip(modes, out_refs):
        ops = refs[at:at + len(_PROJ_OPERANDS[mode])]
        at += len(ops)
        n = o_ref.shape[-1]
        if mode == "precise":
            w_hi, w_lo, b = ops
            h_lo = _mx(h - hb.astype(F32))
            acc = _dot(hb, w_hi[...]) + _dot(h_lo, w_hi[...]) + _dot(hb, w_lo[...]) + b[...]
        elif mode == "plain":
            w, b = ops
            acc = _dot(hb, w[...]) + b[...]
        elif mode == "glu":
            w1, b1, w2, b2 = ops
            acc = (_dot(hb, w1[...]) + b1[...]) * _sigmoid(_dot(hb, w2[...]) + b2[...])
        elif mode == "tab1":
            w1, b1, t1 = ops
            acc = (_dot(hb, w1[...]) + b1[...]) * _lane_tile(t1[...], n)
        else:
            w1, b1, t1, w2, b2, t2 = ops
            acc = ((_dot(hb, w1[...]) + b1[...]) * _lane_tile(t1[...], n)
                   + (_dot(hb, w2[...]) + b2[...]) * _lane_tile(t2[...], n))
        o_ref[...] = acc.astype(o_ref.dtype)


def _norm_proj(x, gain, modtab, geo, groups, name):
    rows, d = x.shape
    specs = [pl.BlockSpec((TILE, d), lambda i: (i, 0)),
             pl.BlockSpec((1, d), lambda i: (0, 0)),
             pl.BlockSpec((None, ADA_CHUNKS, d), lambda i: (geo.mod_all(i), 0, 0))]
    operands, out_specs, out_shapes = [], [], []
    for mode, ops, out_dtype in groups:
        n = ops[0].shape[1]
        for kind, op in zip(_PROJ_OPERANDS[mode], ops):
            if kind == "t":
                specs.append(pl.BlockSpec((TILE, LANES), lambda i: (i % geo.tpb, 0)))
            else:
                specs.append(pl.BlockSpec(op.shape, lambda i: (0, 0)))
        operands += ops
        out_specs.append(pl.BlockSpec((TILE, n), lambda i: (i, 0)))
        out_shapes.append(jax.ShapeDtypeStruct((rows, n), out_dtype))
    return pl.pallas_call(
        functools.partial(_norm_proj_kernel, tuple(g[0] for g in groups)),
        grid=(rows // TILE,),
        in_specs=specs,
        out_specs=out_specs,
        out_shape=out_shapes,
        compiler_params=_params("arbitrary"),
        name=name,
    )(x, gain, modtab, *operands)


def _dwconv_kernel(width, mode, tpb, nct, cur_ref, prev_ref, next_ref, w_ref, b_ref, *refs):
    shifted = width > SUBLANES
    pad_ref = refs[-2] if shifted else refs[-1]
    o_ref = refs[-3] if shifted else refs[-2]
    p = pl.program_id(0) % tpb
    has_prev = jnp.logical_and(p != 0, p != nct)
    has_next = jnp.logical_and(p != nct - 1, p != tpb - 1)
    pad_ref[0:HALO, :] = jnp.where(has_prev, prev_ref[...], 0.0)
    pad_ref[HALO:HALO + TILE, :] = cur_ref[...]
    pad_ref[HALO + TILE:, :] = jnp.where(has_next, next_ref[...], 0.0)
    half = (width - 1) // 2
    acc = jnp.broadcast_to(b_ref[...], o_ref.shape)
    if shifted:
        sh_ref = refs[-1]
        span = sh_ref.shape[1]
        for s in range(1, SUBLANES):
            sh_ref[s - 1] = pad_ref[s:s + span, :]
    for k in range(width):
        off = HALO - half + k
        if shifted and off % SUBLANES:
            base = off - off % SUBLANES
            tap = sh_ref[off % SUBLANES - 1, base:base + TILE, :]
        else:
            tap = pad_ref[off:off + TILE, :]
        acc = acc + w_ref[k:k + 1, :] * tap
    if mode == "ln_silu":
        g_ref, beta_ref = refs[:2]
        cen = acc - jnp.mean(acc, axis=-1, keepdims=True)
        acc = cen * lax.rsqrt(jnp.mean(cen * cen, axis=-1, keepdims=True) + NORM_EPS) * g_ref[...] + beta_ref[...]
    o_ref[...] = (acc * _sigmoid(acc)).astype(o_ref.dtype)


def _dwconv(u, w, b, geo, mode, extra, tc, out_dtype, name):
    rows, c = u.shape
    width = w.shape[0]
    per = TILE // HALO
    last = rows // HALO - 1
    specs = [pl.BlockSpec((TILE, tc), lambda i, j: (i, j)),
             pl.BlockSpec((HALO, tc), lambda i, j: (jnp.maximum(i * per - 1, 0), j)),
             pl.BlockSpec((HALO, tc), lambda i, j: (jnp.minimum((i + 1) * per, last), j)),
             pl.BlockSpec((width, tc), lambda i, j: (0, j)),
             pl.BlockSpec((1, tc), lambda i, j: (0, j))]
    specs += [pl.BlockSpec((1, tc), lambda i, j: (0, j)) for _ in extra]
    scratch = [pltpu.VMEM((TILE + 2 * HALO, tc), F32)]
    if width > SUBLANES:
        scratch.append(pltpu.VMEM((SUBLANES - 1, TILE + 2 * HALO - SUBLANES, tc), F32))
    return pl.pallas_call(
        functools.partial(_dwconv_kernel, width, mode, geo.tpb, geo.nct),
        grid=(rows // TILE, c // tc),
        in_specs=specs,
        out_specs=pl.BlockSpec((TILE, tc), lambda i, j: (i, j)),
        out_shape=jax.ShapeDtypeStruct((rows, c), out_dtype),
        scratch_shapes=scratch,
        compiler_params=_params("arbitrary", "arbitrary"),
        name=name,
    )(u, u, u, w, b.reshape(1, c), *extra)


def _ssd_kernel(groups, heads_per_group, x_ref, b_ref, c_ref, dt_ref, bias_ref, alog_ref, e_ref, y_ref, h_ref):
    direction = pl.program_id(0)
    step = pl.program_id(2)
    n_state = SSM_STATE
    rp = heads_per_group * HEAD

    @pl.when(step == 0)
    def _():
        h_ref[...] = jnp.zeros_like(h_ref)

    fwd = direction == 0
    row = lax.broadcasted_iota(I32, (CHUNK, CHUNK), 0)
    col = lax.broadcasted_iota(I32, (CHUNK, CHUNK), 1)
    tri = (row - col) * jnp.where(fwd, 1, -1) >= 0
    dtv = _softplus(dt_ref[...] + bias_ref[...])
    a = dtv * (-jnp.exp(alog_ref[...]))
    tri_m = _mx(jnp.where(tri, 1.0, 0.0))
    a3 = _split3(a)
    acum = _dot(tri_m, a3[0]) + _dot(tri_m, a3[1]) + _dot(tri_m, a3[2])
    acum_t = acum.T
    expand = e_ref[...]
    dt_x = _dot_exact_rhs(_split2(dtv), expand)
    ac_x = _dot_exact_rhs(_split3(acum), expand)
    tot_x = jnp.where(fwd, ac_x[CHUNK - 1:CHUNK, :], ac_x[0:1, :])
    xdt = x_ref[...] * dt_x
    xdt_b = _mx(xdt)
    xdt_end_b = _mx(xdt * jnp.exp(tot_x - ac_x))
    e_ac = jnp.exp(ac_x)
    decay = jnp.exp(tot_x)
    lo_half = lax.broadcasted_iota(I32, (CHUNK, LANES), 1) < HEAD
    for g in range(groups):
        bg = b_ref[:, g * n_state:(g + 1) * n_state]
        cb_g = _mx(c_ref[:, g * n_state:(g + 1) * n_state])
        cb = _dot_nt(cb_g, _mx(bg))
        bg_t = _mx(bg.T)
        h_t = h_ref[g]
        cols = slice(g * rp, (g + 1) * rp)
        y_off = _dot(cb_g, _mx(h_t)) * e_ac[:, cols]
        blocks = []
        for pair in range(heads_per_group // 2):
            xb = xdt_b[:, g * rp + pair * LANES:g * rp + (pair + 1) * LANES]
            halves = []
            for hh in range(2):
                c = g * heads_per_group + 2 * pair + hh
                seg = acum[:, c:c + 1] - acum_t[c:c + 1, :]
                within = jnp.exp(jnp.where(tri, seg, -jnp.inf))
                halves.append(_dot(_mx(cb * within), xb))
            blocks.append(jnp.where(lo_half, halves[0], halves[1]))
        y_ref[:, cols] = (jnp.concatenate(blocks, axis=1) + y_off).astype(y_ref.dtype)
        h_ref[g] = h_t * decay[:, cols] + _dot(bg_t, xdt_end_b[:, cols])


def _ssd(xbc, dt_raw, dt_bias, a_log, geo, d_inner, heads):
    rows = xbc.shape[0]
    groups = SSM_GROUPS
    hpg = heads // groups
    gn = groups * SSM_STATE
    assert d_inner % gn == 0 and hpg % 2 == 0 and heads <= LANES
    cpb, ncc = geo.cpb, geo.ncc

    def rb(d, b, s):
        back = jnp.where(s < ncc, ncc - 1 - s, cpb + ncc - 1 - s)
        return b * cpb + jnp.where(d == 0, s, back)

    pad = LANES - heads
    bias = jnp.pad(dt_bias.astype(F32), ((0, 0), (0, pad))).reshape(2, 1, LANES)
    alog = jnp.pad(a_log.astype(F32), ((0, 0), (0, pad))).reshape(2, 1, LANES)
    expand = (jnp.arange(LANES)[:, None] == jnp.arange(d_inner)[None, :] // HEAD).astype(MXU_DTYPE)
    return pl.pallas_call(
        functools.partial(_ssd_kernel, groups, hpg),
        grid=(2, geo.batch, cpb),
        in_specs=[pl.BlockSpec((CHUNK, d_inner), lambda d, b, s: (rb(d, b, s), 0)),
                  pl.BlockSpec((CHUNK, gn), lambda d, b, s: (rb(d, b, s), d_inner // gn)),
                  pl.BlockSpec((CHUNK, gn), lambda d, b, s: (rb(d, b, s), d_inner // gn + 1)),
                  pl.BlockSpec((CHUNK, LANES), lambda d, b, s: (rb(d, b, s), d)),
                  pl.BlockSpec((None, 1, LANES), lambda d, b, s: (d, 0, 0)),
                  pl.BlockSpec((None, 1, LANES), lambda d, b, s: (d, 0, 0)),
                  pl.BlockSpec((LANES, d_inner), lambda d, b, s: (0, 0))],
        out_specs=pl.BlockSpec((None, CHUNK, d_inner), lambda d, b, s: (d, rb(d, b, s), 0)),
        out_shape=jax.ShapeDtypeStruct((2, rows, d_inner), MXU_DTYPE),
        scratch_shapes=[pltpu.VMEM((groups, SSM_STATE, hpg * HEAD), F32)],
        compiler_params=_params("arbitrary", "arbitrary", "arbitrary"),
        name="ssd_scan",
    )(xbc, xbc, xbc, dt_raw, bias, alog, expand)


def _ssm_finish_kernel(y_ref, xs_ref, z_ref, dskip_ref, g_ref, o_ref):
    y = y_ref[0].astype(F32) + y_ref[1].astype(F32) + xs_ref[...] * dskip_ref[...]
    z = z_ref[...].astype(F32)
    o_ref[...] = (_rms(y * (z * _sigmoid(z))) * g_ref[...]).astype(o_ref.dtype)


def _ssm_finish(y2, xbc, z, d_skip_cols, norm_g):
    rows, di = z.shape
    return pl.pallas_call(
        _ssm_finish_kernel,
        grid=(rows // TILE,),
        in_specs=[pl.BlockSpec((2, TILE, di), lambda i: (0, i, 0)),
                  pl.BlockSpec((TILE, di), lambda i: (i, 0)),
                  pl.BlockSpec((TILE, di), lambda i: (i, 0)),
                  pl.BlockSpec((1, di), lambda i: (0, 0)),
                  pl.BlockSpec((1, di), lambda i: (0, 0))],
        out_specs=pl.BlockSpec((TILE, di), lambda i: (i, 0)),
        out_shape=jax.ShapeDtypeStruct((rows, di), MXU_DTYPE),
        compiler_params=_params("arbitrary"),
        name="ssm_finish",
    )(y2, xbc, z, d_skip_cols, norm_g)


def _swa_kernel(n_heads, ncc, n_lat, n_ctx, sink_ref, q_ref, kp_ref, kc_ref, kn_ref, kx_ref,
                vp_ref, vc_ref, vn_ref, vx_ref, o_ref):
    j = pl.program_id(1)
    is_ctx = j < ncc
    start = (j - ncc) * CHUNK
    span = 3 * CHUNK
    qi = lax.broadcasted_iota(I32, (CHUNK, span), 0)
    rel = lax.broadcasted_iota(I32, (CHUNK, span), 1) - SWA_WINDOW
    kpos = start + rel
    ninf = -jnp.inf
    band = jnp.where(jnp.abs(qi - rel) <= SWA_WINDOW,
                     jnp.where(kpos >= 0, jnp.where(kpos < n_lat, 0.0, ninf), ninf), ninf)
    band = jnp.where(is_ctx, ninf, band)
    group = n_heads // SWA_KV_HEADS
    bias = jnp.concatenate([band, jnp.zeros((CHUNK, n_ctx), F32)], axis=1)
    bias = jnp.concatenate([bias] * group, axis=0)
    head_of_row = lax.broadcasted_iota(I32, (group * CHUNK, 1), 0) // CHUNK
    outs = []
    for kh in range(SWA_KV_HEADS):
        lanes = slice(kh * LANES, (kh + 1) * LANES)
        k_all = jnp.concatenate([kp_ref[:, lanes], kc_ref[:, lanes], kn_ref[:, lanes], kx_ref[:, lanes]], axis=0)
        v_all = jnp.concatenate([vp_ref[:, lanes], vc_ref[:, lanes], vn_ref[:, lanes], vx_ref[:, lanes]], axis=0)
        heads = range(kh * group, (kh + 1) * group)
        q = jnp.concatenate([q_ref[:, h * LANES:(h + 1) * LANES] for h in heads], axis=0)
        sink = jnp.zeros((group * CHUNK, 1), F32)
        for g, h in enumerate(heads):
            sink = jnp.where(head_of_row == g, sink_ref[h], sink)
        s = _dot_nt(q, k_all) + bias
        m = jnp.maximum(jnp.max(s, axis=-1, keepdims=True), sink)
        e = jnp.exp(s - m)
        o = _dot(_mx(e), v_all) / (jnp.sum(e, axis=-1, keepdims=True) + jnp.exp(sink - m))
        outs += [o[g * CHUNK:(g + 1) * CHUNK] for g in range(group)]
    for c in range(n_heads // 2):
        o_ref[:, c * LANES:(c + 1) * LANES] = (outs[2 * c] + pltpu.roll(outs[2 * c + 1], HEAD, 1)).astype(o_ref.dtype)


def _swa_attention(uq, kd, vp, sinks, geo, n_heads):
    rows = uq.shape[0]
    cpb, ncc, nlc = geo.cpb, geo.ncc, geo.nlc
    kvw = SWA_KV_HEADS * LANES
    ctx_per = geo.rpb // geo.n_ctx

    def lat_block(shift):
        def index(b, j):
            jl = jnp.clip(j - ncc + shift, 0, nlc - 1)
            return (b * cpb + ncc + jl, 0)
        return pl.BlockSpec((CHUNK, kvw), index)

    ctx_spec = pl.BlockSpec((geo.n_ctx, kvw), lambda b, j: (b * ctx_per, 0))
    kv_specs = [lat_block(-1), lat_block(0), lat_block(1), ctx_spec]
    return pl.pallas_call(
        functools.partial(_swa_kernel, n_heads, ncc, geo.n_lat, geo.n_ctx),
        grid=(geo.batch, cpb),
        in_specs=[pl.BlockSpec(memory_space=pltpu.SMEM),
                  pl.BlockSpec((CHUNK, n_heads * LANES), lambda b, j: (b * cpb + j, 0))] + kv_specs + kv_specs,
        out_specs=pl.BlockSpec((CHUNK, n_heads * HEAD), lambda b, j: (b * cpb + j, 0)),
        out_shape=jax.ShapeDtypeStruct((rows, n_heads * HEAD), MXU_DTYPE),
        compiler_params=_params("arbitrary", "arbitrary"),
        name="swa_attention",
    )(sinks, uq, kd, kd, kd, kd, vp, vp, vp, vp)


def _diff_kernel(lambda_init, q_ref, k_ref, v_ref, lam_ref, g_ref, o_ref):
    lp = lam_ref[...]
    lam = (jnp.exp(jnp.sum(lp[0:1] * lp[1:2], axis=-1, keepdims=True))
           - jnp.exp(jnp.sum(lp[2:3] * lp[3:4], axis=-1, keepdims=True)) + lambda_init)
    for hh in range(DIFF_HEADS_PER_STEP):
        v = v_ref[:, hh * LANES:(hh + 1) * LANES]
        parts = []
        for t in range(2):
            lanes = slice((2 * hh + t) * LANES, (2 * hh + t + 1) * LANES)
            s = _dot_nt(q_ref[:, lanes], k_ref[:, lanes])
            e = jnp.exp(s - jnp.max(s, axis=-1, keepdims=True))
            parts.append(_dot(_mx(e), v) / jnp.sum(e, axis=-1, keepdims=True))
        o = parts[0] - lam * parts[1]
        o_ref[:, hh * LANES:(hh + 1) * LANES] = (_rms(o) * g_ref[...] * (1.0 - lambda_init)).astype(o_ref.dtype)


def _diff_attention(uq, kd, v, lam_params, subln_g, geo, n_heads, lambda_init):
    tq = DIFF_TQ
    hp = DIFF_HEADS_PER_STEP
    nq = geo.n_lat // tq
    upb = geo.rpb // tq
    ucx = geo.n_ctx // tq
    assert n_heads % hp == 0 and 2 * HEAD == LANES
    return pl.pallas_call(
        functools.partial(_diff_kernel, lambda_init),
        grid=(geo.batch, n_heads // hp, nq),
        in_specs=[pl.BlockSpec((tq, 2 * hp * LANES), lambda b, h, j: (b * upb + ucx + j, h)),
                  pl.BlockSpec((geo.rpb, 2 * hp * LANES), lambda b, h, j: (b, h)),
                  pl.BlockSpec((geo.rpb, hp * LANES), lambda b, h, j: (b, h)),
                  pl.BlockSpec(lam_params.shape, lambda b, h, j: (0, 0)),
                  pl.BlockSpec((1, 2 * HEAD), lambda b, h, j: (0, 0))],
        out_specs=pl.BlockSpec((tq, hp * 2 * HEAD), lambda b, h, j: (b * nq + j, h)),
        out_shape=jax.ShapeDtypeStruct((geo.batch * geo.n_lat, n_heads * 2 * HEAD), MXU_DTYPE),
        compiler_params=_params("arbitrary", "arbitrary", "arbitrary"),
        name="diff_attention",
    )(uq, kd, v, lam_params, subln_g)


def _post_kernel(x_ref, a_ref, wo_ref, bo_ref, mod_ref, g_ref, wr_hi_ref, wr_lo_ref, br_ref,
                 xo_ref, h_ref, top_ref, gate_ref):
    mod = mod_ref[...]
    x = x_ref[...] + mod[2:3, :] * (_dot(a_ref[...], wo_ref[...]) + bo_ref[...])
    xo_ref[...] = x
    h = _norm_mod(x, g_ref[...], mod, 3, 4)
    _store_row_tiles(h_ref, h)
    h_hi, h_lo = _split2(h)
    logits = _dot(h_hi, wr_hi_ref[...]) + _dot(h_lo, wr_hi_ref[...]) + _dot(h_hi, wr_lo_ref[...]) + br_ref[...]
    lane = lax.broadcasted_iota(I32, logits.shape, 1)
    lane_f = lane.astype(F32)
    top = jnp.zeros(logits.shape, I32)
    gate = jnp.zeros(logits.shape, F32)
    m0 = None
    for k in range(TOP_K):
        m = jnp.max(logits, axis=-1, keepdims=True)
        idx = jnp.min(jnp.where(logits == m, lane_f, float(LANES)), axis=-1, keepdims=True).astype(I32)
        logits = jnp.where(lane == idx, -jnp.inf, logits)
        m0 = m if k == 0 else m0
        top = jnp.where(lane == k, idx, top)
        gate = jnp.where(lane == k, jnp.exp(m - m0), gate)
    top_ref[...] = top
    gate_ref[...] = gate / jnp.sum(gate, axis=-1, keepdims=True)


def _post(x, a, wo, bo, modtab, gain, wr_hi, wr_lo, br, x_tile, mod_index, name):
    rows, din = a.shape
    d = x.shape[1]
    row_spec = pl.BlockSpec((TILE, d), lambda i: (i, 0))
    meta_spec = pl.BlockSpec((TILE, LANES), lambda i: (i, 0))
    const = lambda shape: pl.BlockSpec(shape, lambda i: (0, 0))
    return pl.pallas_call(
        _post_kernel,
        grid=(rows // TILE,),
        in_specs=[pl.BlockSpec((TILE, d), lambda i: (x_tile(i), 0)),
                  pl.BlockSpec((TILE, din), lambda i: (i, 0)),
                  const((din, d)), const((1, d)),
                  pl.BlockSpec((None, ADA_CHUNKS, d), lambda i: (mod_index(i), 0, 0)),
                  const((1, d)), const((d, LANES)), const((d, LANES)), const((1, LANES))],
        out_specs=[row_spec, pl.BlockSpec((TILE * (d // LANES), LANES), lambda i: (i, 0)), meta_spec, meta_spec],
        out_shape=[jax.ShapeDtypeStruct((rows, d), F32), jax.ShapeDtypeStruct((rows * (d // LANES), LANES), F32),
                   jax.ShapeDtypeStruct((rows, LANES), I32), jax.ShapeDtypeStruct((rows, LANES), F32)],
        compiler_params=_params("arbitrary"),
        name=name,
    )(x, a, wo, bo, modtab, gain, wr_hi, wr_lo, br)


def _moe_pos_kernel(top_ref, pos_ref, meta_ref, count_ref, start_ref):
    phase = pl.program_id(0)
    i = pl.program_id(1)

    @pl.when(jnp.logical_and(phase == 0, i == 0))
    def _():
        count_ref[...] = jnp.zeros_like(count_ref)

    @pl.when(phase == 0)
    def _():
        top = top_ref[...]
        lane = lax.broadcasted_iota(I32, top.shape, 1)
        tile_count = sum(jnp.sum(jnp.where(lane == top[:, k:k + 1], 1.0, 0.0), axis=0, keepdims=True)
                         for k in range(TOP_K))
        count_ref[...] = count_ref[...] + tile_count

    @pl.when(jnp.logical_and(phase == 1, i == 0))
    def _():
        counts = count_ref[...]
        padded = jnp.ceil(counts / EXPERT_TILE) * EXPERT_TILE
        lane8 = lax.broadcasted_iota(I32, counts.shape, 1)
        incl = padded
        shift = 1
        while shift < LANES:
            incl = incl + jnp.where(lane8 >= shift, pltpu.roll(incl, shift, 1), 0.0)
            shift *= 2
        start_ref[...] = incl - padded
        sub = lax.broadcasted_iota(I32, counts.shape, 0)
        meta_ref[...] = jnp.where(sub == 0, counts, jnp.where(sub == 1, incl - padded, 0.0))
        count_ref[...] = jnp.zeros_like(count_ref)

    @pl.when(phase == 1)
    def _():
        r = lax.broadcasted_iota(I32, (TILE, TILE), 0)
        c = lax.broadcasted_iota(I32, (TILE, TILE), 1)
        strict_lower = _mx(jnp.where(r > c, 1.0, 0.0))
        base = count_ref[0:1, :] + start_ref[0:1, :]
        lane_t = lax.broadcasted_iota(I32, (TILE, LANES), 1)
        all_ones = jnp.ones((LANES, LANES), MXU_DTYPE)
        for sub in range(top_ref.shape[0] // TILE):
            rows = slice(sub * TILE, (sub + 1) * TILE)
            top_t = top_ref[rows, :]
            pos = jnp.zeros((TILE, LANES), I32)
            for k in range(TOP_K):
                onehot = jnp.where(lane_t == top_t[:, k:k + 1], 1.0, 0.0)
                before = _dot(strict_lower, _mx(onehot)) + base
                slot = _dot_exact_rhs(_split3(onehot * before), all_ones)
                pos = jnp.where(lane_t == k, slot.astype(I32), pos)
                base = base + jnp.sum(onehot, axis=0, keepdims=True)
            pos_ref[rows, :] = pos
        count_ref[...] = jnp.broadcast_to(base - start_ref[0:1, :], count_ref.shape)


def _moe_pos(top):
    rows = top.shape[0]
    step_rows = POS_ROWS if rows % POS_ROWS == 0 else TILE
    return pl.pallas_call(
        _moe_pos_kernel,
        grid=(2, rows // step_rows),
        in_specs=[pl.BlockSpec((step_rows, LANES), lambda p, i: (i, 0))],
        out_specs=[pl.BlockSpec((step_rows, LANES), lambda p, i: (i * p, 0)),
                   pl.BlockSpec((8, LANES), lambda p, i: (0, 0))],
        out_shape=[jax.ShapeDtypeStruct((rows, LANES), I32), jax.ShapeDtypeStruct((8, LANES), F32)],
        scratch_shapes=[pltpu.VMEM((8, LANES), F32), pltpu.VMEM((8, LANES), F32)],
        compiler_params=_params("arbitrary", "arbitrary"),
        name="moe_pos",
    )(top)


def _token_copy(nc, src_ref, src_tok, dst_ref, dst_tok, sem):
    def start_row(tok):
        return tok * nc if isinstance(tok, int) else pl.multiple_of(tok * nc, nc)
    return pltpu.make_async_copy(src_ref.at[pl.ds(start_row(src_tok), nc)],
                                 dst_ref.at[pl.ds(start_row(dst_tok), nc)], sem)


def _dispatch_kernel(nc, pos_ref, fill_ref, h_ref, xs_ref, zero_ref, sem):
    rows = h_ref.shape[0] // nc
    base = pl.program_id(0) * (rows * TOP_K)

    @pl.when(pl.program_id(0) == 0)
    def _():
        zero_ref[...] = jnp.zeros_like(zero_ref)

        def fill_copy(j):
            start = pl.multiple_of(jnp.maximum(fill_ref[j], 0) * nc, EXPERT_TILE * nc)
            return pltpu.make_async_copy(zero_ref, xs_ref.at[pl.ds(start, EXPERT_TILE * nc)], sem)

        for j in range(fill_ref.shape[0]):
            pl.when(fill_ref[j] >= 0)(lambda j=j: fill_copy(j).start())
        for j in range(fill_ref.shape[0]):
            pl.when(fill_ref[j] >= 0)(lambda j=j: fill_copy(j).wait())

    for t in range(rows):
        for k in range(TOP_K):
            _token_copy(nc, h_ref, t, xs_ref, pos_ref[base + t * TOP_K + k], sem).start(priority=k % 2)
    for k in range(TOP_K):
        pltpu.make_async_copy(h_ref, xs_ref.at[pl.ds(0, rows * nc)], sem).wait()


def _dispatch(pos_flat, fill_rows, h_tiles, n_slots):
    nc = h_tiles.shape[0] * TOP_K // pos_flat.shape[0]
    rows = h_tiles.shape[0] // nc
    return pl.pallas_call(
        functools.partial(_dispatch_kernel, nc),
        grid_spec=pltpu.PrefetchScalarGridSpec(
            num_scalar_prefetch=2,
            grid=(rows // DISPATCH_ROWS,),
            in_specs=[pl.BlockSpec((DISPATCH_ROWS * nc, LANES), lambda i, pos, fill: (i, 0))],
            out_specs=pl.BlockSpec(memory_space=pl.ANY),
            scratch_shapes=[pltpu.VMEM((EXPERT_TILE * nc, LANES), F32), pltpu.SemaphoreType.DMA(())]),
        out_shape=jax.ShapeDtypeStruct((n_slots * nc, LANES), F32),
        compiler_params=_params("arbitrary"),
        name="moe_dispatch",
    )(pos_flat, fill_rows, h_tiles)


def _expert_kernel(d_expert, te_ref, na_ref, x_ref, wgu_ref, bgu_ref, wd_ref, bd_ref, o_ref, wgu_mx, wd_mx):
    i = pl.program_id(0)
    active = i < na_ref[0]
    new_expert = jnp.logical_or(i == 0, te_ref[i] != te_ref[jnp.maximum(i - 1, 0)])

    @pl.when(jnp.logical_and(active, new_expert))
    def _():
        wgu_mx[...] = _mx(wgu_ref[...])
        wd_mx[...] = _mx(wd_ref[...])

    @pl.when(active)
    def _():
        x = _load_row_tiles(x_ref, EXPERT_TILE, wgu_mx.shape[0] // LANES)
        gu = _dot(_mx(x), wgu_mx[...]) + bgu_ref[...]
        glu = jnp.minimum(gu[:, :d_expert], SWIGLU_LIMIT)
        lin = jnp.clip(gu[:, d_expert:], -SWIGLU_LIMIT, SWIGLU_LIMIT)
        act = glu * _sigmoid(SWIGLU_ALPHA * glu) * (lin + 1.0)
        _store_row_tiles(o_ref, _dot(_mx(act), wd_mx[...]) + bd_ref[...])

    @pl.when(jnp.logical_not(active))
    def _():
        o_ref[...] = jnp.zeros_like(o_ref)


def _experts(layer, tile_expert, n_active, xs, w_gu, b_gu, w_down, b_down):
    depth, n_exp, d, two_de = w_gu.shape
    nc = d // LANES
    n_slots = xs.shape[0] // nc
    de = two_de // 2
    n_tiles = n_slots // EXPERT_TILE
    row = lambda i, te, na: (jnp.minimum(i, na[0] - 1), 0)
    by_expert = lambda i, te, na: (layer, te[i], 0, 0)
    return pl.pallas_call(
        functools.partial(_expert_kernel, de),
        grid_spec=pltpu.PrefetchScalarGridSpec(
            num_scalar_prefetch=2,
            grid=(n_tiles,),
            in_specs=[pl.BlockSpec((EXPERT_TILE * nc, LANES), row),
                      pl.BlockSpec((None, None, d, two_de), by_expert),
                      pl.BlockSpec((None, None, 1, two_de), by_expert),
                      pl.BlockSpec((None, None, de, d), by_expert),
                      pl.BlockSpec((None, None, 1, d), by_expert)],
            out_specs=pl.BlockSpec((EXPERT_TILE * nc, LANES), lambda i, te, na: (i, 0)),
            scratch_shapes=[pltpu.VMEM((d, two_de), MXU_DTYPE), pltpu.VMEM((de, d), MXU_DTYPE)]),
        out_shape=jax.ShapeDtypeStruct((n_slots * nc, LANES), F32),
        compiler_params=_params("arbitrary"),
        name="moe_experts",
    )(tile_expert, n_active, xs, w_gu, b_gu.reshape(depth, n_exp, 1, two_de), w_down,
      b_down.reshape(depth, n_exp, 1, d))


def _combine_kernel(final, n_tiles, pos_ref, x_ref, gate_ref, mod_ref, gfin_ref, ys_ref, o_ref, buf, sem):
    i = pl.program_id(0)
    nc = x_ref.shape[1] // LANES

    def issue(tile, slot):
        base = tile * (TILE * TOP_K)
        for t in range(TILE):
            for k in range(TOP_K):
                _token_copy(nc, ys_ref, pos_ref[base + t * TOP_K + k], buf.at[slot, k], t,
                            sem.at[slot]).start(priority=k % 2)

    pl.when(i == 0)(lambda: issue(0, 0))
    for slot in range(2):
        pl.when(jnp.logical_and(i + 1 < n_tiles, (i + 1) % 2 == slot))(lambda slot=slot: issue(i + 1, slot))

    slot = i % 2
    for k in range(TOP_K):
        pltpu.make_async_copy(ys_ref.at[pl.ds(0, TILE * nc)], buf.at[slot, k], sem.at[slot]).wait()
    gates = gate_ref[...]
    f = gates[:, 0:1] * _load_row_tiles(buf.at[slot, 0], TILE, nc)
    for k in range(1, TOP_K):
        f = f + gates[:, k:k + 1] * _load_row_tiles(buf.at[slot, k], TILE, nc)
    x = x_ref[...] + mod_ref[5:6, :] * f
    if final:
        x = _rms(x) * gfin_ref[...]
    o_ref[...] = x


def _combine(pos_flat, x, gates, modtab, g_final, ys, mod_index, final):
    rows, d = x.shape
    return pl.pallas_call(
        functools.partial(_combine_kernel, final, rows // TILE),
        grid_spec=pltpu.PrefetchScalarGridSpec(
            num_scalar_prefetch=1,
            grid=(rows // TILE,),
            in_specs=[pl.BlockSpec((TILE, d), lambda i, pos: (i, 0)),
                      pl.BlockSpec((TILE, LANES), lambda i, pos: (i, 0)),
                      pl.BlockSpec((None, ADA_CHUNKS, d), lambda i, pos: (mod_index(i), 0, 0)),
                      pl.BlockSpec((1, d), lambda i, pos: (0, 0)),
                      pl.BlockSpec(memory_space=pl.ANY)],
            out_specs=pl.BlockSpec((TILE, d), lambda i, pos: (i, 0)),
            scratch_shapes=[pltpu.VMEM((2, TOP_K, TILE * (d // LANES), LANES), F32),
                            pltpu.SemaphoreType.DMA((2,))]),
        out_shape=jax.ShapeDtypeStruct((rows, d), F32),
        compiler_params=_params("arbitrary"),
        name="moe_combine",
    )(pos_flat, x, gates, modtab, g_final, ys)


def _moe(layer, x, h, top, gates, modtab, mod_index, w_gu, b_gu, w_down, b_down, g_final, final):
    rows = top.shape[0]
    n_exp = w_gu.shape[1]
    pos, meta = _moe_pos(top)
    counts = meta[0, :n_exp].astype(I32)
    starts = meta[1, :n_exp].astype(I32)
    n_tiles = rows * TOP_K // EXPERT_TILE + n_exp
    tiles_per = (counts + EXPERT_TILE - 1) // EXPERT_TILE
    tile_start = starts // EXPERT_TILE
    tile_end = tile_start + tiles_per
    n_active = tile_end[-1]
    tile = jnp.minimum(jnp.arange(n_tiles, dtype=I32), n_active - 1)
    tile_expert = jnp.sum(tile_end[None, :] <= tile[:, None], axis=1).astype(I32)
    pos_flat = pos[:, :TOP_K].reshape(-1)
    last_tile = jnp.where(tiles_per > 0, tile_end - 1, -1)
    tail_tile = n_active + jnp.arange(n_exp, dtype=I32)
    tail_tile = jnp.where(tail_tile < n_tiles, tail_tile, -1)
    fill_tiles = jnp.concatenate([last_tile, tail_tile])
    fill_rows = jnp.where(fill_tiles >= 0, fill_tiles * EXPERT_TILE, -1).astype(I32)
    xs = _dispatch(pos_flat, fill_rows, h, n_tiles * EXPERT_TILE)
    ys = _experts(layer, tile_expert, n_active.reshape(1), xs, w_gu, b_gu, w_down, b_down)
    return _combine(pos_flat, x, gates, modtab, g_final, ys, mod_index, final)


def _rot_cols(w):
    lead = w.shape[:-1]
    blocks = w.reshape(lead + (-1, 2, HEAD // 2))
    return jnp.concatenate([-blocks[..., 1:2, :], blocks[..., 0:1, :]], axis=-2).reshape(w.shape)


def _pair_cols(a, b):
    lead = a.shape[:-1]
    a3 = a.reshape(lead + (-1, HEAD))
    b3 = b.reshape(lead + (-1, HEAD))
    return jnp.concatenate([a3, b3], axis=-1).reshape(lead + (-1,))


def _rope_tables(geo):
    t = jnp.arange(geo.n_lat)
    rowp = (t // GRID_W).astype(F32)
    colp = (t % GRID_W).astype(F32)
    quarter = HEAD // 4
    inv_freq = ROPE_BASE ** (-jnp.arange(quarter, dtype=F32) / quarter)
    ang = jnp.concatenate([rowp[:, None] * inv_freq, colp[:, None] * inv_freq], axis=-1)
    cos = jnp.concatenate([jnp.ones((geo.n_ctx, HEAD // 2), F32), jnp.cos(ang)], axis=0)
    sin = jnp.concatenate([jnp.zeros((geo.n_ctx, HEAD // 2), F32), jnp.sin(ang)], axis=0)
    cos64 = jnp.concatenate([cos, cos], axis=-1)
    sin64 = jnp.concatenate([sin, sin], axis=-1)
    scale = HEAD ** -0.5
    q_tab = jnp.concatenate([cos64, sin64], axis=-1) * scale
    k_cos = jnp.concatenate([cos64, cos64], axis=-1)
    k_sin = jnp.concatenate([sin64, sin64], axis=-1)
    return q_tab, k_cos, k_sin


def _row(v):
    return v.reshape(1, -1).astype(F32)


def kernel(x, c, ctx, c_ctx, ada_w, ada_b, g_mix, g_ffn, g_final, conv_w_pw1, conv_b_pw1, conv_w_dw, conv_b_dw, conv_ln_g, conv_ln_b, conv_w_pw2, conv_b_pw2, ssm_w_in, ssm_w_conv, ssm_b_conv, ssm_a_log, ssm_dt_bias, ssm_d, ssm_norm_g, ssm_w_out, swa_w_qkv, swa_b_qkv, swa_sinks, swa_w_o, swa_b_o, diff_w_qkv, diff_lambda_q1, diff_lambda_k1, diff_lambda_q2, diff_lambda_k2, diff_subln_g, diff_w_o, moe_w_router, moe_b_router, moe_w_gu, moe_b_gu, moe_w_down, moe_b_down):
    batch, n_lat, d = x.shape
    n_ctx = ctx.shape[1]
    depth = ada_w.shape[0]
    geo = _Geo(batch, n_ctx, n_lat)
    n_exp = moe_w_router.shape[-1]
    q_tab, k_cos, k_sin = _rope_tables(geo)

    cond_rows = 16
    assert batch + 1 <= cond_rows
    cond = jnp.zeros((cond_rows, d), F32).at[:batch].set(c).at[batch].set(c_ctx)
    ada = _adaln(cond, ada_w, ada_b)
    mod_lat = ada[:, :batch].reshape(depth, batch, ADA_CHUNKS, d)
    mod_ctx = jnp.broadcast_to(ada[:, batch].reshape(depth, 1, ADA_CHUNKS, d), mod_lat.shape)
    modtabs = jnp.stack([mod_ctx, mod_lat], axis=2).reshape(depth, 2 * batch, ADA_CHUNKS, d)

    xs = jnp.concatenate([ctx, x], axis=1).reshape(geo.rows, d)
    zero_bias = jnp.zeros((1, d), F32)
    out = None
    for i in range(depth):
        kind, j = i % 4, i // 4
        ctx_out = i < depth - 1
        modtab = modtabs[i]
        gain = _row(g_mix[i])
        if kind == 0:
            w1 = _mx(conv_w_pw1[j])
            b1 = _row(conv_b_pw1[j])
            u, = _norm_proj(xs, gain, modtab, geo, [("glu", [w1[:, :d], b1[:, :d], w1[:, d:], b1[:, d:]], F32)],
                            "conv_pw1_glu")
            a = _dwconv(u, conv_w_dw[j], conv_b_dw[j], geo, "ln_silu", [_row(conv_ln_g[j]), _row(conv_ln_b[j])],
                        d, MXU_DTYPE, "conv_dw_ln")
            wo, bo = _mx(conv_w_pw2[j]), _row(conv_b_pw2[j])
        elif kind == 1:
            di = ssm_norm_g.shape[-1]
            heads = ssm_a_log.shape[-1]
            conv_dim = ssm_w_conv.shape[-1]
            w_in = ssm_w_in[j]
            w_dt = w_in[:, di + conv_dim:].reshape(d, 2, heads)
            w_dt = jnp.pad(w_dt, ((0, 0), (0, 0), (0, LANES - heads))).reshape(d, 2 * LANES)
            w_dt_hi = _mx(w_dt)
            w_dt_lo = _mx(w_dt - w_dt_hi.astype(F32))
            z, xbc, dt_raw = _norm_proj(xs, gain, modtab, geo, [
                ("plain", [_mx(w_in[:, :di]), jnp.zeros((1, di), F32)], MXU_DTYPE),
                ("plain", [_mx(w_in[:, di:di + conv_dim]), jnp.zeros((1, conv_dim), F32)], F32),
                ("precise", [w_dt_hi, w_dt_lo, jnp.zeros((1, 2 * LANES), F32)], F32)], "ssm_in")
            tc = 1024 if conv_dim % 1024 == 0 else 512
            xbc = _dwconv(xbc, ssm_w_conv[j], ssm_b_conv[j], geo, "silu", [], tc, F32, "ssm_conv")
            y2 = _ssd(xbc, dt_raw, ssm_dt_bias[j], ssm_a_log[j], geo, di, heads)
            a = _ssm_finish(y2, xbc, z, _row(jnp.repeat(ssm_d[j], HEAD)), _row(ssm_norm_g[j]))
            wo, bo = _mx(ssm_w_out[j]), zero_bias
        elif kind == 2:
            nh = swa_sinks.shape[-1]
            nq, nkv = nh * HEAD, SWA_KV_HEADS * HEAD
            w, b = swa_w_qkv[j], swa_b_qkv[j][None, :]
            wq, wk, wv = w[:, :nq], w[:, nq:nq + nkv], w[:, nq + nkv:]
            bq, bk, bv = b[:, :nq], b[:, nq:nq + nkv], b[:, nq + nkv:]
            uq, kd, vp = _norm_proj(xs, gain, modtab, geo, [
                ("tab1", [_mx(_pair_cols(wq, _rot_cols(wq))), _pair_cols(bq, _rot_cols(bq)), q_tab], MXU_DTYPE),
                ("tab2", [_mx(_pair_cols(wk, wk)), _pair_cols(bk, bk), k_cos,
                          _mx(_pair_cols(_rot_cols(wk), _rot_cols(wk))), _pair_cols(_rot_cols(bk), _rot_cols(bk)),
                          k_sin], MXU_DTYPE),
                ("plain", [_mx(_pair_cols(wv, jnp.zeros_like(wv))), _pair_cols(bv, jnp.zeros_like(bv))],
                 MXU_DTYPE)], "swa_qkv")
            a = _swa_attention(uq, kd, vp, swa_sinks[j].astype(F32), geo, nh)
            wo, bo = _mx(swa_w_o[j]), _row(swa_b_o[j])
        else:
            assert not ctx_out, "differential attention is only built for a layer without context output"
            lambda_init = 0.8 - 0.6 * math.exp(-0.3 * i)
            w = diff_w_qkv[j]
            wq, wk, wv = w[:, :d], w[:, d:2 * d], w[:, 2 * d:]
            zb = jnp.zeros((1, 2 * d), F32)
            uq, kd, v = _norm_proj(xs, gain, modtab, geo, [
                ("tab1", [_mx(_pair_cols(wq, _rot_cols(wq))), zb, q_tab], MXU_DTYPE),
                ("tab2", [_mx(_pair_cols(wk, wk)), zb, k_cos,
                          _mx(_pair_cols(_rot_cols(wk), _rot_cols(wk))), zb, k_sin], MXU_DTYPE),
                ("plain", [_mx(wv), zero_bias], MXU_DTYPE)], "diff_qkv")
            lam_params = jnp.stack([diff_lambda_q1[j], diff_lambda_k1[j], diff_lambda_q2[j],
                                    diff_lambda_k2[j]]).astype(F32)
            a = _diff_attention(uq, kd, v, lam_params, _row(diff_subln_g[j]), geo, d // (2 * HEAD), lambda_init)
            wo, bo = _mx(diff_w_o[j]), zero_bias

        wr = jnp.pad(moe_w_router[i], ((0, 0), (0, LANES - n_exp)))
        wr_hi = _mx(wr)
        wr_lo = _mx(wr - wr_hi.astype(F32))
        br = jnp.pad(_row(moe_b_router[i]), ((0, 0), (0, LANES - n_exp)), constant_values=-1e30)
        if ctx_out:
            x_tile, mod_index = (lambda t: t), geo.mod_all
        else:
            x_tile, mod_index = geo.lat_tile, geo.mod_lat
            if a.shape[0] == geo.rows:
                a = a.reshape(batch, geo.rpb, -1)[:, n_ctx:].reshape(batch * n_lat, -1)
        xs, h, top, gates = _post(xs, a, wo, bo, modtab, _row(g_ffn[i]), wr_hi, wr_lo, br, x_tile, mod_index,
                                  "post_mixer")
        final = i == depth - 1
        xs = _moe(i, xs, h, top, gates, modtab, mod_index, moe_w_gu, moe_b_gu, moe_w_down, moe_b_down,
                  _row(g_final), final)
        if not ctx_out and not final:
            raise NotImplementedError("a layer without context output must be the last layer")
        out = xs
    return out.reshape(batch, n_lat, d)
```

```python
import functools
import math

import jax
import jax.numpy as jnp
from jax import lax
from jax.experimental import pallas as pl
from jax.experimental.pallas import tpu as pltpu

F32 = jnp.float32
I32 = jnp.int32
MXU_DTYPE = jnp.bfloat16

LANES = 128
SUBLANES = 8
VMEM_LIMIT_BYTES = 56 * 1024 * 1024

TILE = 256
CHUNK = 128
HALO = 16
HEAD = 64
NORM_EPS = 1e-6
ROPE_BASE = 10000.0
GRID_W = 64
ADA_CHUNKS = 6
SSM_GROUPS = 4
SSM_STATE = 128
SWA_KV_HEADS = 4
SWA_WINDOW = 128
TOP_K = 4
SWIGLU_LIMIT = 7.0
SWIGLU_ALPHA = 1.702
EXPERT_TILE = 512
POS_ROWS = 1024
DISPATCH_ROWS = 512
DIFF_TQ = 256
DIFF_HEADS_PER_STEP = 4


def _mx(v):
    return v.astype(MXU_DTYPE)


def _dot(a, b):
    return jnp.dot(a, b, preferred_element_type=F32)


def _dot_nt(a, b):
    return lax.dot_general(a, b, (((1,), (1,)), ((), ())), preferred_element_type=F32)


def _split2(v):
    hi = _mx(v)
    return hi, _mx(v - hi.astype(F32))


def _split3(v):
    hi = _mx(v)
    r = v - hi.astype(F32)
    mid = _mx(r)
    return hi, mid, _mx(r - mid.astype(F32))


def _dot_exact_rhs(parts, m):
    acc = _dot(parts[0], m)
    for p in parts[1:]:
        acc = acc + _dot(p, m)
    return acc


def _sigmoid(v):
    return 1.0 / (1.0 + jnp.exp(-v))


def _softplus(v):
    return jnp.maximum(v, 0.0) + jnp.log(1.0 + jnp.exp(-jnp.abs(v)))


def _rms(v):
    return v * lax.rsqrt(jnp.mean(v * v, axis=-1, keepdims=True) + NORM_EPS)


def _store_row_tiles(ref, value):
    nc = value.shape[1] // LANES
    for c in range(nc):
        ref[pl.ds(c, value.shape[0], stride=nc), :] = value[:, c * LANES:(c + 1) * LANES]


def _load_row_tiles(ref, rows, nc):
    return jnp.concatenate([ref[pl.ds(c, rows, stride=nc), :] for c in range(nc)], axis=1)


def _params(*sem):
    return pltpu.CompilerParams(dimension_semantics=sem, vmem_limit_bytes=VMEM_LIMIT_BYTES)


class _Geo:
    def __init__(self, batch, n_ctx, n_lat):
        assert n_ctx % TILE == 0 and n_lat % TILE == 0 and n_lat % n_ctx == 0
        self.batch, self.n_ctx, self.n_lat = batch, n_ctx, n_lat
        self.rpb = n_ctx + n_lat
        self.tpb = self.rpb // TILE
        self.nct = n_ctx // TILE
        self.nlt = n_lat // TILE
        self.cpb = self.rpb // CHUNK
        self.ncc = n_ctx // CHUNK
        self.nlc = n_lat // CHUNK
        self.rows = batch * self.rpb

    def mod_all(self, i):
        return (i // self.tpb) * 2 + (i % self.tpb >= self.nct).astype(I32)

    def lat_tile(self, i):
        return (i // self.nlt) * self.tpb + self.nct + i % self.nlt

    def mod_lat(self, i):
        return (i // self.nlt) * 2 + 1


def _adaln_kernel(c_ref, w_ref, b_ref, o_ref):
    c = c_ref[...]
    s_hi, s_lo = _split2(c * _sigmoid(c))
    w_hi, w_lo = _split2(w_ref[...])
    o_ref[...] = _dot(s_hi, w_hi) + _dot(s_lo, w_hi) + _dot(s_hi, w_lo) + b_ref[...]


def _adaln(cond, ada_w, ada_b):
    depth, d, n = ada_w.shape
    rows = cond.shape[0]
    return pl.pallas_call(
        _adaln_kernel,
        grid=(depth, n // d),
        in_specs=[pl.BlockSpec((rows, d), lambda l, j: (0, 0)),
                  pl.BlockSpec((None, d, d), lambda l, j: (l, 0, j)),
                  pl.BlockSpec((None, 1, d), lambda l, j: (l, 0, j))],
        out_specs=pl.BlockSpec((None, rows, d), lambda l, j: (l, 0, j)),
        out_shape=jax.ShapeDtypeStruct((depth, rows, n), F32),
        compiler_params=_params("arbitrary", "arbitrary"),
        name="adaln",
    )(cond, ada_w, ada_b.reshape(depth, 1, n))


def _norm_mod(x, g, mod, shift_row, scale_row):
    return _rms(x) * g * (1.0 + mod[scale_row:scale_row + 1, :]) + mod[shift_row:shift_row + 1, :]


def _lane_tile(tab, n):
    return jnp.tile(tab, (1, n // LANES))


_PROJ_OPERANDS = {"plain": "wb", "precise": "wwb", "glu": "wbwb", "tab1": "wbt", "tab2": "wbtwbt"}


def _norm_proj_kernel(modes, x_ref, g_ref, mod_ref, *refs):
    out_refs = refs[len(refs) - len(modes):]
    h = _norm_mod(x_ref[...], g_ref[...], mod_ref[...], 0, 1)
    hb = _mx(h)
    at = 0
    for mode, o_ref in zip(modes, out_refs):
        ops = refs[at:at + len(_PROJ_OPERANDS[mode])]
        at += len(ops)
        n = o_ref.shape[-1]
        if mode == "precise":
            w_hi, w_lo, b = ops
            h_lo = _mx(h - hb.astype(F32))
            acc = _dot(hb, w_hi[...]) + _dot(h_lo, w_hi[...]) + _dot(hb, w_lo[...]) + b[...]
        elif mode == "plain":
            w, b = ops
            acc = _dot(hb, w[...]) + b[...]
        elif mode == "glu":
            w1, b1, w2, b2 = ops
            acc = (_dot(hb, w1[...]) + b1[...]) * _sigmoid(_dot(hb, w2[...]) + b2[...])
        elif mode == "tab1":
            w1, b1, t1 = ops
            acc = (_dot(hb, w1[...]) + b1[...]) * _lane_tile(t1[...], n)
        else:
            w1, b1, t1, w2, b2, t2 = ops
            acc = ((_dot(hb, w1[...]) + b1[...]) * _lane_tile(t1[...], n)
                   + (_dot(hb, w2[...]) + b2[...]) * _lane_tile(t2[...], n))
        o_ref[...] = acc.astype(o_ref.dtype)


def _norm_proj(x, gain, modtab, geo, groups, name):
    rows, d = x.shape
    specs = [pl.BlockSpec((TILE, d), lambda i: (i, 0)),
             pl.BlockSpec((1, d), lambda i: (0, 0)),
             pl.BlockSpec((None, ADA_CHUNKS, d), lambda i: (geo.mod_all(i), 0, 0))]
    operands, out_specs, out_shapes = [], [], []
    for mode, ops, out_dtype in groups:
        n = ops[0].shape[1]
        for kind, op in zip(_PROJ_OPERANDS[mode], ops):
            if kind == "t":
                specs.append(pl.BlockSpec((TILE, LANES), lambda i: (i % geo.tpb, 0)))
            else:
                specs.append(pl.BlockSpec(op.shape, lambda i: (0, 0)))
        operands += ops
        out_specs.append(pl.BlockSpec((TILE, n), lambda i: (i, 0)))
        out_shapes.append(jax.ShapeDtypeStruct((rows, n), out_dtype))
    return pl.pallas_call(
        functools.partial(_norm_proj_kernel, tuple(g[0] for g in groups)),
        grid=(rows // TILE,),
        in_specs=specs,
        out_specs=out_specs,
        out_shape=out_shapes,
        compiler_params=_params("arbitrary"),
        name=name,
    )(x, gain, modtab, *operands)


def _dwconv_kernel(width, mode, tpb, nct, cur_ref, prev_ref, next_ref, w_ref, b_ref, *refs):
    shifted = width > SUBLANES
    pad_ref = refs[-2] if shifted else refs[-1]
    o_ref = refs[-3] if shifted else refs[-2]
    p = pl.program_id(0) % tpb
    has_prev = jnp.logical_and(p != 0, p != nct)
    has_next = jnp.logical_and(p != nct - 1, p != tpb - 1)
    pad_ref[0:HALO, :] = jnp.where(has_prev, prev_ref[...], 0.0)
    pad_ref[HALO:HALO + TILE, :] = cur_ref[...]
    pad_ref[HALO + TILE:, :] = jnp.where(has_next, next_ref[...], 0.0)
    half = (width - 1) // 2
    acc = jnp.broadcast_to(b_ref[...], o_ref.shape)
    if shifted:
        sh_ref = refs[-1]
        span = sh_ref.shape[1]
        for s in range(1, SUBLANES):
            sh_ref[s - 1] = pad_ref[s:s + span, :]
    for k in range(width):
        off = HALO - half + k
        if shifted and off % SUBLANES:
            base = off - off % SUBLANES
            tap = sh_ref[off % SUBLANES - 1, base:base + TILE, :]
        else:
            tap = pad_ref[off:off + TILE, :]
        acc = acc + w_ref[k:k + 1, :] * tap
    if mode == "ln_silu":
        g_ref, beta_ref = refs[:2]
        cen = acc - jnp.mean(acc, axis=-1, keepdims=True)
        acc = cen * lax.rsqrt(jnp.mean(cen * cen, axis=-1, keepdims=True) + NORM_EPS) * g_ref[...] + beta_ref[...]
    o_ref[...] = (acc * _sigmoid(acc)).astype(o_ref.dtype)


def _dwconv(u, w, b, geo, mode, extra, tc, out_dtype, name):
    rows, c = u.shape
    width = w.shape[0]
    per = TILE // HALO
    last = rows // HALO - 1
    specs = [pl.BlockSpec((TILE, tc), lambda i, j: (i, j)),
             pl.BlockSpec((HALO, tc), lambda i, j: (jnp.maximum(i * per - 1, 0), j)),
             pl.BlockSpec((HALO, tc), lambda i, j: (jnp.minimum((i + 1) * per, last), j)),
             pl.BlockSpec((width, tc), lambda i, j: (0, j)),
             pl.BlockSpec((1, tc), lambda i, j: (0, j))]
    specs += [pl.BlockSpec((1, tc), lambda i, j: (0, j)) for _ in extra]
    scratch = [pltpu.VMEM((TILE + 2 * HALO, tc), F32)]
    if width > SUBLANES:
        scratch.append(pltpu.VMEM((SUBLANES - 1, TILE + 2 * HALO - SUBLANES, tc), F32))
    return pl.pallas_call(
        functools.partial(_dwconv_kernel, width, mode, geo.tpb, geo.nct),
        grid=(rows // TILE, c // tc),
        in_specs=specs,
        out_specs=pl.BlockSpec((TILE, tc), lambda i, j: (i, j)),
        out_shape=jax.ShapeDtypeStruct((rows, c), out_dtype),
        scratch_shapes=scratch,
        compiler_params=_params("arbitrary", "arbitrary"),
        name=name,
    )(u, u, u, w, b.reshape(1, c), *extra)


def _ssd_kernel(groups, heads_per_group, x_ref, b_ref, c_ref, dt_ref, bias_ref, alog_ref, e_ref, y_ref, h_ref):
    direction = pl.program_id(0)
    step = pl.program_id(2)
    n_state = SSM_STATE
    rp = heads_per_group * HEAD

    @pl.when(step == 0)
    def _():
        h_ref[...] = jnp.zeros_like(h_ref)

    fwd = direction == 0
    row = lax.broadcasted_iota(I32, (CHUNK, CHUNK), 0)
    col = lax.broadcasted_iota(I32, (CHUNK, CHUNK), 1)
    tri = (row - col) * jnp.where(fwd, 1, -1) >= 0
    dtv = _softplus(dt_ref[...] + bias_ref[...])
    a = dtv * (-jnp.exp(alog_ref[...]))
    tri_m = _mx(jnp.where(tri, 1.0, 0.0))
    a3 = _split3(a)
    acum = _dot(tri_m, a3[0]) + _dot(tri_m, a3[1]) + _dot(tri_m, a3[2])
    acum_t = acum.T
    expand = e_ref[...]
    dt_x = _dot_exact_rhs(_split2(dtv), expand)
    ac_x = _dot_exact_rhs(_split3(acum), expand)
    tot_x = jnp.where(fwd, ac_x[CHUNK - 1:CHUNK, :], ac_x[0:1, :])
    xdt = x_ref[...] * dt_x
    xdt_b = _mx(xdt)
    xdt_end_b = _mx(xdt * jnp.exp(tot_x - ac_x))
    e_ac = jnp.exp(ac_x)
    decay = jnp.exp(tot_x)
    lo_half = lax.broadcasted_iota(I32, (CHUNK, LANES), 1) < HEAD
    for g in range(groups):
        bg = b_ref[:, g * n_state:(g + 1) * n_state]
        cb_g = _mx(c_ref[:, g * n_state:(g + 1) * n_state])
        cb = _dot_nt(cb_g, _mx(bg))
        bg_t = _mx(bg.T)
        h_t = h_ref[g]
        cols = slice(g * rp, (g + 1) * rp)
        y_off = _dot(cb_g, _mx(h_t)) * e_ac[:, cols]
        blocks = []
        for pair in range(heads_per_group // 2):
            xb = xdt_b[:, g * rp + pair * LANES:g * rp + (pair + 1) * LANES]
            halves = []
            for hh in range(2):
                c = g * heads_per_group + 2 * pair + hh
                seg = acum[:, c:c + 1] - acum_t[c:c + 1, :]
                within = jnp.exp(jnp.where(tri, seg, -jnp.inf))
                halves.append(_dot(_mx(cb * within), xb))
            blocks.append(jnp.where(lo_half, halves[0], halves[1]))
        y_ref[:, cols] = (jnp.concatenate(blocks, axis=1) + y_off).astype(y_ref.dtype)
        h_ref[g] = h_t * decay[:, cols] + _dot(bg_t, xdt_end_b[:, cols])


def _ssd(xbc, dt_raw, dt_bias, a_log, geo, d_inner, heads):
    rows = xbc.shape[0]
    groups = SSM_GROUPS
    hpg = heads // groups
    gn = groups * SSM_STATE
    assert d_inner % gn == 0 and hpg % 2 == 0 and heads <= LANES
    cpb, ncc = geo.cpb, geo.ncc

    def rb(d, b, s):
        back = jnp.where(s < ncc, ncc - 1 - s, cpb + ncc - 1 - s)
        return b * cpb + jnp.where(d == 0, s, back)

    pad = LANES - heads
    bias = jnp.pad(dt_bias.astype(F32), ((0, 0), (0, pad))).reshape(2, 1, LANES)
    alog = jnp.pad(a_log.astype(F32), ((0, 0), (0, pad))).reshape(2, 1, LANES)
    expand = (jnp.arange(LANES)[:, None] == jnp.arange(d_inner)[None, :] // HEAD).astype(MXU_DTYPE)
    return pl.pallas_call(
        functools.partial(_ssd_kernel, groups, hpg),
        grid=(2, geo.batch, cpb),
        in_specs=[pl.BlockSpec((CHUNK, d_inner), lambda d, b, s: (rb(d, b, s), 0)),
                  pl.BlockSpec((CHUNK, gn), lambda d, b, s: (rb(d, b, s), d_inner // gn)),
                  pl.BlockSpec((CHUNK, gn), lambda d, b, s: (rb(d, b, s), d_inner // gn + 1)),
                  pl.BlockSpec((CHUNK, LANES), lambda d, b, s: (rb(d, b, s), d)),
                  pl.BlockSpec((None, 1, LANES), lambda d, b, s: (d, 0, 0)),
                  pl.BlockSpec((None, 1, LANES), lambda d, b, s: (d, 0, 0)),
                  pl.BlockSpec((LANES, d_inner), lambda d, b, s: (0, 0))],
        out_specs=pl.BlockSpec((None, CHUNK, d_inner), lambda d, b, s: (d, rb(d, b, s), 0)),
        out_shape=jax.ShapeDtypeStruct((2, rows, d_inner), MXU_DTYPE),
        scratch_shapes=[pltpu.VMEM((groups, SSM_STATE, hpg * HEAD), F32)],
        compiler_params=_params("arbitrary", "arbitrary", "arbitrary"),
        name="ssd_scan",
    )(xbc, xbc, xbc, dt_raw, bias, alog, expand)


def _ssm_finish_kernel(y_ref, xs_ref, z_ref, dskip_ref, g_ref, o_ref):
    y = y_ref[0].astype(F32) + y_ref[1].astype(F32) + xs_ref[...] * dskip_ref[...]
    z = z_ref[...].astype(F32)
    o_ref[...] = (_rms(y * (z * _sigmoid(z))) * g_ref[...]).astype(o_ref.dtype)


def _ssm_finish(y2, xbc, z, d_skip_cols, norm_g):
    rows, di = z.shape
    return pl.pallas_call(
        _ssm_finish_kernel,
        grid=(rows // TILE,),
        in_specs=[pl.BlockSpec((2, TILE, di), lambda i: (0, i, 0)),
                  pl.BlockSpec((TILE, di), lambda i: (i, 0)),
                  pl.BlockSpec((TILE, di), lambda i: (i, 0)),
                  pl.BlockSpec((1, di), lambda i: (0, 0)),
                  pl.BlockSpec((1, di), lambda i: (0, 0))],
        out_specs=pl.BlockSpec((TILE, di), lambda i: (i, 0)),
        out_shape=jax.ShapeDtypeStruct((rows, di), MXU_DTYPE),
        compiler_params=_params("arbitrary"),
        name="ssm_finish",
    )(y2, xbc, z, d_skip_cols, norm_g)


def _swa_kernel(n_heads, ncc, n_lat, n_ctx, sink_ref, q_ref, kp_ref, kc_ref, kn_ref, kx_ref,
                vp_ref, vc_ref, vn_ref, vx_ref, o_ref):
    j = pl.program_id(1)
    is_ctx = j < ncc
    start = (j - ncc) * CHUNK
    span = 3 * CHUNK
    qi = lax.broadcasted_iota(I32, (CHUNK, span), 0)
    rel = lax.broadcasted_iota(I32, (CHUNK, span), 1) - SWA_WINDOW
    kpos = start + rel
    ninf = -jnp.inf
    band = jnp.where(jnp.abs(qi - rel) <= SWA_WINDOW,
                     jnp.where(kpos >= 0, jnp.where(kpos < n_lat, 0.0, ninf), ninf), ninf)
    band = jnp.where(is_ctx, ninf, band)
    group = n_heads // SWA_KV_HEADS
    bias = jnp.concatenate([band, jnp.zeros((CHUNK, n_ctx), F32)], axis=1)
    bias = jnp.concatenate([bias] * group, axis=0)
    head_of_row = lax.broadcasted_iota(I32, (group * CHUNK, 1), 0) // CHUNK
    outs = []
    for kh in range(SWA_KV_HEADS):
        lanes = slice(kh * LANES, (kh + 1) * LANES)
        k_all = jnp.concatenate([kp_ref[:, lanes], kc_ref[:, lanes], kn_ref[:, lanes], kx_ref[:, lanes]], axis=0)
        v_all = jnp.concatenate([vp_ref[:, lanes], vc_ref[:, lanes], vn_ref[:, lanes], vx_ref[:, lanes]], axis=0)
        heads = range(kh * group, (kh + 1) * group)
        q = jnp.concatenate([q_ref[:, h * LANES:(h + 1) * LANES] for h in heads], axis=0)
        sink = jnp.zeros((group * CHUNK, 1), F32)
        for g, h in enumerate(heads):
            sink = jnp.where(head_of_row == g, sink_ref[h], sink)
        s = _dot_nt(q, k_all) + bias
        m = jnp.maximum(jnp.max(s, axis=-1, keepdims=True), sink)
        e = jnp.exp(s - m)
        o = _dot(_mx(e), v_all) / (jnp.sum(e, axis=-1, keepdims=True) + jnp.exp(sink - m))
        outs += [o[g * CHUNK:(g + 1) * CHUNK] for g in range(group)]
    for c in range(n_heads // 2):
        o_ref[:, c * LANES:(c + 1) * LANES] = (outs[2 * c] + pltpu.roll(outs[2 * c + 1], HEAD, 1)).astype(o_ref.dtype)


def _swa_attention(uq, kd, vp, sinks, geo, n_heads):
    rows = uq.shape[0]
    cpb, ncc, nlc = geo.cpb, geo.ncc, geo.nlc
    kvw = SWA_KV_HEADS * LANES
    ctx_per = geo.rpb // geo.n_ctx

    def lat_block(shift):
        def index(b, j):
            jl = jnp.clip(j - ncc + shift, 0, nlc - 1)
            return (b * cpb + ncc + jl, 0)
        return pl.BlockSpec((CHUNK, kvw), index)

    ctx_spec = pl.BlockSpec((geo.n_ctx, kvw), lambda b, j: (b * ctx_per, 0))
    kv_specs = [lat_block(-1), lat_block(0), lat_block(1), ctx_spec]
    return pl.pallas_call(
        functools.partial(_swa_kernel, n_heads, ncc, geo.n_lat, geo.n_ctx),
        grid=(geo.batch, cpb),
        in_specs=[pl.BlockSpec(memory_space=pltpu.SMEM),
                  pl.BlockSpec((CHUNK, n_heads * LANES), lambda b, j: (b * cpb + j, 0))] + kv_specs + kv_specs,
        out_specs=pl.BlockSpec((CHUNK, n_heads * HEAD), lambda b, j: (b * cpb + j, 0)),
        out_shape=jax.ShapeDtypeStruct((rows, n_heads * HEAD), MXU_DTYPE),
        compiler_params=_params("arbitrary", "arbitrary"),
        name="swa_attention",
    )(sinks, uq, kd, kd, kd, kd, vp, vp, vp, vp)


def _diff_kernel(lambda_init, q_ref, k_ref, v_ref, lam_ref, g_ref, o_ref):
    lp = lam_ref[...]
    lam = (jnp.exp(jnp.sum(lp[0:1] * lp[1:2], axis=-1, keepdims=True))
           - jnp.exp(jnp.sum(lp[2:3] * lp[3:4], axis=-1, keepdims=True)) + lambda_init)
    for hh in range(DIFF_HEADS_PER_STEP):
        v = v_ref[:, hh * LANES:(hh + 1) * LANES]
        parts = []
        for t in range(2):
            lanes = slice((2 * hh + t) * LANES, (2 * hh + t + 1) * LANES)
            s = _dot_nt(q_ref[:, lanes], k_ref[:, lanes])
            e = jnp.exp(s - jnp.max(s, axis=-1, keepdims=True))
            parts.append(_dot(_mx(e), v) / jnp.sum(e, axis=-1, keepdims=True))
        o = parts[0] - lam * parts[1]
        o_ref[:, hh * LANES:(hh + 1) * LANES] = (_rms(o) * g_ref[...] * (1.0 - lambda_init)).astype(o_ref.dtype)


def _diff_attention(uq, kd, v, lam_params, subln_g, geo, n_heads, lambda_init):
    tq = DIFF_TQ
    hp = DIFF_HEADS_PER_STEP
    nq = geo.n_lat // tq
    upb = geo.rpb // tq
    ucx = geo.n_ctx // tq
    assert n_heads % hp == 0 and 2 * HEAD == LANES
    return pl.pallas_call(
        functools.partial(_diff_kernel, lambda_init),
        grid=(geo.batch, n_heads // hp, nq),
        in_specs=[pl.BlockSpec((tq, 2 * hp * LANES), lambda b, h, j: (b * upb + ucx + j, h)),
                  pl.BlockSpec((geo.rpb, 2 * hp * LANES), lambda b, h, j: (b, h)),
                  pl.BlockSpec((geo.rpb, hp * LANES), lambda b, h, j: (b, h)),
                  pl.BlockSpec(lam_params.shape, lambda b, h, j: (0, 0)),
                  pl.BlockSpec((1, 2 * HEAD), lambda b, h, j: (0, 0))],
        out_specs=pl.BlockSpec((tq, hp * 2 * HEAD), lambda b, h, j: (b * nq + j, h)),
        out_shape=jax.ShapeDtypeStruct((geo.batch * geo.n_lat, n_heads * 2 * HEAD), MXU_DTYPE),
        compiler_params=_params("arbitrary", "arbitrary", "arbitrary"),
        name="diff_attention",
    )(uq, kd, v, lam_params, subln_g)


def _post_kernel(x_ref, a_ref, wo_ref, bo_ref, mod_ref, g_ref, wr_hi_ref, wr_lo_ref, br_ref,
                 xo_ref, h_ref, top_ref, gate_ref):
    mod = mod_ref[...]
    x = x_ref[...] + mod[2:3, :] * (_dot(a_ref[...], wo_ref[...]) + bo_ref[...])
    xo_ref[...] = x
    h = _norm_mod(x, g_ref[...], mod, 3, 4)
    _store_row_tiles(h_ref, h)
    h_hi, h_lo = _split2(h)
    logits = _dot(h_hi, wr_hi_ref[...]) + _dot(h_lo, wr_hi_ref[...]) + _dot(h_hi, wr_lo_ref[...]) + br_ref[...]
    lane = lax.broadcasted_iota(I32, logits.shape, 1)
    lane_f = lane.astype(F32)
    top = jnp.zeros(logits.shape, I32)
    gate = jnp.zeros(logits.shape, F32)
    m0 = None
    for k in range(TOP_K):
        m = jnp.max(logits, axis=-1, keepdims=True)
        idx = jnp.min(jnp.where(logits == m, lane_f, float(LANES)), axis=-1, keepdims=True).astype(I32)
        logits = jnp.where(lane == idx, -jnp.inf, logits)
        m0 = m if k == 0 else m0
        top = jnp.where(lane == k, idx, top)
        gate = jnp.where(lane == k, jnp.exp(m - m0), gate)
    top_ref[...] = top
    gate_ref[...] = gate / jnp.sum(gate, axis=-1, keepdims=True)


def _post(x, a, wo, bo, modtab, gain, wr_hi, wr_lo, br, x_tile, mod_index, name):
    rows, din = a.shape
    d = x.shape[1]
    row_spec = pl.BlockSpec((TILE, d), lambda i: (i, 0))
    meta_spec = pl.BlockSpec((TILE, LANES), lambda i: (i, 0))
    const = lambda shape: pl.BlockSpec(shape, lambda i: (0, 0))
    return pl.pallas_call(
        _post_kernel,
        grid=(rows // TILE,),
        in_specs=[pl.BlockSpec((TILE, d), lambda i: (x_tile(i), 0)),
                  pl.BlockSpec((TILE, din), lambda i: (i, 0)),
                  const((din, d)), const((1, d)),
                  pl.BlockSpec((None, ADA_CHUNKS, d), lambda i: (mod_index(i), 0, 0)),
                  const((1, d)), const((d, LANES)), const((d, LANES)), const((1, LANES))],
        out_specs=[row_spec, pl.BlockSpec((TILE * (d // LANES), LANES), lambda i: (i, 0)), meta_spec, meta_spec],
        out_shape=[jax.ShapeDtypeStruct((rows, d), F32), jax.ShapeDtypeStruct((rows * (d // LANES), LANES), F32),
                   jax.ShapeDtypeStruct((rows, LANES), I32), jax.ShapeDtypeStruct((rows, LANES), F32)],
        compiler_params=_params("arbitrary"),
        name=name,
    )(x, a, wo, bo, modtab, gain, wr_hi, wr_lo, br)


def _moe_pos_kernel(top_ref, pos_ref, meta_ref, count_ref, start_ref):
    phase = pl.program_id(0)
    i = pl.program_id(1)

    @pl.when(jnp.logical_and(phase == 0, i == 0))
    def _():
        count_ref[...] = jnp.zeros_like(count_ref)

    @pl.when(phase == 0)
    def _():
        top = top_ref[...]
        lane = lax.broadcasted_iota(I32, top.shape, 1)
        tile_count = sum(jnp.sum(jnp.where(lane == top[:, k:k + 1], 1.0, 0.0), axis=0, keepdims=True)
                         for k in range(TOP_K))
        count_ref[...] = count_ref[...] + tile_count

    @pl.when(jnp.logical_and(phase == 1, i == 0))
    def _():
        counts = count_ref[...]
        padded = jnp.ceil(counts / EXPERT_TILE) * EXPERT_TILE
        lane8 = lax.broadcasted_iota(I32, counts.shape, 1)
        incl = padded
        shift = 1
        while shift < LANES:
            incl = incl + jnp.where(lane8 >= shift, pltpu.roll(incl, shift, 1), 0.0)
            shift *= 2
        start_ref[...] = incl - padded
        sub = lax.broadcasted_iota(I32, counts.shape, 0)
        meta_ref[...] = jnp.where(sub == 0, counts, jnp.where(sub == 1, incl - padded, 0.0))
        count_ref[...] = jnp.zeros_like(count_ref)

    @pl.when(phase == 1)
    def _():
        r = lax.broadcasted_iota(I32, (TILE, TILE), 0)
        c = lax.broadcasted_iota(I32, (TILE, TILE), 1)
        strict_lower = _mx(jnp.where(r > c, 1.0, 0.0))
        base = count_ref[0:1, :] + start_ref[0:1, :]
        lane_t = lax.broadcasted_iota(I32, (TILE, LANES), 1)
        all_ones = jnp.ones((LANES, LANES), MXU_DTYPE)
        for sub in range(top_ref.shape[0] // TILE):
            rows = slice(sub * TILE, (sub + 1) * TILE)
            top_t = top_ref[rows, :]
            pos = jnp.zeros((TILE, LANES), I32)
            for k in range(TOP_K):
                onehot = jnp.where(lane_t == top_t[:, k:k + 1], 1.0, 0.0)
                before = _dot(strict_lower, _mx(onehot)) + base
                slot = _dot_exact_rhs(_split3(onehot * before), all_ones)
                pos = jnp.where(lane_t == k, slot.astype(I32), pos)
                base = base + jnp.sum(onehot, axis=0, keepdims=True)
            pos_ref[rows, :] = pos
        count_ref[...] = jnp.broadcast_to(base - start_ref[0:1, :], count_ref.shape)


def _moe_pos(top):
    rows = top.shape[0]
    step_rows = POS_ROWS if rows % POS_ROWS == 0 else TILE
    return pl.pallas_call(
        _moe_pos_kernel,
        grid=(2, rows // step_rows),
        in_specs=[pl.BlockSpec((step_rows, LANES), lambda p, i: (i, 0))],
        out_specs=[pl.BlockSpec((step_rows, LANES), lambda p, i: (i * p, 0)),
                   pl.BlockSpec((8, LANES), lambda p, i: (0, 0))],
        out_shape=[jax.ShapeDtypeStruct((rows, LANES), I32), jax.ShapeDtypeStruct((8, LANES), F32)],
        scratch_shapes=[pltpu.VMEM((8, LANES), F32), pltpu.VMEM((8, LANES), F32)],
        compiler_params=_params("arbitrary", "arbitrary"),
        name="moe_pos",
    )(top)


def _token_copy(nc, src_ref, src_tok, dst_ref, dst_tok, sem):
    def start_row(tok):
        return tok * nc if isinstance(tok, int) else pl.multiple_of(tok * nc, nc)
    return pltpu.make_async_copy(src_ref.at[pl.ds(start_row(src_tok), nc)],
                                 dst_ref.at[pl.ds(start_row(dst_tok), nc)], sem)


def _dispatch_kernel(nc, pos_ref, fill_ref, h_ref, xs_ref, zero_ref, sem):
    rows = h_ref.shape[0] // nc
    base = pl.program_id(0) * (rows * TOP_K)

    @pl.when(pl.program_id(0) == 0)
    def _():
        zero_ref[...] = jnp.zeros_like(zero_ref)

        def fill_copy(j):
            start = pl.multiple_of(jnp.maximum(fill_ref[j], 0) * nc, EXPERT_TILE * nc)
            return pltpu.make_async_copy(zero_ref, xs_ref.at[pl.ds(start, EXPERT_TILE * nc)], sem)

        for j in range(fill_ref.shape[0]):
            pl.when(fill_ref[j] >= 0)(lambda j=j: fill_copy(j).start())
        for j in range(fill_ref.shape[0]):
            pl.when(fill_ref[j] >= 0)(lambda j=j: fill_copy(j).wait())

    for t in range(rows):
        for k in range(TOP_K):
            _token_copy(nc, h_ref, t, xs_ref, pos_ref[base + t * TOP_K + k], sem).start(priority=k % 2)
    for k in range(TOP_K):
        pltpu.make_async_copy(h_ref, xs_ref.at[pl.ds(0, rows * nc)], sem).wait()


def _dispatch(pos_flat, fill_rows, h_tiles, n_slots):
    nc = h_tiles.shape[0] * TOP_K // pos_flat.shape[0]
    rows = h_tiles.shape[0] // nc
    return pl.pallas_call(
        functools.partial(_dispatch_kernel, nc),
        grid_spec=pltpu.PrefetchScalarGridSpec(
            num_scalar_prefetch=2,
            grid=(rows // DISPATCH_ROWS,),
            in_specs=[pl.BlockSpec((DISPATCH_ROWS * nc, LANES), lambda i, pos, fill: (i, 0))],
            out_specs=pl.BlockSpec(memory_space=pl.ANY),
            scratch_shapes=[pltpu.VMEM((EXPERT_TILE * nc, LANES), F32), pltpu.SemaphoreType.DMA(())]),
        out_shape=jax.ShapeDtypeStruct((n_slots * nc, LANES), F32),
        compiler_params=_params("arbitrary"),
        name="moe_dispatch",
    )(pos_flat, fill_rows, h_tiles)


def _expert_kernel(layer, d_expert, te_ref, nx_ref, slot_ref, na_ref, x_ref, wgu_hbm, bgu_ref, wd_hbm, bd_ref,
                   o_ref, wgu_f32, wd_f32, wgu_mx, wd_mx, sem):
    i = pl.program_id(0)
    active = i < na_ref[0]
    expert = te_ref[i]
    slot = slot_ref[i]
    new_expert = jnp.logical_or(i == 0, expert != te_ref[jnp.maximum(i - 1, 0)])

    def fetch(e, s):
        return (pltpu.make_async_copy(wgu_hbm.at[layer, e], wgu_f32.at[s], sem.at[0, s]),
                pltpu.make_async_copy(wd_hbm.at[layer, e], wd_f32.at[s], sem.at[1, s]))

    @pl.when(i == 0)
    def _():
        for copy in fetch(expert, slot):
            copy.start()

    @pl.when(jnp.logical_and(active, new_expert))
    def _():
        for copy in fetch(expert, slot):
            copy.wait()

        @pl.when(nx_ref[i] >= 0)
        def _():
            for copy in fetch(nx_ref[i], 1 - slot):
                copy.start()

        wgu_mx[...] = _mx(wgu_f32[slot])
        wd_mx[...] = _mx(wd_f32[slot])

    @pl.when(active)
    def _():
        x = _load_row_tiles(x_ref, EXPERT_TILE, wgu_mx.shape[0] // LANES)
        gu = _dot(_mx(x), wgu_mx[...]) + bgu_ref[...]
        glu = jnp.minimum(gu[:, :d_expert], SWIGLU_LIMIT)
        lin = jnp.clip(gu[:, d_expert:], -SWIGLU_LIMIT, SWIGLU_LIMIT)
        act = glu * _sigmoid(SWIGLU_ALPHA * glu) * (lin + 1.0)
        _store_row_tiles(o_ref, _dot(_mx(act), wd_mx[...]) + bd_ref[...])

    @pl.when(jnp.logical_not(active))
    def _():
        o_ref[...] = jnp.zeros_like(o_ref)


def _experts(layer, tile_expert, next_expert, weight_slot, n_active, xs, w_gu, b_gu, w_down, b_down):
    depth, n_exp, d, two_de = w_gu.shape
    nc = d // LANES
    n_slots = xs.shape[0] // nc
    de = two_de // 2
    n_tiles = n_slots // EXPERT_TILE
    row = lambda i, te, nx, sl, na: (jnp.minimum(i, na[0] - 1), 0)
    by_expert = lambda i, te, nx, sl, na: (layer, te[i], 0, 0)
    return pl.pallas_call(
        functools.partial(_expert_kernel, layer, de),
        grid_spec=pltpu.PrefetchScalarGridSpec(
            num_scalar_prefetch=4,
            grid=(n_tiles,),
            in_specs=[pl.BlockSpec((EXPERT_TILE * nc, LANES), row),
                      pl.BlockSpec(memory_space=pl.ANY),
                      pl.BlockSpec((None, None, 1, two_de), by_expert),
                      pl.BlockSpec(memory_space=pl.ANY),
                      pl.BlockSpec((None, None, 1, d), by_expert)],
            out_specs=pl.BlockSpec((EXPERT_TILE * nc, LANES), lambda i, te, nx, sl, na: (i, 0)),
            scratch_shapes=[pltpu.VMEM((2, d, two_de), F32), pltpu.VMEM((2, de, d), F32),
                            pltpu.VMEM((d, two_de), MXU_DTYPE), pltpu.VMEM((de, d), MXU_DTYPE),
                            pltpu.SemaphoreType.DMA((2, 2))]),
        out_shape=jax.ShapeDtypeStruct((n_slots * nc, LANES), F32),
        compiler_params=_params("arbitrary"),
        name="moe_experts",
    )(tile_expert, next_expert, weight_slot, n_active, xs, w_gu, b_gu.reshape(depth, n_exp, 1, two_de), w_down,
      b_down.reshape(depth, n_exp, 1, d))


def _combine_kernel(final, n_tiles, pos_ref, x_ref, gate_ref, mod_ref, gfin_ref, ys_ref, o_ref, buf, sem):
    i = pl.program_id(0)
    nc = x_ref.shape[1] // LANES

    def issue(tile, slot):
        base = tile * (TILE * TOP_K)
        for t in range(TILE):
            for k in range(TOP_K):
                _token_copy(nc, ys_ref, pos_ref[base + t * TOP_K + k], buf.at[slot, k], t,
                            sem.at[slot]).start(priority=k % 2)

    pl.when(i == 0)(lambda: issue(0, 0))
    for slot in range(2):
        pl.when(jnp.logical_and(i + 1 < n_tiles, (i + 1) % 2 == slot))(lambda slot=slot: issue(i + 1, slot))

    slot = i % 2
    for k in range(TOP_K):
        pltpu.make_async_copy(ys_ref.at[pl.ds(0, TILE * nc)], buf.at[slot, k], sem.at[slot]).wait()
    gates = gate_ref[...]
    f = gates[:, 0:1] * _load_row_tiles(buf.at[slot, 0], TILE, nc)
    for k in range(1, TOP_K):
        f = f + gates[:, k:k + 1] * _load_row_tiles(buf.at[slot, k], TILE, nc)
    x = x_ref[...] + mod_ref[5:6, :] * f
    if final:
        x = _rms(x) * gfin_ref[...]
    o_ref[...] = x


def _combine(pos_flat, x, gates, modtab, g_final, ys, mod_index, final):
    rows, d = x.shape
    return pl.pallas_call(
        functools.partial(_combine_kernel, final, rows // TILE),
        grid_spec=pltpu.PrefetchScalarGridSpec(
            num_scalar_prefetch=1,
            grid=(rows // TILE,),
            in_specs=[pl.BlockSpec((TILE, d), lambda i, pos: (i, 0)),
                      pl.BlockSpec((TILE, LANES), lambda i, pos: (i, 0)),
                      pl.BlockSpec((None, ADA_CHUNKS, d), lambda i, pos: (mod_index(i), 0, 0)),
                      pl.BlockSpec((1, d), lambda i, pos: (0, 0)),
                      pl.BlockSpec(memory_space=pl.ANY)],
            out_specs=pl.BlockSpec((TILE, d), lambda i, pos: (i, 0)),
            scratch_shapes=[pltpu.VMEM((2, TOP_K, TILE * (d // LANES), LANES), F32),
                            pltpu.SemaphoreType.DMA((2,))]),
        out_shape=jax.ShapeDtypeStruct((rows, d), F32),
        compiler_params=_params("arbitrary"),
        name="moe_combine",
    )(pos_flat, x, gates, modtab, g_final, ys)


def _moe(layer, x, h, top, gates, modtab, mod_index, w_gu, b_gu, w_down, b_down, g_final, final):
    rows = top.shape[0]
    n_exp = w_gu.shape[1]
    pos, meta = _moe_pos(top)
    counts = meta[0, :n_exp].astype(I32)
    starts = meta[1, :n_exp].astype(I32)
    n_tiles = rows * TOP_K // EXPERT_TILE + n_exp
    tiles_per = (counts + EXPERT_TILE - 1) // EXPERT_TILE
    tile_start = starts // EXPERT_TILE
    tile_end = tile_start + tiles_per
    n_active = tile_end[-1]
    tile = jnp.minimum(jnp.arange(n_tiles, dtype=I32), n_active - 1)
    tile_expert = jnp.sum(tile_end[None, :] <= tile[:, None], axis=1).astype(I32)
    pos_flat = pos[:, :TOP_K].reshape(-1)
    last_tile = jnp.where(tiles_per > 0, tile_end - 1, -1)
    tail_tile = n_active + jnp.arange(n_exp, dtype=I32)
    tail_tile = jnp.where(tail_tile < n_tiles, tail_tile, -1)
    fill_tiles = jnp.concatenate([last_tile, tail_tile])
    fill_rows = jnp.where(fill_tiles >= 0, fill_tiles * EXPERT_TILE, -1).astype(I32)
    xs = _dispatch(pos_flat, fill_rows, h, n_tiles * EXPERT_TILE)
    ids = jnp.arange(n_exp, dtype=I32)
    later = jnp.logical_and(ids[None, :] > ids[:, None], tiles_per[None, :] > 0)
    next_nonempty = jnp.min(jnp.where(later, ids[None, :], n_exp), axis=1)
    next_nonempty = jnp.where(next_nonempty < n_exp, next_nonempty, -1).astype(I32)
    rank_nonempty = jnp.cumsum((tiles_per > 0).astype(I32)) - 1
    ys = _experts(layer, tile_expert, next_nonempty[tile_expert], (rank_nonempty[tile_expert] % 2).astype(I32),
                  n_active.reshape(1), xs, w_gu, b_gu, w_down, b_down)
    return _combine(pos_flat, x, gates, modtab, g_final, ys, mod_index, final)


def _rot_cols(w):
    lead = w.shape[:-1]
    blocks = w.reshape(lead + (-1, 2, HEAD // 2))
    return jnp.concatenate([-blocks[..., 1:2, :], blocks[..., 0:1, :]], axis=-2).reshape(w.shape)


def _pair_cols(a, b):
    lead = a.shape[:-1]
    a3 = a.reshape(lead + (-1, HEAD))
    b3 = b.reshape(lead + (-1, HEAD))
    return jnp.concatenate([a3, b3], axis=-1).reshape(lead + (-1,))


def _rope_tables(geo):
    t = jnp.arange(geo.n_lat)
    rowp = (t // GRID_W).astype(F32)
    colp = (t % GRID_W).astype(F32)
    quarter = HEAD // 4
    inv_freq = ROPE_BASE ** (-jnp.arange(quarter, dtype=F32) / quarter)
    ang = jnp.concatenate([rowp[:, None] * inv_freq, colp[:, None] * inv_freq], axis=-1)
    cos = jnp.concatenate([jnp.ones((geo.n_ctx, HEAD // 2), F32), jnp.cos(ang)], axis=0)
    sin = jnp.concatenate([jnp.zeros((geo.n_ctx, HEAD // 2), F32), jnp.sin(ang)], axis=0)
    cos64 = jnp.concatenate([cos, cos], axis=-1)
    sin64 = jnp.concatenate([sin, sin], axis=-1)
    scale = HEAD ** -0.5
    q_tab = jnp.concatenate([cos64, sin64], axis=-1) * scale
    k_cos = jnp.concatenate([cos64, cos64], axis=-1)
    k_sin = jnp.concatenate([sin64, sin64], axis=-1)
    return q_tab, k_cos, k_sin


def _row(v):
    return v.reshape(1, -1).astype(F32)


def kernel(x, c, ctx, c_ctx, ada_w, ada_b, g_mix, g_ffn, g_final, conv_w_pw1, conv_b_pw1, conv_w_dw, conv_b_dw, conv_ln_g, conv_ln_b, conv_w_pw2, conv_b_pw2, ssm_w_in, ssm_w_conv, ssm_b_conv, ssm_a_log, ssm_dt_bias, ssm_d, ssm_norm_g, ssm_w_out, swa_w_qkv, swa_b_qkv, swa_sinks, swa_w_o, swa_b_o, diff_w_qkv, diff_lambda_q1, diff_lambda_k1, diff_lambda_q2, diff_lambda_k2, diff_subln_g, diff_w_o, moe_w_router, moe_b_router, moe_w_gu, moe_b_gu, moe_w_down, moe_b_down):
    batch, n_lat, d = x.shape
    n_ctx = ctx.shape[1]
    depth = ada_w.shape[0]
    geo = _Geo(batch, n_ctx, n_lat)
    n_exp = moe_w_router.shape[-1]
    q_tab, k_cos, k_sin = _rope_tables(geo)

    cond_rows = 16
    assert batch + 1 <= cond_rows
    cond = jnp.zeros((cond_rows, d), F32).at[:batch].set(c).at[batch].set(c_ctx)
    ada = _adaln(cond, ada_w, ada_b)
    mod_lat = ada[:, :batch].reshape(depth, batch, ADA_CHUNKS, d)
    mod_ctx = jnp.broadcast_to(ada[:, batch].reshape(depth, 1, ADA_CHUNKS, d), mod_lat.shape)
    modtabs = jnp.stack([mod_ctx, mod_lat], axis=2).reshape(depth, 2 * batch, ADA_CHUNKS, d)

    xs = jnp.concatenate([ctx, x], axis=1).reshape(geo.rows, d)
    zero_bias = jnp.zeros((1, d), F32)
    out = None
    for i in range(depth):
        kind, j = i % 4, i // 4
        ctx_out = i < depth - 1
        modtab = modtabs[i]
        gain = _row(g_mix[i])
        if kind == 0:
            w1 = _mx(conv_w_pw1[j])
            b1 = _row(conv_b_pw1[j])
            u, = _norm_proj(xs, gain, modtab, geo, [("glu", [w1[:, :d], b1[:, :d], w1[:, d:], b1[:, d:]], F32)],
                            "conv_pw1_glu")
            a = _dwconv(u, conv_w_dw[j], conv_b_dw[j], geo, "ln_silu", [_row(conv_ln_g[j]), _row(conv_ln_b[j])],
                        d, MXU_DTYPE, "conv_dw_ln")
            wo, bo = _mx(conv_w_pw2[j]), _row(conv_b_pw2[j])
        elif kind == 1:
            di = ssm_norm_g.shape[-1]
            heads = ssm_a_log.shape[-1]
            conv_dim = ssm_w_conv.shape[-1]
            w_in = ssm_w_in[j]
            w_dt = w_in[:, di + conv_dim:].reshape(d, 2, heads)
            w_dt = jnp.pad(w_dt, ((0, 0), (0, 0), (0, LANES - heads))).reshape(d, 2 * LANES)
            w_dt_hi = _mx(w_dt)
            w_dt_lo = _mx(w_dt - w_dt_hi.astype(F32))
            z, xbc, dt_raw = _norm_proj(xs, gain, modtab, geo, [
                ("plain", [_mx(w_in[:, :di]), jnp.zeros((1, di), F32)], MXU_DTYPE),
                ("plain", [_mx(w_in[:, di:di + conv_dim]), jnp.zeros((1, conv_dim), F32)], F32),
                ("precise", [w_dt_hi, w_dt_lo, jnp.zeros((1, 2 * LANES), F32)], F32)], "ssm_in")
            tc = 1024 if conv_dim % 1024 == 0 else 512
            xbc = _dwconv(xbc, ssm_w_conv[j], ssm_b_conv[j], geo, "silu", [], tc, F32, "ssm_conv")
            y2 = _ssd(xbc, dt_raw, ssm_dt_bias[j], ssm_a_log[j], geo, di, heads)
            a = _ssm_finish(y2, xbc, z, _row(jnp.repeat(ssm_d[j], HEAD)), _row(ssm_norm_g[j]))
            wo, bo = _mx(ssm_w_out[j]), zero_bias
        elif kind == 2:
            nh = swa_sinks.shape[-1]
            nq, nkv = nh * HEAD, SWA_KV_HEADS * HEAD
            w, b = swa_w_qkv[j], swa_b_qkv[j][None, :]
            wq, wk, wv = w[:, :nq], w[:, nq:nq + nkv], w[:, nq + nkv:]
            bq, bk, bv = b[:, :nq], b[:, nq:nq + nkv], b[:, nq + nkv:]
            uq, kd, vp = _norm_proj(xs, gain, modtab, geo, [
                ("tab1", [_mx(_pair_cols(wq, _rot_cols(wq))), _pair_cols(bq, _rot_cols(bq)), q_tab], MXU_DTYPE),
                ("tab2", [_mx(_pair_cols(wk, wk)), _pair_cols(bk, bk), k_cos,
                          _mx(_pair_cols(_rot_cols(wk), _rot_cols(wk))), _pair_cols(_rot_cols(bk), _rot_cols(bk)),
                          k_sin], MXU_DTYPE),
                ("plain", [_mx(_pair_cols(wv, jnp.zeros_like(wv))), _pair_cols(bv, jnp.zeros_like(bv))],
                 MXU_DTYPE)], "swa_qkv")
            a = _swa_attention(uq, kd, vp, swa_sinks[j].astype(F32), geo, nh)
            wo, bo = _mx(swa_w_o[j]), _row(swa_b_o[j])
        else:
            assert not ctx_out, "differential attention is only built for a layer without context output"
            lambda_init = 0.8 - 0.6 * math.exp(-0.3 * i)
            w = diff_w_qkv[j]
            wq, wk, wv = w[:, :d], w[:, d:2 * d], w[:, 2 * d:]
            zb = jnp.zeros((1, 2 * d), F32)
            uq, kd, v = _norm_proj(xs, gain, modtab, geo, [
                ("tab1", [_mx(_pair_cols(wq, _rot_cols(wq))), zb, q_tab], MXU_DTYPE),
                ("tab2", [_mx(_pair_cols(wk, wk)), zb, k_cos,
                          _mx(_pair_cols(_rot_cols(wk), _rot_cols(wk))), zb, k_sin], MXU_DTYPE),
                ("plain", [_mx(wv), zero_bias], MXU_DTYPE)], "diff_qkv")
            lam_params = jnp.stack([diff_lambda_q1[j], diff_lambda_k1[j], diff_lambda_q2[j],
                                    diff_lambda_k2[j]]).astype(F32)
            a = _diff_attention(uq, kd, v, lam_params, _row(diff_subln_g[j]), geo, d // (2 * HEAD), lambda_init)
            wo, bo = _mx(diff_w_o[j]), zero_bias

        wr = jnp.pad(moe_w_router[i], ((0, 0), (0, LANES - n_exp)))
        wr_hi = _mx(wr)
        wr_lo = _mx(wr - wr_hi.astype(F32))
        br = jnp.pad(_row(moe_b_router[i]), ((0, 0), (0, LANES - n_exp)), constant_values=-1e30)
        if ctx_out:
            x_tile, mod_index = (lambda t: t), geo.mod_all
        else:
            x_tile, mod_index = geo.lat_tile, geo.mod_lat
            if a.shape[0] == geo.rows:
                a = a.reshape(batch, geo.rpb, -1)[:, n_ctx:].reshape(batch * n_lat, -1)
        xs, h, top, gates = _post(xs, a, wo, bo, modtab, _row(g_ffn[i]), wr_hi, wr_lo, br, x_tile, mod_index,
                                  "post_mixer")
        final = i == depth - 1
        xs = _moe(i, xs, h, top, gates, modtab, mod_index, moe_w_gu, moe_b_gu, moe_w_down, moe_b_down,
                  _row(g_final), final)
        if not ctx_out and not final:
            raise NotImplementedError("a layer without context output must be the last layer")
        out = xs
    return out.reshape(batch, n_lat, d)
```

```python
import functools
import math

import jax
import jax.numpy as jnp
from jax import lax
from jax.experimental import pallas as pl
from jax.experimental.pallas import tpu as pltpu

F32 = jnp.float32
I32 = jnp.int32
MXU_DTYPE = jnp.bfloat16

LANES = 128
SUBLANES = 8
VMEM_LIMIT_BYTES = 56 * 1024 * 1024

TILE = 256
CHUNK = 128
HALO = 16
HEAD = 64
NORM_EPS = 1e-6
ROPE_BASE = 10000.0
GRID_W = 64
ADA_CHUNKS = 6
SSM_GROUPS = 4
SSM_STATE = 128
SWA_KV_HEADS = 4
SWA_WINDOW = 128
TOP_K = 4
SWIGLU_LIMIT = 7.0
SWIGLU_ALPHA = 1.702
EXPERT_TILE = 512
POS_ROWS = 1024
DISPATCH_ROWS = 512
DIFF_TQ = 256
DIFF_HEADS_PER_STEP = 4


def _mx(v):
    return v.astype(MXU_DTYPE)


def _dot(a, b):
    return jnp.dot(a, b, preferred_element_type=F32)


def _dot_nt(a, b):
    return lax.dot_general(a, b, (((1,), (1,)), ((), ())), preferred_element_type=F32)


def _split2(v):
    hi = _mx(v)
    return hi, _mx(v - hi.astype(F32))


def _split3(v):
    hi = _mx(v)
    r = v - hi.astype(F32)
    mid = _mx(r)
    return hi, mid, _mx(r - mid.astype(F32))


def _dot_exact_rhs(parts, m):
    acc = _dot(parts[0], m)
    for p in parts[1:]:
        acc = acc + _dot(p, m)
    return acc


def _sigmoid(v):
    return 1.0 / (1.0 + jnp.exp(-v))


def _softplus(v):
    return jnp.maximum(v, 0.0) + jnp.log(1.0 + jnp.exp(-jnp.abs(v)))


def _rms(v):
    return v * lax.rsqrt(jnp.mean(v * v, axis=-1, keepdims=True) + NORM_EPS)


def _store_row_tiles(ref, value):
    nc = value.shape[1] // LANES
    for c in range(nc):
        ref[pl.ds(c, value.shape[0], stride=nc), :] = value[:, c * LANES:(c + 1) * LANES]


def _load_row_tiles(ref, rows, nc):
    return jnp.concatenate([ref[pl.ds(c, rows, stride=nc), :] for c in range(nc)], axis=1)


def _params(*sem):
    return pltpu.CompilerParams(dimension_semantics=sem, vmem_limit_bytes=VMEM_LIMIT_BYTES)


class _Geo:
    def __init__(self, batch, n_ctx, n_lat):
        assert n_ctx % TILE == 0 and n_lat % TILE == 0 and n_lat % n_ctx == 0
        self.batch, self.n_ctx, self.n_lat = batch, n_ctx, n_lat
        self.rpb = n_ctx + n_lat
        self.tpb = self.rpb // TILE
        self.nct = n_ctx // TILE
        self.nlt = n_lat // TILE
        self.cpb = self.rpb // CHUNK
        self.ncc = n_ctx // CHUNK
        self.nlc = n_lat // CHUNK
        self.rows = batch * self.rpb

    def mod_all(self, i):
        return (i // self.tpb) * 2 + (i % self.tpb >= self.nct).astype(I32)

    def lat_tile(self, i):
        return (i // self.nlt) * self.tpb + self.nct + i % self.nlt

    def mod_lat(self, i):
        return (i // self.nlt) * 2 + 1


def _adaln_kernel(c_ref, w_ref, b_ref, o_ref):
    c = c_ref[...]
    s_hi, s_lo = _split2(c * _sigmoid(c))
    w_hi, w_lo = _split2(w_ref[...])
    o_ref[...] = _dot(s_hi, w_hi) + _dot(s_lo, w_hi) + _dot(s_hi, w_lo) + b_ref[...]


def _adaln(cond, ada_w, ada_b):
    depth, d, n = ada_w.shape
    rows = cond.shape[0]
    return pl.pallas_call(
        _adaln_kernel,
        grid=(depth, n // d),
        in_specs=[pl.BlockSpec((rows, d), lambda l, j: (0, 0)),
                  pl.BlockSpec((None, d, d), lambda l, j: (l, 0, j)),
                  pl.BlockSpec((None, 1, d), lambda l, j: (l, 0, j))],
        out_specs=pl.BlockSpec((None, rows, d), lambda l, j: (l, 0, j)),
        out_shape=jax.ShapeDtypeStruct((depth, rows, n), F32),
        compiler_params=_params("arbitrary", "arbitrary"),
        name="adaln",
    )(cond, ada_w, ada_b.reshape(depth, 1, n))


def _norm_mod(x, g, mod, shift_row, scale_row):
    return _rms(x) * g * (1.0 + mod[scale_row:scale_row + 1, :]) + mod[shift_row:shift_row + 1, :]


def _lane_tile(tab, n):
    return jnp.tile(tab, (1, n // LANES))


_PROJ_OPERANDS = {"plain": "wb", "precise": "wwb", "glu": "wbwb", "tab1": "wbt", "tab2": "wbtwbt"}


def _norm_proj_kernel(modes, x_ref, g_ref, mod_ref, *refs):
    out_refs = refs[len(refs) - len(modes):]
    h = _norm_mod(x_ref[...], g_ref[...], mod_ref[...], 0, 1)
    hb = _mx(h)
    at = 0
    for mode, o_ref in zip(modes, out_refs):
        ops = refs[at:at + len(_PROJ_OPERANDS[mode])]
        at += len(ops)
        n = o_ref.shape[-1]
        if mode == "precise":
            w_hi, w_lo, b = ops
            h_lo = _mx(h - hb.astype(F32))
            acc = _dot(hb, w_hi[...]) + _dot(h_lo, w_hi[...]) + _dot(hb, w_lo[...]) + b[...]
        elif mode == "plain":
            w, b = ops
            acc = _dot(hb, w[...]) + b[...]
        elif mode == "glu":
            w1, b1, w2, b2 = ops
            acc = (_dot(hb, w1[...]) + b1[...]) * _sigmoid(_dot(hb, w2[...]) + b2[...])
        elif mode == "tab1":
            w1, b1, t1 = ops
            acc = (_dot(hb, w1[...]) + b1[...]) * _lane_tile(t1[...], n)
        else:
            w1, b1, t1, w2, b2, t2 = ops
            acc = ((_dot(hb, w1[...]) + b1[...]) * _lane_tile(t1[...], n)
                   + (_dot(hb, w2[...]) + b2[...]) * _lane_tile(t2[...], n))
        o_ref[...] = acc.astype(o_ref.dtype)


def _norm_proj(x, gain, modtab, geo, groups, name):
    rows, d = x.shape
    specs = [pl.BlockSpec((TILE, d), lambda i: (i, 0)),
             pl.BlockSpec((1, d), lambda i: (0, 0)),
             pl.BlockSpec((None, ADA_CHUNKS, d), lambda i: (geo.mod_all(i), 0, 0))]
    operands, out_specs, out_shapes = [], [], []
    for mode, ops, out_dtype in groups:
        n = ops[0].shape[1]
        for kind, op in zip(_PROJ_OPERANDS[mode], ops):
            if kind == "t":
                specs.append(pl.BlockSpec((TILE, LANES), lambda i: (i % geo.tpb, 0)))
            else:
                specs.append(pl.BlockSpec(op.shape, lambda i: (0, 0)))
        operands += ops
        out_specs.append(pl.BlockSpec((TILE, n), lambda i: (i, 0)))
        out_shapes.append(jax.ShapeDtypeStruct((rows, n), out_dtype))
    return pl.pallas_call(
        functools.partial(_norm_proj_kernel, tuple(g[0] for g in groups)),
        grid=(rows // TILE,),
        in_specs=specs,
        out_specs=out_specs,
        out_shape=out_shapes,
        compiler_params=_params("arbitrary"),
        name=name,
    )(x, gain, modtab, *operands)


def _dwconv_kernel(width, mode, tpb, nct, cur_ref, prev_ref, next_ref, w_ref, b_ref, *refs):
    shifted = width > SUBLANES
    pad_ref = refs[-2] if shifted else refs[-1]
    o_ref = refs[-3] if shifted else refs[-2]
    p = pl.program_id(0) % tpb
    has_prev = jnp.logical_and(p != 0, p != nct)
    has_next = jnp.logical_and(p != nct - 1, p != tpb - 1)
    pad_ref[0:HALO, :] = jnp.where(has_prev, prev_ref[...], 0.0)
    pad_ref[HALO:HALO + TILE, :] = cur_ref[...]
    pad_ref[HALO + TILE:, :] = jnp.where(has_next, next_ref[...], 0.0)
    half = (width - 1) // 2
    acc = jnp.broadcast_to(b_ref[...], o_ref.shape)
    if shifted:
        sh_ref = refs[-1]
        span = sh_ref.shape[1]
        for s in range(1, SUBLANES):
            sh_ref[s - 1] = pad_ref[s:s + span, :]
    for k in range(width):
        off = HALO - half + k
        if shifted and off % SUBLANES:
            base = off - off % SUBLANES
            tap = sh_ref[off % SUBLANES - 1, base:base + TILE, :]
        else:
            tap = pad_ref[off:off + TILE, :]
        acc = acc + w_ref[k:k + 1, :] * tap
    if mode == "ln_silu":
        g_ref, beta_ref = refs[:2]
        cen = acc - jnp.mean(acc, axis=-1, keepdims=True)
        acc = cen * lax.rsqrt(jnp.mean(cen * cen, axis=-1, keepdims=True) + NORM_EPS) * g_ref[...] + beta_ref[...]
    o_ref[...] = (acc * _sigmoid(acc)).astype(o_ref.dtype)


def _dwconv(u, w, b, geo, mode, extra, tc, out_dtype, name):
    rows, c = u.shape
    width = w.shape[0]
    per = TILE // HALO
    last = rows // HALO - 1
    specs = [pl.BlockSpec((TILE, tc), lambda i, j: (i, j)),
             pl.BlockSpec((HALO, tc), lambda i, j: (jnp.maximum(i * per - 1, 0), j)),
             pl.BlockSpec((HALO, tc), lambda i, j: (jnp.minimum((i + 1) * per, last), j)),
             pl.BlockSpec((width, tc), lambda i, j: (0, j)),
             pl.BlockSpec((1, tc), lambda i, j: (0, j))]
    specs += [pl.BlockSpec((1, tc), lambda i, j: (0, j)) for _ in extra]
    scratch = [pltpu.VMEM((TILE + 2 * HALO, tc), F32)]
    if width > SUBLANES:
        scratch.append(pltpu.VMEM((SUBLANES - 1, TILE + 2 * HALO - SUBLANES, tc), F32))
    return pl.pallas_call(
        functools.partial(_dwconv_kernel, width, mode, geo.tpb, geo.nct),
        grid=(rows // TILE, c // tc),
        in_specs=specs,
        out_specs=pl.BlockSpec((TILE, tc), lambda i, j: (i, j)),
        out_shape=jax.ShapeDtypeStruct((rows, c), out_dtype),
        scratch_shapes=scratch,
        compiler_params=_params("arbitrary", "arbitrary"),
        name=name,
    )(u, u, u, w, b.reshape(1, c), *extra)


def _ssd_kernel(groups, heads_per_group, x_ref, b_ref, c_ref, dt_ref, bias_ref, alog_ref, e_ref, y_ref, h_ref):
    direction = pl.program_id(0)
    step = pl.program_id(2)
    n_state = SSM_STATE
    rp = heads_per_group * HEAD

    @pl.when(step == 0)
    def _():
        h_ref[...] = jnp.zeros_like(h_ref)

    fwd = direction == 0
    row = lax.broadcasted_iota(I32, (CHUNK, CHUNK), 0)
    col = lax.broadcasted_iota(I32, (CHUNK, CHUNK), 1)
    tri = (row - col) * jnp.where(fwd, 1, -1) >= 0
    dtv = _softplus(dt_ref[...] + bias_ref[...])
    a = dtv * (-jnp.exp(alog_ref[...]))
    tri_m = _mx(jnp.where(tri, 1.0, 0.0))
    a3 = _split3(a)
    acum = _dot(tri_m, a3[0]) + _dot(tri_m, a3[1]) + _dot(tri_m, a3[2])
    acum_t = acum.T
    expand = e_ref[...]
    dt_x = _dot_exact_rhs(_split2(dtv), expand)
    ac_x = _dot_exact_rhs(_split3(acum), expand)
    tot_x = jnp.where(fwd, ac_x[CHUNK - 1:CHUNK, :], ac_x[0:1, :])
    xdt = x_ref[...] * dt_x
    xdt_b = _mx(xdt)
    xdt_end_b = _mx(xdt * jnp.exp(tot_x - ac_x))
    e_ac = jnp.exp(ac_x)
    decay = jnp.exp(tot_x)
    lo_half = lax.broadcasted_iota(I32, (CHUNK, LANES), 1) < HEAD
    for g in range(groups):
        bg = b_ref[:, g * n_state:(g + 1) * n_state]
        cb_g = _mx(c_ref[:, g * n_state:(g + 1) * n_state])
        cb = _dot_nt(cb_g, _mx(bg))
        bg_t = _mx(bg.T)
        h_t = h_ref[g]
        cols = slice(g * rp, (g + 1) * rp)
        y_off = _dot(cb_g, _mx(h_t)) * e_ac[:, cols]
        blocks = []
        for pair in range(heads_per_group // 2):
            xb = xdt_b[:, g * rp + pair * LANES:g * rp + (pair + 1) * LANES]
            halves = []
            for hh in range(2):
                c = g * heads_per_group + 2 * pair + hh
                seg = acum[:, c:c + 1] - acum_t[c:c + 1, :]
                within = jnp.exp(jnp.where(tri, seg, -jnp.inf))
                halves.append(_dot(_mx(cb * within), xb))
            blocks.append(jnp.where(lo_half, halves[0], halves[1]))
        y_ref[:, cols] = (jnp.concatenate(blocks, axis=1) + y_off).astype(y_ref.dtype)
        h_ref[g] = h_t * decay[:, cols] + _dot(bg_t, xdt_end_b[:, cols])


def _ssd(xbc, dt_raw, dt_bias, a_log, geo, d_inner, heads):
    rows = xbc.shape[0]
    groups = SSM_GROUPS
    hpg = heads // groups
    gn = groups * SSM_STATE
    assert d_inner % gn == 0 and hpg % 2 == 0 and heads <= LANES
    cpb, ncc = geo.cpb, geo.ncc

    def rb(d, b, s):
        back = jnp.where(s < ncc, ncc - 1 - s, cpb + ncc - 1 - s)
        return b * cpb + jnp.where(d == 0, s, back)

    pad = LANES - heads
    bias = jnp.pad(dt_bias.astype(F32), ((0, 0), (0, pad))).reshape(2, 1, LANES)
    alog = jnp.pad(a_log.astype(F32), ((0, 0), (0, pad))).reshape(2, 1, LANES)
    expand = (jnp.arange(LANES)[:, None] == jnp.arange(d_inner)[None, :] // HEAD).astype(MXU_DTYPE)
    return pl.pallas_call(
        functools.partial(_ssd_kernel, groups, hpg),
        grid=(2, geo.batch, cpb),
        in_specs=[pl.BlockSpec((CHUNK, d_inner), lambda d, b, s: (rb(d, b, s), 0)),
                  pl.BlockSpec((CHUNK, gn), lambda d, b, s: (rb(d, b, s), d_inner // gn)),
                  pl.BlockSpec((CHUNK, gn), lambda d, b, s: (rb(d, b, s), d_inner // gn + 1)),
                  pl.BlockSpec((CHUNK, LANES), lambda d, b, s: (rb(d, b, s), d)),
                  pl.BlockSpec((None, 1, LANES), lambda d, b, s: (d, 0, 0)),
                  pl.BlockSpec((None, 1, LANES), lambda d, b, s: (d, 0, 0)),
                  pl.BlockSpec((LANES, d_inner), lambda d, b, s: (0, 0))],
        out_specs=pl.BlockSpec((None, CHUNK, d_inner), lambda d, b, s: (d, rb(d, b, s), 0)),
        out_shape=jax.ShapeDtypeStruct((2, rows, d_inner), MXU_DTYPE),
        scratch_shapes=[pltpu.VMEM((groups, SSM_STATE, hpg * HEAD), F32)],
        compiler_params=_params("arbitrary", "arbitrary", "arbitrary"),
        name="ssd_scan",
    )(xbc, xbc, xbc, dt_raw, bias, alog, expand)


def _ssm_finish_kernel(y_ref, xs_ref, z_ref, dskip_ref, g_ref, o_ref):
    y = y_ref[0].astype(F32) + y_ref[1].astype(F32) + xs_ref[...] * dskip_ref[...]
    z = z_ref[...].astype(F32)
    o_ref[...] = (_rms(y * (z * _sigmoid(z))) * g_ref[...]).astype(o_ref.dtype)


def _ssm_finish(y2, xbc, z, d_skip_cols, norm_g):
    rows, di = z.shape
    return pl.pallas_call(
        _ssm_finish_kernel,
        grid=(rows // TILE,),
        in_specs=[pl.BlockSpec((2, TILE, di), lambda i: (0, i, 0)),
                  pl.BlockSpec((TILE, di), lambda i: (i, 0)),
                  pl.BlockSpec((TILE, di), lambda i: (i, 0)),
                  pl.BlockSpec((1, di), lambda i: (0, 0)),
                  pl.BlockSpec((1, di), lambda i: (0, 0))],
        out_specs=pl.BlockSpec((TILE, di), lambda i: (i, 0)),
        out_shape=jax.ShapeDtypeStruct((rows, di), MXU_DTYPE),
        compiler_params=_params("arbitrary"),
        name="ssm_finish",
    )(y2, xbc, z, d_skip_cols, norm_g)


def _swa_kernel(n_heads, ncc, n_lat, n_ctx, sink_ref, q_ref, kp_ref, kc_ref, kn_ref, kx_ref,
                vp_ref, vc_ref, vn_ref, vx_ref, o_ref):
    j = pl.program_id(1)
    is_ctx = j < ncc
    start = (j - ncc) * CHUNK
    span = 3 * CHUNK
    qi = lax.broadcasted_iota(I32, (CHUNK, span), 0)
    rel = lax.broadcasted_iota(I32, (CHUNK, span), 1) - SWA_WINDOW
    kpos = start + rel
    ninf = -jnp.inf
    band = jnp.where(jnp.abs(qi - rel) <= SWA_WINDOW,
                     jnp.where(kpos >= 0, jnp.where(kpos < n_lat, 0.0, ninf), ninf), ninf)
    band = jnp.where(is_ctx, ninf, band)
    group = n_heads // SWA_KV_HEADS
    bias = jnp.concatenate([band, jnp.zeros((CHUNK, n_ctx), F32)], axis=1)
    bias = jnp.concatenate([bias] * group, axis=0)
    head_of_row = lax.broadcasted_iota(I32, (group * CHUNK, 1), 0) // CHUNK
    outs = []
    for kh in range(SWA_KV_HEADS):
        lanes = slice(kh * LANES, (kh + 1) * LANES)
        k_all = jnp.concatenate([kp_ref[:, lanes], kc_ref[:, lanes], kn_ref[:, lanes], kx_ref[:, lanes]], axis=0)
        v_all = jnp.concatenate([vp_ref[:, lanes], vc_ref[:, lanes], vn_ref[:, lanes], vx_ref[:, lanes]], axis=0)
        heads = range(kh * group, (kh + 1) * group)
        q = jnp.concatenate([q_ref[:, h * LANES:(h + 1) * LANES] for h in heads], axis=0)
        sink = jnp.zeros((group * CHUNK, 1), F32)
        for g, h in enumerate(heads):
            sink = jnp.where(head_of_row == g, sink_ref[h], sink)
        s = _dot_nt(q, k_all) + bias
        m = jnp.maximum(jnp.max(s, axis=-1, keepdims=True), sink)
        e = jnp.exp(s - m)
        o = _dot(_mx(e), v_all) / (jnp.sum(e, axis=-1, keepdims=True) + jnp.exp(sink - m))
        outs += [o[g * CHUNK:(g + 1) * CHUNK] for g in range(group)]
    for c in range(n_heads // 2):
        o_ref[:, c * LANES:(c + 1) * LANES] = (outs[2 * c] + pltpu.roll(outs[2 * c + 1], HEAD, 1)).astype(o_ref.dtype)


def _swa_attention(uq, kd, vp, sinks, geo, n_heads):
    rows = uq.shape[0]
    cpb, ncc, nlc = geo.cpb, geo.ncc, geo.nlc
    kvw = SWA_KV_HEADS * LANES
    ctx_per = geo.rpb // geo.n_ctx

    def lat_block(shift):
        def index(b, j):
            jl = jnp.clip(j - ncc + shift, 0, nlc - 1)
            return (b * cpb + ncc + jl, 0)
        return pl.BlockSpec((CHUNK, kvw), index)

    ctx_spec = pl.BlockSpec((geo.n_ctx, kvw), lambda b, j: (b * ctx_per, 0))
    kv_specs = [lat_block(-1), lat_block(0), lat_block(1), ctx_spec]
    return pl.pallas_call(
        functools.partial(_swa_kernel, n_heads, ncc, geo.n_lat, geo.n_ctx),
        grid=(geo.batch, cpb),
        in_specs=[pl.BlockSpec(memory_space=pltpu.SMEM),
                  pl.BlockSpec((CHUNK, n_heads * LANES), lambda b, j: (b * cpb + j, 0))] + kv_specs + kv_specs,
        out_specs=pl.BlockSpec((CHUNK, n_heads * HEAD), lambda b, j: (b * cpb + j, 0)),
        out_shape=jax.ShapeDtypeStruct((rows, n_heads * HEAD), MXU_DTYPE),
        compiler_params=_params("arbitrary", "arbitrary"),
        name="swa_attention",
    )(sinks, uq, kd, kd, kd, kd, vp, vp, vp, vp)


def _diff_kernel(lambda_init, q_ref, k_ref, v_ref, lam_ref, g_ref, o_ref):
    lp = lam_ref[...]
    lam = (jnp.exp(jnp.sum(lp[0:1] * lp[1:2], axis=-1, keepdims=True))
           - jnp.exp(jnp.sum(lp[2:3] * lp[3:4], axis=-1, keepdims=True)) + lambda_init)
    for hh in range(DIFF_HEADS_PER_STEP):
        v = v_ref[:, hh * LANES:(hh + 1) * LANES]
        parts = []
        for t in range(2):
            lanes = slice((2 * hh + t) * LANES, (2 * hh + t + 1) * LANES)
            s = _dot_nt(q_ref[:, lanes], k_ref[:, lanes])
            e = jnp.exp(s - jnp.max(s, axis=-1, keepdims=True))
            parts.append(_dot(_mx(e), v) / jnp.sum(e, axis=-1, keepdims=True))
        o = parts[0] - lam * parts[1]
        o_ref[:, hh * LANES:(hh + 1) * LANES] = (_rms(o) * g_ref[...] * (1.0 - lambda_init)).astype(o_ref.dtype)


def _diff_attention(uq, kd, v, lam_params, subln_g, geo, n_heads, lambda_init):
    tq = DIFF_TQ
    hp = DIFF_HEADS_PER_STEP
    nq = geo.n_lat // tq
    upb = geo.rpb // tq
    ucx = geo.n_ctx // tq
    assert n_heads % hp == 0 and 2 * HEAD == LANES
    return pl.pallas_call(
        functools.partial(_diff_kernel, lambda_init),
        grid=(geo.batch, n_heads // hp, nq),
        in_specs=[pl.BlockSpec((tq, 2 * hp * LANES), lambda b, h, j: (b * upb + ucx + j, h)),
                  pl.BlockSpec((geo.rpb, 2 * hp * LANES), lambda b, h, j: (b, h)),
                  pl.BlockSpec((geo.rpb, hp * LANES), lambda b, h, j: (b, h)),
                  pl.BlockSpec(lam_params.shape, lambda b, h, j: (0, 0)),
                  pl.BlockSpec((1, 2 * HEAD), lambda b, h, j: (0, 0))],
        out_specs=pl.BlockSpec((tq, hp * 2 * HEAD), lambda b, h, j: (b * nq + j, h)),
        out_shape=jax.ShapeDtypeStruct((geo.batch * geo.n_lat, n_heads * 2 * HEAD), MXU_DTYPE),
        compiler_params=_params("arbitrary", "arbitrary", "arbitrary"),
        name="diff_attention",
    )(uq, kd, v, lam_params, subln_g)


def _post_kernel(x_ref, a_ref, wo_ref, bo_ref, mod_ref, g_ref, wr_hi_ref, wr_lo_ref, br_ref,
                 xo_ref, h_ref, top_ref, gate_ref):
    mod = mod_ref[...]
    x = x_ref[...] + mod[2:3, :] * (_dot(a_ref[...], wo_ref[...]) + bo_ref[...])
    xo_ref[...] = x
    h = _norm_mod(x, g_ref[...], mod, 3, 4)
    _store_row_tiles(h_ref, h)
    h_hi, h_lo = _split2(h)
    logits = _dot(h_hi, wr_hi_ref[...]) + _dot(h_lo, wr_hi_ref[...]) + _dot(h_hi, wr_lo_ref[...]) + br_ref[...]
    lane = lax.broadcasted_iota(I32, logits.shape, 1)
    lane_f = lane.astype(F32)
    top = jnp.zeros(logits.shape, I32)
    gate = jnp.zeros(logits.shape, F32)
    m0 = None
    for k in range(TOP_K):
        m = jnp.max(logits, axis=-1, keepdims=True)
        idx = jnp.min(jnp.where(logits == m, lane_f, float(LANES)), axis=-1, keepdims=True).astype(I32)
        logits = jnp.where(lane == idx, -jnp.inf, logits)
        m0 = m if k == 0 else m0
        top = jnp.where(lane == k, idx, top)
        gate = jnp.where(lane == k, jnp.exp(m - m0), gate)
    top_ref[...] = top
    gate_ref[...] = gate / jnp.sum(gate, axis=-1, keepdims=True)


def _post(x, a, wo, bo, modtab, gain, wr_hi, wr_lo, br, x_tile, mod_index, name):
    rows, din = a.shape
    d = x.shape[1]
    row_spec = pl.BlockSpec((TILE, d), lambda i: (i, 0))
    meta_spec = pl.BlockSpec((TILE, LANES), lambda i: (i, 0))
    const = lambda shape: pl.BlockSpec(shape, lambda i: (0, 0))
    return pl.pallas_call(
        _post_kernel,
        grid=(rows // TILE,),
        in_specs=[pl.BlockSpec((TILE, d), lambda i: (x_tile(i), 0)),
                  pl.BlockSpec((TILE, din), lambda i: (i, 0)),
                  const((din, d)), const((1, d)),
                  pl.BlockSpec((None, ADA_CHUNKS, d), lambda i: (mod_index(i), 0, 0)),
                  const((1, d)), const((d, LANES)), const((d, LANES)), const((1, LANES))],
        out_specs=[row_spec, pl.BlockSpec((TILE * (d // LANES), LANES), lambda i: (i, 0)), meta_spec, meta_spec],
        out_shape=[jax.ShapeDtypeStruct((rows, d), F32), jax.ShapeDtypeStruct((rows * (d // LANES), LANES), F32),
                   jax.ShapeDtypeStruct((rows, LANES), I32), jax.ShapeDtypeStruct((rows, LANES), F32)],
        compiler_params=_params("arbitrary"),
        name=name,
    )(x, a, wo, bo, modtab, gain, wr_hi, wr_lo, br)


def _moe_pos_kernel(top_ref, pos_ref, meta_ref, count_ref, start_ref):
    phase = pl.program_id(0)
    i = pl.program_id(1)

    @pl.when(jnp.logical_and(phase == 0, i == 0))
    def _():
        count_ref[...] = jnp.zeros_like(count_ref)

    @pl.when(phase == 0)
    def _():
        top = top_ref[...]
        lane = lax.broadcasted_iota(I32, top.shape, 1)
        tile_count = sum(jnp.sum(jnp.where(lane == top[:, k:k + 1], 1.0, 0.0), axis=0, keepdims=True)
                         for k in range(TOP_K))
        count_ref[...] = count_ref[...] + tile_count

    @pl.when(jnp.logical_and(phase == 1, i == 0))
    def _():
        counts = count_ref[...]
        padded = jnp.ceil(counts / EXPERT_TILE) * EXPERT_TILE
        lane8 = lax.broadcasted_iota(I32, counts.shape, 1)
        incl = padded
        shift = 1
        while shift < LANES:
            incl = incl + jnp.where(lane8 >= shift, pltpu.roll(incl, shift, 1), 0.0)
            shift *= 2
        start_ref[...] = incl - padded
        sub = lax.broadcasted_iota(I32, counts.shape, 0)
        meta_ref[...] = jnp.where(sub == 0, counts, jnp.where(sub == 1, incl - padded, 0.0))
        count_ref[...] = jnp.zeros_like(count_ref)

    @pl.when(phase == 1)
    def _():
        r = lax.broadcasted_iota(I32, (TILE, TILE), 0)
        c = lax.broadcasted_iota(I32, (TILE, TILE), 1)
        strict_lower = _mx(jnp.where(r > c, 1.0, 0.0))
        base = count_ref[0:1, :] + start_ref[0:1, :]
        lane_t = lax.broadcasted_iota(I32, (TILE, LANES), 1)
        all_ones = jnp.ones((LANES, LANES), MXU_DTYPE)
        for sub in range(top_ref.shape[0] // TILE):
            rows = slice(sub * TILE, (sub + 1) * TILE)
            top_t = top_ref[rows, :]
            pos = jnp.zeros((TILE, LANES), I32)
            for k in range(TOP_K):
                onehot = jnp.where(lane_t == top_t[:, k:k + 1], 1.0, 0.0)
                before = _dot(strict_lower, _mx(onehot)) + base
                slot = _dot_exact_rhs(_split3(onehot * before), all_ones)
                pos = jnp.where(lane_t == k, slot.astype(I32), pos)
                base = base + jnp.sum(onehot, axis=0, keepdims=True)
            pos_ref[rows, :] = pos
        count_ref[...] = jnp.broadcast_to(base - start_ref[0:1, :], count_ref.shape)


def _moe_pos(top):
    rows = top.shape[0]
    step_rows = POS_ROWS if rows % POS_ROWS == 0 else TILE
    return pl.pallas_call(
        _moe_pos_kernel,
        grid=(2, rows // step_rows),
        in_specs=[pl.BlockSpec((step_rows, LANES), lambda p, i: (i, 0))],
        out_specs=[pl.BlockSpec((step_rows, LANES), lambda p, i: (i * p, 0)),
                   pl.BlockSpec((8, LANES), lambda p, i: (0, 0))],
        out_shape=[jax.ShapeDtypeStruct((rows, LANES), I32), jax.ShapeDtypeStruct((8, LANES), F32)],
        scratch_shapes=[pltpu.VMEM((8, LANES), F32), pltpu.VMEM((8, LANES), F32)],
        compiler_params=_params("arbitrary", "arbitrary"),
        name="moe_pos",
    )(top)


def _token_copy(nc, src_ref, src_tok, dst_ref, dst_tok, sem):
    def start_row(tok):
        return tok * nc if isinstance(tok, int) else pl.multiple_of(tok * nc, nc)
    return pltpu.make_async_copy(src_ref.at[pl.ds(start_row(src_tok), nc)],
                                 dst_ref.at[pl.ds(start_row(dst_tok), nc)], sem)


def _dispatch_kernel(nc, pos_ref, fill_ref, h_ref, xs_ref, zero_ref, sem):
    rows = h_ref.shape[0] // nc
    base = pl.program_id(0) * (rows * TOP_K)

    @pl.when(pl.program_id(0) == 0)
    def _():
        zero_ref[...] = jnp.zeros_like(zero_ref)

        def fill_copy(j):
            start = pl.multiple_of(jnp.maximum(fill_ref[j], 0) * nc, EXPERT_TILE * nc)
            return pltpu.make_async_copy(zero_ref, xs_ref.at[pl.ds(start, EXPERT_TILE * nc)], sem)

        for j in range(fill_ref.shape[0]):
            pl.when(fill_ref[j] >= 0)(lambda j=j: fill_copy(j).start())
        for j in range(fill_ref.shape[0]):
            pl.when(fill_ref[j] >= 0)(lambda j=j: fill_copy(j).wait())

    for t in range(rows):
        for k in range(TOP_K):
            _token_copy(nc, h_ref, t, xs_ref, pos_ref[base + t * TOP_K + k], sem).start(priority=k % 2)
    for k in range(TOP_K):
        pltpu.make_async_copy(h_ref, xs_ref.at[pl.ds(0, rows * nc)], sem).wait()


def _dispatch(pos_flat, fill_rows, h_tiles, n_slots):
    nc = h_tiles.shape[0] * TOP_K // pos_flat.shape[0]
    rows = h_tiles.shape[0] // nc
    return pl.pallas_call(
        functools.partial(_dispatch_kernel, nc),
        grid_spec=pltpu.PrefetchScalarGridSpec(
            num_scalar_prefetch=2,
            grid=(rows // DISPATCH_ROWS,),
            in_specs=[pl.BlockSpec((DISPATCH_ROWS * nc, LANES), lambda i, pos, fill: (i, 0))],
            out_specs=pl.BlockSpec(memory_space=pl.ANY),
            scratch_shapes=[pltpu.VMEM((EXPERT_TILE * nc, LANES), F32), pltpu.SemaphoreType.DMA(())]),
        out_shape=jax.ShapeDtypeStruct((n_slots * nc, LANES), F32),
        compiler_params=_params("arbitrary"),
        name="moe_dispatch",
    )(pos_flat, fill_rows, h_tiles)


def _expert_kernel(layer, d_expert, te_ref, nx_ref, slot_ref, na_ref, x_ref, wgu_hbm, bgu_ref, wd_hbm, bd_ref,
                   o_ref, wgu_f32, wd_f32, wgu_mx, wd_mx, sem):
    i = pl.program_id(0)
    active = i < na_ref[0]
    expert = te_ref[i]
    slot = slot_ref[i]
    new_expert = jnp.logical_or(i == 0, expert != te_ref[jnp.maximum(i - 1, 0)])

    def fetch(e, s):
        return (pltpu.make_async_copy(wgu_hbm.at[layer, e], wgu_f32.at[s], sem.at[0, s]),
                pltpu.make_async_copy(wd_hbm.at[layer, e], wd_f32.at[s], sem.at[1, s]))

    @pl.when(i == 0)
    def _():
        for copy in fetch(expert, slot):
            copy.start()

    @pl.when(jnp.logical_and(active, new_expert))
    def _():
        for copy in fetch(expert, slot):
            copy.wait()

        @pl.when(nx_ref[i] >= 0)
        def _():
            for copy in fetch(nx_ref[i], 1 - slot):
                copy.start()

        wgu_mx[...] = _mx(wgu_f32[slot])
        wd_mx[...] = _mx(wd_f32[slot])

    @pl.when(active)
    def _():
        x = _load_row_tiles(x_ref, EXPERT_TILE, wgu_mx.shape[0] // LANES)
        gu = _dot(_mx(x), wgu_mx[...]) + bgu_ref[...]
        glu = jnp.minimum(gu[:, :d_expert], SWIGLU_LIMIT)
        lin = jnp.clip(gu[:, d_expert:], -SWIGLU_LIMIT, SWIGLU_LIMIT)
        act = glu * _sigmoid(SWIGLU_ALPHA * glu) * (lin + 1.0)
        _store_row_tiles(o_ref, _dot(_mx(act), wd_mx[...]) + bd_ref[...])

    @pl.when(jnp.logical_not(active))
    def _():
        o_ref[...] = jnp.zeros_like(o_ref)


def _experts(layer, tile_expert, next_expert, weight_slot, n_active, xs, w_gu, b_gu, w_down, b_down):
    depth, n_exp, d, two_de = w_gu.shape
    nc = d // LANES
    n_slots = xs.shape[0] // nc
    de = two_de // 2
    n_tiles = n_slots // EXPERT_TILE
    row = lambda i, te, nx, sl, na: (jnp.minimum(i, na[0] - 1), 0)
    by_expert = lambda i, te, nx, sl, na: (layer, te[i], 0, 0)
    return pl.pallas_call(
        functools.partial(_expert_kernel, layer, de),
        grid_spec=pltpu.PrefetchScalarGridSpec(
            num_scalar_prefetch=4,
            grid=(n_tiles,),
            in_specs=[pl.BlockSpec((EXPERT_TILE * nc, LANES), row),
                      pl.BlockSpec(memory_space=pl.ANY),
                      pl.BlockSpec((None, None, 1, two_de), by_expert),
                      pl.BlockSpec(memory_space=pl.ANY),
                      pl.BlockSpec((None, None, 1, d), by_expert)],
            out_specs=pl.BlockSpec((EXPERT_TILE * nc, LANES), lambda i, te, nx, sl, na: (i, 0)),
            scratch_shapes=[pltpu.VMEM((2, d, two_de), F32), pltpu.VMEM((2, de, d), F32),
                            pltpu.VMEM((d, two_de), MXU_DTYPE), pltpu.VMEM((de, d), MXU_DTYPE),
                            pltpu.SemaphoreType.DMA((2, 2))]),
        out_shape=jax.ShapeDtypeStruct((n_slots * nc, LANES), F32),
        compiler_params=_params("arbitrary"),
        name="moe_experts",
    )(tile_expert, next_expert, weight_slot, n_active, xs, w_gu, b_gu.reshape(depth, n_exp, 1, two_de), w_down,
      b_down.reshape(depth, n_exp, 1, d))


def _combine_kernel(final, n_tiles, pos_ref, x_ref, gate_ref, mod_ref, gfin_ref, ys_ref, o_ref, buf, sem):
    i = pl.program_id(0)
    nc = x_ref.shape[1] // LANES

    def issue(tile, slot):
        base = tile * (TILE * TOP_K)
        for t in range(TILE):
            for k in range(TOP_K):
                _token_copy(nc, ys_ref, pos_ref[base + t * TOP_K + k], buf.at[slot, k], t,
                            sem.at[slot]).start(priority=k % 2)

    pl.when(i == 0)(lambda: issue(0, 0))
    for slot in range(2):
        pl.when(jnp.logical_and(i + 1 < n_tiles, (i + 1) % 2 == slot))(lambda slot=slot: issue(i + 1, slot))

    slot = i % 2
    for k in range(TOP_K):
        pltpu.make_async_copy(ys_ref.at[pl.ds(0, TILE * nc)], buf.at[slot, k], sem.at[slot]).wait()
    gates = gate_ref[...]
    f = gates[:, 0:1] * _load_row_tiles(buf.at[slot, 0], TILE, nc)
    for k in range(1, TOP_K):
        f = f + gates[:, k:k + 1] * _load_row_tiles(buf.at[slot, k], TILE, nc)
    x = x_ref[...] + mod_ref[5:6, :] * f
    if final:
        x = _rms(x) * gfin_ref[...]
    o_ref[...] = x


def _combine(pos_flat, x, gates, modtab, g_final, ys, mod_index, final):
    rows, d = x.shape
    return pl.pallas_call(
        functools.partial(_combine_kernel, final, rows // TILE),
        grid_spec=pltpu.PrefetchScalarGridSpec(
            num_scalar_prefetch=1,
            grid=(rows // TILE,),
            in_specs=[pl.BlockSpec((TILE, d), lambda i, pos: (i, 0)),
                      pl.BlockSpec((TILE, LANES), lambda i, pos: (i, 0)),
                      pl.BlockSpec((None, ADA_CHUNKS, d), lambda i, pos: (mod_index(i), 0, 0)),
                      pl.BlockSpec((1, d), lambda i, pos: (0, 0)),
                      pl.BlockSpec(memory_space=pl.ANY)],
            out_specs=pl.BlockSpec((TILE, d), lambda i, pos: (i, 0)),
            scratch_shapes=[pltpu.VMEM((2, TOP_K, TILE * (d // LANES), LANES), F32),
                            pltpu.SemaphoreType.DMA((2,))]),
        out_shape=jax.ShapeDtypeStruct((rows, d), F32),
        compiler_params=_params("arbitrary"),
        name="moe_combine",
    )(pos_flat, x, gates, modtab, g_final, ys)


def _moe(layer, x, h, top, gates, modtab, mod_index, w_gu, b_gu, w_down, b_down, g_final, final):
    rows = top.shape[0]
    n_exp = w_gu.shape[1]
    pos, meta = _moe_pos(top)
    counts = meta[0, :n_exp].astype(I32)
    starts = meta[1, :n_exp].astype(I32)
    n_tiles = rows * TOP_K // EXPERT_TILE + n_exp
    tiles_per = (counts + EXPERT_TILE - 1) // EXPERT_TILE
    tile_start = starts // EXPERT_TILE
    tile_end = tile_start + tiles_per
    n_active = tile_end[-1]
    tile = jnp.minimum(jnp.arange(n_tiles, dtype=I32), n_active - 1)
    tile_expert = jnp.sum(tile_end[None, :] <= tile[:, None], axis=1).astype(I32)
    pos_flat = pos[:, :TOP_K].reshape(-1)
    last_tile = jnp.where(tiles_per > 0, tile_end - 1, -1)
    tail_tile = n_active + jnp.arange(n_exp, dtype=I32)
    tail_tile = jnp.where(tail_tile < n_tiles, tail_tile, -1)
    fill_tiles = jnp.concatenate([last_tile, tail_tile])
    fill_rows = jnp.where(fill_tiles >= 0, fill_tiles * EXPERT_TILE, -1).astype(I32)
    xs = _dispatch(pos_flat, fill_rows, h, n_tiles * EXPERT_TILE)
    ids = jnp.arange(n_exp, dtype=I32)
    later = jnp.logical_and(ids[None, :] > ids[:, None], tiles_per[None, :] > 0)
    next_nonempty = jnp.min(jnp.where(later, ids[None, :], n_exp), axis=1)
    next_nonempty = jnp.where(next_nonempty < n_exp, next_nonempty, -1).astype(I32)
    rank_nonempty = jnp.cumsum((tiles_per > 0).astype(I32)) - 1
    owner = tile_expert[:, None] == ids[None, :]
    tile_next = jnp.sum(jnp.where(owner, next_nonempty[None, :], 0), axis=1).astype(I32)
    tile_slot = jnp.sum(jnp.where(owner, rank_nonempty[None, :] % 2, 0), axis=1).astype(I32)
    ys = _experts(layer, tile_expert, tile_next, tile_slot, n_active.reshape(1), xs, w_gu, b_gu, w_down, b_down)
    return _combine(pos_flat, x, gates, modtab, g_final, ys, mod_index, final)


def _rot_cols(w):
    lead = w.shape[:-1]
    blocks = w.reshape(lead + (-1, 2, HEAD // 2))
    return jnp.concatenate([-blocks[..., 1:2, :], blocks[..., 0:1, :]], axis=-2).reshape(w.shape)


def _pair_cols(a, b):
    lead = a.shape[:-1]
    a3 = a.reshape(lead + (-1, HEAD))
    b3 = b.reshape(lead + (-1, HEAD))
    return jnp.concatenate([a3, b3], axis=-1).reshape(lead + (-1,))


def _rope_tables(geo):
    t = jnp.arange(geo.n_lat)
    rowp = (t // GRID_W).astype(F32)
    colp = (t % GRID_W).astype(F32)
    quarter = HEAD // 4
    inv_freq = ROPE_BASE ** (-jnp.arange(quarter, dtype=F32) / quarter)
    ang = jnp.concatenate([rowp[:, None] * inv_freq, colp[:, None] * inv_freq], axis=-1)
    cos = jnp.concatenate([jnp.ones((geo.n_ctx, HEAD // 2), F32), jnp.cos(ang)], axis=0)
    sin = jnp.concatenate([jnp.zeros((geo.n_ctx, HEAD // 2), F32), jnp.sin(ang)], axis=0)
    cos64 = jnp.concatenate([cos, cos], axis=-1)
    sin64 = jnp.concatenate([sin, sin], axis=-1)
    scale = HEAD ** -0.5
    q_tab = jnp.concatenate([cos64, sin64], axis=-1) * scale
    k_cos = jnp.concatenate([cos64, cos64], axis=-1)
    k_sin = jnp.concatenate([sin64, sin64], axis=-1)
    return q_tab, k_cos, k_sin


def _row(v):
    return v.reshape(1, -1).astype(F32)


def kernel(x, c, ctx, c_ctx, ada_w, ada_b, g_mix, g_ffn, g_final, conv_w_pw1, conv_b_pw1, conv_w_dw, conv_b_dw, conv_ln_g, conv_ln_b, conv_w_pw2, conv_b_pw2, ssm_w_in, ssm_w_conv, ssm_b_conv, ssm_a_log, ssm_dt_bias, ssm_d, ssm_norm_g, ssm_w_out, swa_w_qkv, swa_b_qkv, swa_sinks, swa_w_o, swa_b_o, diff_w_qkv, diff_lambda_q1, diff_lambda_k1, diff_lambda_q2, diff_lambda_k2, diff_subln_g, diff_w_o, moe_w_router, moe_b_router, moe_w_gu, moe_b_gu, moe_w_down, moe_b_down):
    batch, n_lat, d = x.shape
    n_ctx = ctx.shape[1]
    depth = ada_w.shape[0]
    geo = _Geo(batch, n_ctx, n_lat)
    n_exp = moe_w_router.shape[-1]
    q_tab, k_cos, k_sin = _rope_tables(geo)

    cond_rows = 16
    assert batch + 1 <= cond_rows
    cond = jnp.zeros((cond_rows, d), F32).at[:batch].set(c).at[batch].set(c_ctx)
    ada = _adaln(cond, ada_w, ada_b)
    mod_lat = ada[:, :batch].reshape(depth, batch, ADA_CHUNKS, d)
    mod_ctx = jnp.broadcast_to(ada[:, batch].reshape(depth, 1, ADA_CHUNKS, d), mod_lat.shape)
    modtabs = jnp.stack([mod_ctx, mod_lat], axis=2).reshape(depth, 2 * batch, ADA_CHUNKS, d)

    xs = jnp.concatenate([ctx, x], axis=1).reshape(geo.rows, d)
    zero_bias = jnp.zeros((1, d), F32)
    out = None
    for i in range(depth):
        kind, j = i % 4, i // 4
        ctx_out = i < depth - 1
        modtab = modtabs[i]
        gain = _row(g_mix[i])
        if kind == 0:
            w1 = _mx(conv_w_pw1[j])
            b1 = _row(conv_b_pw1[j])
            u, = _norm_proj(xs, gain, modtab, geo, [("glu", [w1[:, :d], b1[:, :d], w1[:, d:], b1[:, d:]], F32)],
                            "conv_pw1_glu")
            a = _dwconv(u, conv_w_dw[j], conv_b_dw[j], geo, "ln_silu", [_row(conv_ln_g[j]), _row(conv_ln_b[j])],
                        d, MXU_DTYPE, "conv_dw_ln")
            wo, bo = _mx(conv_w_pw2[j]), _row(conv_b_pw2[j])
        elif kind == 1:
            di = ssm_norm_g.shape[-1]
            heads = ssm_a_log.shape[-1]
            conv_dim = ssm_w_conv.shape[-1]
            w_in = ssm_w_in[j]
            w_dt = w_in[:, di + conv_dim:].reshape(d, 2, heads)
            w_dt = jnp.pad(w_dt, ((0, 0), (0, 0), (0, LANES - heads))).reshape(d, 2 * LANES)
            w_dt_hi = _mx(w_dt)
            w_dt_lo = _mx(w_dt - w_dt_hi.astype(F32))
            z, xbc, dt_raw = _norm_proj(xs, gain, modtab, geo, [
                ("plain", [_mx(w_in[:, :di]), jnp.zeros((1, di), F32)], MXU_DTYPE),
                ("plain", [_mx(w_in[:, di:di + conv_dim]), jnp.zeros((1, conv_dim), F32)], F32),
                ("precise", [w_dt_hi, w_dt_lo, jnp.zeros((1, 2 * LANES), F32)], F32)], "ssm_in")
            tc = 1024 if conv_dim % 1024 == 0 else 512
            xbc = _dwconv(xbc, ssm_w_conv[j], ssm_b_conv[j], geo, "silu", [], tc, F32, "ssm_conv")
            y2 = _ssd(xbc, dt_raw, ssm_dt_bias[j], ssm_a_log[j], geo, di, heads)
            a = _ssm_finish(y2, xbc, z, _row(jnp.repeat(ssm_d[j], HEAD)), _row(ssm_norm_g[j]))
            wo, bo = _mx(ssm_w_out[j]), zero_bias
        elif kind == 2:
            nh = swa_sinks.shape[-1]
            nq, nkv = nh * HEAD, SWA_KV_HEADS * HEAD
            w, b = swa_w_qkv[j], swa_b_qkv[j][None, :]
            wq, wk, wv = w[:, :nq], w[:, nq:nq + nkv], w[:, nq + nkv:]
            bq, bk, bv = b[:, :nq], b[:, nq:nq + nkv], b[:, nq + nkv:]
            uq, kd, vp = _norm_proj(xs, gain, modtab, geo, [
                ("tab1", [_mx(_pair_cols(wq, _rot_cols(wq))), _pair_cols(bq, _rot_cols(bq)), q_tab], MXU_DTYPE),
                ("tab2", [_mx(_pair_cols(wk, wk)), _pair_cols(bk, bk), k_cos,
                          _mx(_pair_cols(_rot_cols(wk), _rot_cols(wk))), _pair_cols(_rot_cols(bk), _rot_cols(bk)),
                          k_sin], MXU_DTYPE),
                ("plain", [_mx(_pair_cols(wv, jnp.zeros_like(wv))), _pair_cols(bv, jnp.zeros_like(bv))],
                 MXU_DTYPE)], "swa_qkv")
            a = _swa_attention(uq, kd, vp, swa_sinks[j].astype(F32), geo, nh)
            wo, bo = _mx(swa_w_o[j]), _row(swa_b_o[j])
        else:
            assert not ctx_out, "differential attention is only built for a layer without context output"
            lambda_init = 0.8 - 0.6 * math.exp(-0.3 * i)
            w = diff_w_qkv[j]
            wq, wk, wv = w[:, :d], w[:, d:2 * d], w[:, 2 * d:]
            zb = jnp.zeros((1, 2 * d), F32)
            uq, kd, v = _norm_proj(xs, gain, modtab, geo, [
                ("tab1", [_mx(_pair_cols(wq, _rot_cols(wq))), zb, q_tab], MXU_DTYPE),
                ("tab2", [_mx(_pair_cols(wk, wk)), zb, k_cos,
                          _mx(_pair_cols(_rot_cols(wk), _rot_cols(wk))), zb, k_sin], MXU_DTYPE),
                ("plain", [_mx(wv), zero_bias], MXU_DTYPE)], "diff_qkv")
            lam_params = jnp.stack([diff_lambda_q1[j], diff_lambda_k1[j], diff_lambda_q2[j],
                                    diff_lambda_k2[j]]).astype(F32)
            a = _diff_attention(uq, kd, v, lam_params, _row(diff_subln_g[j]), geo, d // (2 * HEAD), lambda_init)
            wo, bo = _mx(diff_w_o[j]), zero_bias

        wr = jnp.pad(moe_w_router[i], ((0, 0), (0, LANES - n_exp)))
        wr_hi = _mx(wr)
        wr_lo = _mx(wr - wr_hi.astype(F32))
        br = jnp.pad(_row(moe_b_router[i]), ((0, 0), (0, LANES - n_exp)), constant_values=-1e30)
        if ctx_out:
            x_tile, mod_index = (lambda t: t), geo.mod_all
        else:
            x_tile, mod_index = geo.lat_tile, geo.mod_lat
            if a.shape[0] == geo.rows:
                a = a.reshape(batch, geo.rpb, -1)[:, n_ctx:].reshape(batch * n_lat, -1)
        xs, h, top, gates = _post(xs, a, wo, bo, modtab, _row(g_ffn[i]), wr_hi, wr_lo, br, x_tile, mod_index,
                                  "post_mixer")
        final = i == depth - 1
        xs = _moe(i, xs, h, top, gates, modtab, mod_index, moe_w_gu, moe_b_gu, moe_w_down, moe_b_down,
                  _row(g_final), final)
        if not ctx_out and not final:
            raise NotImplementedError("a layer without context output must be the last layer")
        out = xs
    return out.reshape(batch, n_lat, d)
```

```python
import functools
import math

import jax
import jax.numpy as jnp
from jax import lax
from jax.experimental import pallas as pl
from jax.experimental.pallas import tpu as pltpu

F32 = jnp.float32
I32 = jnp.int32
MXU_DTYPE = jnp.bfloat16

LANES = 128
SUBLANES = 8
VMEM_LIMIT_BYTES = 56 * 1024 * 1024

TILE = 256
CHUNK = 128
HALO = 16
HEAD = 64
NORM_EPS = 1e-6
ROPE_BASE = 10000.0
GRID_W = 64
ADA_CHUNKS = 6
SSM_GROUPS = 4
SSM_STATE = 128
SWA_KV_HEADS = 4
SWA_WINDOW = 128
TOP_K = 4
SWIGLU_LIMIT = 7.0
SWIGLU_ALPHA = 1.702
EXPERT_TILE = 512
POS_ROWS = 1024
DISPATCH_ROWS = 1024
DIFF_TQ = 256
DIFF_HEADS_PER_STEP = 4


def _mx(v):
    return v.astype(MXU_DTYPE)


def _dot(a, b):
    return jnp.dot(a, b, preferred_element_type=F32)


def _dot_nt(a, b):
    return lax.dot_general(a, b, (((1,), (1,)), ((), ())), preferred_element_type=F32)


def _split2(v):
    hi = _mx(v)
    return hi, _mx(v - hi.astype(F32))


def _split3(v):
    hi = _mx(v)
    r = v - hi.astype(F32)
    mid = _mx(r)
    return hi, mid, _mx(r - mid.astype(F32))


def _dot_exact_rhs(parts, m):
    acc = _dot(parts[0], m)
    for p in parts[1:]:
        acc = acc + _dot(p, m)
    return acc


def _sigmoid(v):
    return 1.0 / (1.0 + jnp.exp(-v))


def _softplus(v):
    return jnp.maximum(v, 0.0) + jnp.log(1.0 + jnp.exp(-jnp.abs(v)))


def _rms(v):
    return v * lax.rsqrt(jnp.mean(v * v, axis=-1, keepdims=True) + NORM_EPS)


def _store_row_tiles(ref, value):
    nc = value.shape[1] // LANES
    for c in range(nc):
        ref[pl.ds(c, value.shape[0], stride=nc), :] = value[:, c * LANES:(c + 1) * LANES]


def _load_row_tiles(ref, rows, nc):
    return jnp.concatenate([ref[pl.ds(c, rows, stride=nc), :] for c in range(nc)], axis=1)


def _params(*sem):
    return pltpu.CompilerParams(dimension_semantics=sem, vmem_limit_bytes=VMEM_LIMIT_BYTES)


class _Geo:
    def __init__(self, batch, n_ctx, n_lat):
        assert n_ctx % TILE == 0 and n_lat % TILE == 0 and n_lat % n_ctx == 0
        self.batch, self.n_ctx, self.n_lat = batch, n_ctx, n_lat
        self.rpb = n_ctx + n_lat
        self.tpb = self.rpb // TILE
        self.nct = n_ctx // TILE
        self.nlt = n_lat // TILE
        self.cpb = self.rpb // CHUNK
        self.ncc = n_ctx // CHUNK
        self.nlc = n_lat // CHUNK
        self.rows = batch * self.rpb

    def mod_all(self, i):
        return (i // self.tpb) * 2 + (i % self.tpb >= self.nct).astype(I32)

    def lat_tile(self, i):
        return (i // self.nlt) * self.tpb + self.nct + i % self.nlt

    def mod_lat(self, i):
        return (i // self.nlt) * 2 + 1


def _adaln_kernel(c_ref, w_ref, b_ref, o_ref):
    c = c_ref[...]
    s_hi, s_lo = _split2(c * _sigmoid(c))
    w_hi, w_lo = _split2(w_ref[...])
    o_ref[...] = _dot(s_hi, w_hi) + _dot(s_lo, w_hi) + _dot(s_hi, w_lo) + b_ref[...]


def _adaln(cond, ada_w, ada_b):
    depth, d, n = ada_w.shape
    rows = cond.shape[0]
    return pl.pallas_call(
        _adaln_kernel,
        grid=(depth, n // d),
        in_specs=[pl.BlockSpec((rows, d), lambda l, j: (0, 0)),
                  pl.BlockSpec((None, d, d), lambda l, j: (l, 0, j)),
                  pl.BlockSpec((None, 1, d), lambda l, j: (l, 0, j))],
        out_specs=pl.BlockSpec((None, rows, d), lambda l, j: (l, 0, j)),
        out_shape=jax.ShapeDtypeStruct((depth, rows, n), F32),
        compiler_params=_params("arbitrary", "arbitrary"),
        name="adaln",
    )(cond, ada_w, ada_b.reshape(depth, 1, n))


def _norm_mod(x, g, mod, shift_row, scale_row):
    return _rms(x) * g * (1.0 + mod[scale_row:scale_row + 1, :]) + mod[shift_row:shift_row + 1, :]


def _lane_tile(tab, n):
    return jnp.tile(tab, (1, n // LANES))


_PROJ_OPERANDS = {"plain": "wb", "precise": "wwb", "glu": "wbwb", "tab1": "wbt", "tab2": "wbtwbt"}


def _norm_proj_kernel(modes, x_ref, g_ref, mod_ref, *refs):
    out_refs = refs[len(refs) - len(modes):]
    h = _norm_mod(x_ref[...], g_ref[...], mod_ref[...], 0, 1)
    hb = _mx(h)
    at = 0
    for mode, o_ref in zip(modes, out_refs):
        ops = refs[at:at + len(_PROJ_OPERANDS[mode])]
        at += len(ops)
        n = o_ref.shape[-1]
        if mode == "precise":
            w_hi, w_lo, b = ops
            h_lo = _mx(h - hb.astype(F32))
            acc = _dot(hb, w_hi[...]) + _dot(h_lo, w_hi[...]) + _dot(hb, w_lo[...]) + b[...]
        elif mode == "plain":
            w, b = ops
            acc = _dot(hb, w[...]) + b[...]
        elif mode == "glu":
            w1, b1, w2, b2 = ops
            acc = (_dot(hb, w1[...]) + b1[...]) * _sigmoid(_dot(hb, w2[...]) + b2[...])
        elif mode == "tab1":
            w1, b1, t1 = ops
            acc = (_dot(hb, w1[...]) + b1[...]) * _lane_tile(t1[...], n)
        else:
            w1, b1, t1, w2, b2, t2 = ops
            acc = ((_dot(hb, w1[...]) + b1[...]) * _lane_tile(t1[...], n)
                   + (_dot(hb, w2[...]) + b2[...]) * _lane_tile(t2[...], n))
        o_ref[...] = acc.astype(o_ref.dtype)


def _norm_proj(x, gain, modtab, geo, groups, name):
    rows, d = x.shape
    specs = [pl.BlockSpec((TILE, d), lambda i: (i, 0)),
             pl.BlockSpec((1, d), lambda i: (0, 0)),
             pl.BlockSpec((None, ADA_CHUNKS, d), lambda i: (geo.mod_all(i), 0, 0))]
    operands, out_specs, out_shapes = [], [], []
    for mode, ops, out_dtype in groups:
        n = ops[0].shape[1]
        for kind, op in zip(_PROJ_OPERANDS[mode], ops):
            if kind == "t":
                specs.append(pl.BlockSpec((TILE, LANES), lambda i: (i % geo.tpb, 0)))
            else:
                specs.append(pl.BlockSpec(op.shape, lambda i: (0, 0)))
        operands += ops
        out_specs.append(pl.BlockSpec((TILE, n), lambda i: (i, 0)))
        out_shapes.append(jax.ShapeDtypeStruct((rows, n), out_dtype))
    return pl.pallas_call(
        functools.partial(_norm_proj_kernel, tuple(g[0] for g in groups)),
        grid=(rows // TILE,),
        in_specs=specs,
        out_specs=out_specs,
        out_shape=out_shapes,
        compiler_params=_params("arbitrary"),
        name=name,
    )(x, gain, modtab, *operands)


def _dwconv_kernel(width, mode, tpb, nct, cur_ref, prev_ref, next_ref, w_ref, b_ref, *refs):
    shifted = width > SUBLANES
    pad_ref = refs[-2] if shifted else refs[-1]
    o_ref = refs[-3] if shifted else refs[-2]
    p = pl.program_id(0) % tpb
    has_prev = jnp.logical_and(p != 0, p != nct)
    has_next = jnp.logical_and(p != nct - 1, p != tpb - 1)
    pad_ref[0:HALO, :] = jnp.where(has_prev, prev_ref[...], 0.0)
    pad_ref[HALO:HALO + TILE, :] = cur_ref[...]
    pad_ref[HALO + TILE:, :] = jnp.where(has_next, next_ref[...], 0.0)
    half = (width - 1) // 2
    acc = jnp.broadcast_to(b_ref[...], o_ref.shape)
    if shifted:
        sh_ref = refs[-1]
        span = sh_ref.shape[1]
        for s in range(1, SUBLANES):
            sh_ref[s - 1] = pad_ref[s:s + span, :]
    for k in range(width):
        off = HALO - half + k
        if shifted and off % SUBLANES:
            base = off - off % SUBLANES
            tap = sh_ref[off % SUBLANES - 1, base:base + TILE, :]
        else:
            tap = pad_ref[off:off + TILE, :]
        acc = acc + w_ref[k:k + 1, :] * tap
    if mode == "ln_silu":
        g_ref, beta_ref = refs[:2]
        cen = acc - jnp.mean(acc, axis=-1, keepdims=True)
        acc = cen * lax.rsqrt(jnp.mean(cen * cen, axis=-1, keepdims=True) + NORM_EPS) * g_ref[...] + beta_ref[...]
    o_ref[...] = (acc * _sigmoid(acc)).astype(o_ref.dtype)


def _dwconv(u, w, b, geo, mode, extra, tc, out_dtype, name):
    rows, c = u.shape
    width = w.shape[0]
    per = TILE // HALO
    last = rows // HALO - 1
    specs = [pl.BlockSpec((TILE, tc), lambda i, j: (i, j)),
             pl.BlockSpec((HALO, tc), lambda i, j: (jnp.maximum(i * per - 1, 0), j)),
             pl.BlockSpec((HALO, tc), lambda i, j: (jnp.minimum((i + 1) * per, last), j)),
             pl.BlockSpec((width, tc), lambda i, j: (0, j)),
             pl.BlockSpec((1, tc), lambda i, j: (0, j))]
    specs += [pl.BlockSpec((1, tc), lambda i, j: (0, j)) for _ in extra]
    scratch = [pltpu.VMEM((TILE + 2 * HALO, tc), F32)]
    if width > SUBLANES:
        scratch.append(pltpu.VMEM((SUBLANES - 1, TILE + 2 * HALO - SUBLANES, tc), F32))
    return pl.pallas_call(
        functools.partial(_dwconv_kernel, width, mode, geo.tpb, geo.nct),
        grid=(rows // TILE, c // tc),
        in_specs=specs,
        out_specs=pl.BlockSpec((TILE, tc), lambda i, j: (i, j)),
        out_shape=jax.ShapeDtypeStruct((rows, c), out_dtype),
        scratch_shapes=scratch,
        compiler_params=_params("arbitrary", "arbitrary"),
        name=name,
    )(u, u, u, w, b.reshape(1, c), *extra)


def _ssd_kernel(groups, heads_per_group, x_ref, b_ref, c_ref, dt_ref, bias_ref, alog_ref, e_ref, y_ref, h_ref):
    direction = pl.program_id(0)
    step = pl.program_id(2)
    n_state = SSM_STATE
    rp = heads_per_group * HEAD

    @pl.when(step == 0)
    def _():
        h_ref[...] = jnp.zeros_like(h_ref)

    fwd = direction == 0
    row = lax.broadcasted_iota(I32, (CHUNK, CHUNK), 0)
    col = lax.broadcasted_iota(I32, (CHUNK, CHUNK), 1)
    tri = (row - col) * jnp.where(fwd, 1, -1) >= 0
    dtv = _softplus(dt_ref[...] + bias_ref[...])
    a = dtv * (-jnp.exp(alog_ref[...]))
    tri_m = _mx(jnp.where(tri, 1.0, 0.0))
    a3 = _split3(a)
    acum = _dot(tri_m, a3[0]) + _dot(tri_m, a3[1]) + _dot(tri_m, a3[2])
    acum_t = acum.T
    expand = e_ref[...]
    dt_x = _dot_exact_rhs(_split2(dtv), expand)
    ac_x = _dot_exact_rhs(_split3(acum), expand)
    tot_x = jnp.where(fwd, ac_x[CHUNK - 1:CHUNK, :], ac_x[0:1, :])
    xdt = x_ref[...] * dt_x
    xdt_b = _mx(xdt)
    xdt_end_b = _mx(xdt * jnp.exp(tot_x - ac_x))
    e_ac = jnp.exp(ac_x)
    decay = jnp.exp(tot_x)
    lo_half = lax.broadcasted_iota(I32, (CHUNK, LANES), 1) < HEAD
    for g in range(groups):
        bg = b_ref[:, g * n_state:(g + 1) * n_state]
        cb_g = _mx(c_ref[:, g * n_state:(g + 1) * n_state])
        cb = _dot_nt(cb_g, _mx(bg))
        bg_t = _mx(bg.T)
        h_t = h_ref[g]
        cols = slice(g * rp, (g + 1) * rp)
        y_off = _dot(cb_g, _mx(h_t)) * e_ac[:, cols]
        blocks = []
        for pair in range(heads_per_group // 2):
            xb = xdt_b[:, g * rp + pair * LANES:g * rp + (pair + 1) * LANES]
            halves = []
            for hh in range(2):
                c = g * heads_per_group + 2 * pair + hh
                seg = acum[:, c:c + 1] - acum_t[c:c + 1, :]
                within = jnp.exp(jnp.where(tri, seg, -jnp.inf))
                halves.append(_dot(_mx(cb * within), xb))
            blocks.append(jnp.where(lo_half, halves[0], halves[1]))
        y_ref[:, cols] = (jnp.concatenate(blocks, axis=1) + y_off).astype(y_ref.dtype)
        h_ref[g] = h_t * decay[:, cols] + _dot(bg_t, xdt_end_b[:, cols])


def _ssd(xbc, dt_raw, dt_bias, a_log, geo, d_inner, heads):
    rows = xbc.shape[0]
    groups = SSM_GROUPS
    hpg = heads // groups
    gn = groups * SSM_STATE
    assert d_inner % gn == 0 and hpg % 2 == 0 and heads <= LANES
    cpb, ncc = geo.cpb, geo.ncc

    def rb(d, b, s):
        back = jnp.where(s < ncc, ncc - 1 - s, cpb + ncc - 1 - s)
        return b * cpb + jnp.where(d == 0, s, back)

    pad = LANES - heads
    bias = jnp.pad(dt_bias.astype(F32), ((0, 0), (0, pad))).reshape(2, 1, LANES)
    alog = jnp.pad(a_log.astype(F32), ((0, 0), (0, pad))).reshape(2, 1, LANES)
    expand = (jnp.arange(LANES)[:, None] == jnp.arange(d_inner)[None, :] // HEAD).astype(MXU_DTYPE)
    return pl.pallas_call(
        functools.partial(_ssd_kernel, groups, hpg),
        grid=(2, geo.batch, cpb),
        in_specs=[pl.BlockSpec((CHUNK, d_inner), lambda d, b, s: (rb(d, b, s), 0)),
                  pl.BlockSpec((CHUNK, gn), lambda d, b, s: (rb(d, b, s), d_inner // gn)),
                  pl.BlockSpec((CHUNK, gn), lambda d, b, s: (rb(d, b, s), d_inner // gn + 1)),
                  pl.BlockSpec((CHUNK, LANES), lambda d, b, s: (rb(d, b, s), d)),
                  pl.BlockSpec((None, 1, LANES), lambda d, b, s: (d, 0, 0)),
                  pl.BlockSpec((None, 1, LANES), lambda d, b, s: (d, 0, 0)),
                  pl.BlockSpec((LANES, d_inner), lambda d, b, s: (0, 0))],
        out_specs=pl.BlockSpec((None, CHUNK, d_inner), lambda d, b, s: (d, rb(d, b, s), 0)),
        out_shape=jax.ShapeDtypeStruct((2, rows, d_inner), MXU_DTYPE),
        scratch_shapes=[pltpu.VMEM((groups, SSM_STATE, hpg * HEAD), F32)],
        compiler_params=_params("arbitrary", "arbitrary", "arbitrary"),
        name="ssd_scan",
    )(xbc, xbc, xbc, dt_raw, bias, alog, expand)


def _ssm_finish_kernel(y_ref, xs_ref, z_ref, dskip_ref, g_ref, o_ref):
    y = y_ref[0].astype(F32) + y_ref[1].astype(F32) + xs_ref[...] * dskip_ref[...]
    z = z_ref[...].astype(F32)
    o_ref[...] = (_rms(y * (z * _sigmoid(z))) * g_ref[...]).astype(o_ref.dtype)


def _ssm_finish(y2, xbc, z, d_skip_cols, norm_g):
    rows, di = z.shape
    return pl.pallas_call(
        _ssm_finish_kernel,
        grid=(rows // TILE,),
        in_specs=[pl.BlockSpec((2, TILE, di), lambda i: (0, i, 0)),
                  pl.BlockSpec((TILE, di), lambda i: (i, 0)),
                  pl.BlockSpec((TILE, di), lambda i: (i, 0)),
                  pl.BlockSpec((1, di), lambda i: (0, 0)),
                  pl.BlockSpec((1, di), lambda i: (0, 0))],
        out_specs=pl.BlockSpec((TILE, di), lambda i: (i, 0)),
        out_shape=jax.ShapeDtypeStruct((rows, di), MXU_DTYPE),
        compiler_params=_params("arbitrary"),
        name="ssm_finish",
    )(y2, xbc, z, d_skip_cols, norm_g)


def _swa_kernel(n_heads, ncc, n_lat, n_ctx, sink_ref, q_ref, kp_ref, kc_ref, kn_ref, kx_ref,
                vp_ref, vc_ref, vn_ref, vx_ref, o_ref):
    j = pl.program_id(1)
    is_ctx = j < ncc
    start = (j - ncc) * CHUNK
    span = 3 * CHUNK
    qi = lax.broadcasted_iota(I32, (CHUNK, span), 0)
    rel = lax.broadcasted_iota(I32, (CHUNK, span), 1) - SWA_WINDOW
    kpos = start + rel
    ninf = -jnp.inf
    band = jnp.where(jnp.abs(qi - rel) <= SWA_WINDOW,
                     jnp.where(kpos >= 0, jnp.where(kpos < n_lat, 0.0, ninf), ninf), ninf)
    band = jnp.where(is_ctx, ninf, band)
    group = n_heads // SWA_KV_HEADS
    bias = jnp.concatenate([band, jnp.zeros((CHUNK, n_ctx), F32)], axis=1)
    bias = jnp.concatenate([bias] * group, axis=0)
    head_of_row = lax.broadcasted_iota(I32, (group * CHUNK, 1), 0) // CHUNK
    outs = []
    for kh in range(SWA_KV_HEADS):
        lanes = slice(kh * LANES, (kh + 1) * LANES)
        k_all = jnp.concatenate([kp_ref[:, lanes], kc_ref[:, lanes], kn_ref[:, lanes], kx_ref[:, lanes]], axis=0)
        v_all = jnp.concatenate([vp_ref[:, lanes], vc_ref[:, lanes], vn_ref[:, lanes], vx_ref[:, lanes]], axis=0)
        heads = range(kh * group, (kh + 1) * group)
        q = jnp.concatenate([q_ref[:, h * LANES:(h + 1) * LANES] for h in heads], axis=0)
        sink = jnp.zeros((group * CHUNK, 1), F32)
        for g, h in enumerate(heads):
            sink = jnp.where(head_of_row == g, sink_ref[h], sink)
        s = _dot_nt(q, k_all) + bias
        m = jnp.maximum(jnp.max(s, axis=-1, keepdims=True), sink)
        e = jnp.exp(s - m)
        o = _dot(_mx(e), v_all) / (jnp.sum(e, axis=-1, keepdims=True) + jnp.exp(sink - m))
        outs += [o[g * CHUNK:(g + 1) * CHUNK] for g in range(group)]
    for c in range(n_heads // 2):
        o_ref[:, c * LANES:(c + 1) * LANES] = (outs[2 * c] + pltpu.roll(outs[2 * c + 1], HEAD, 1)).astype(o_ref.dtype)


def _swa_attention(uq, kd, vp, sinks, geo, n_heads):
    rows = uq.shape[0]
    cpb, ncc, nlc = geo.cpb, geo.ncc, geo.nlc
    kvw = SWA_KV_HEADS * LANES
    ctx_per = geo.rpb // geo.n_ctx

    def lat_block(shift):
        def index(b, j):
            jl = jnp.clip(j - ncc + shift, 0, nlc - 1)
            return (b * cpb + ncc + jl, 0)
        return pl.BlockSpec((CHUNK, kvw), index)

    ctx_spec = pl.BlockSpec((geo.n_ctx, kvw), lambda b, j: (b * ctx_per, 0))
    kv_specs = [lat_block(-1), lat_block(0), lat_block(1), ctx_spec]
    return pl.pallas_call(
        functools.partial(_swa_kernel, n_heads, ncc, geo.n_lat, geo.n_ctx),
        grid=(geo.batch, cpb),
        in_specs=[pl.BlockSpec(memory_space=pltpu.SMEM),
                  pl.BlockSpec((CHUNK, n_heads * LANES), lambda b, j: (b * cpb + j, 0))] + kv_specs + kv_specs,
        out_specs=pl.BlockSpec((CHUNK, n_heads * HEAD), lambda b, j: (b * cpb + j, 0)),
        out_shape=jax.ShapeDtypeStruct((rows, n_heads * HEAD), MXU_DTYPE),
        compiler_params=_params("arbitrary", "arbitrary"),
        name="swa_attention",
    )(sinks, uq, kd, kd, kd, kd, vp, vp, vp, vp)


def _diff_kernel(lambda_init, q_ref, k_ref, v_ref, lam_ref, g_ref, o_ref):
    lp = lam_ref[...]
    lam = (jnp.exp(jnp.sum(lp[0:1] * lp[1:2], axis=-1, keepdims=True))
           - jnp.exp(jnp.sum(lp[2:3] * lp[3:4], axis=-1, keepdims=True)) + lambda_init)
    for hh in range(DIFF_HEADS_PER_STEP):
        v = v_ref[:, hh * LANES:(hh + 1) * LANES]
        parts = []
        for t in range(2):
            lanes = slice((2 * hh + t) * LANES, (2 * hh + t + 1) * LANES)
            s = _dot_nt(q_ref[:, lanes], k_ref[:, lanes])
            e = jnp.exp(s - jnp.max(s, axis=-1, keepdims=True))
            parts.append(_dot(_mx(e), v) / jnp.sum(e, axis=-1, keepdims=True))
        o = parts[0] - lam * parts[1]
        o_ref[:, hh * LANES:(hh + 1) * LANES] = (_rms(o) * g_ref[...] * (1.0 - lambda_init)).astype(o_ref.dtype)


def _diff_attention(uq, kd, v, lam_params, subln_g, geo, n_heads, lambda_init):
    tq = DIFF_TQ
    hp = DIFF_HEADS_PER_STEP
    nq = geo.n_lat // tq
    upb = geo.rpb // tq
    ucx = geo.n_ctx // tq
    assert n_heads % hp == 0 and 2 * HEAD == LANES
    return pl.pallas_call(
        functools.partial(_diff_kernel, lambda_init),
        grid=(geo.batch, n_heads // hp, nq),
        in_specs=[pl.BlockSpec((tq, 2 * hp * LANES), lambda b, h, j: (b * upb + ucx + j, h)),
                  pl.BlockSpec((geo.rpb, 2 * hp * LANES), lambda b, h, j: (b, h)),
                  pl.BlockSpec((geo.rpb, hp * LANES), lambda b, h, j: (b, h)),
                  pl.BlockSpec(lam_params.shape, lambda b, h, j: (0, 0)),
                  pl.BlockSpec((1, 2 * HEAD), lambda b, h, j: (0, 0))],
        out_specs=pl.BlockSpec((tq, hp * 2 * HEAD), lambda b, h, j: (b * nq + j, h)),
        out_shape=jax.ShapeDtypeStruct((geo.batch * geo.n_lat, n_heads * 2 * HEAD), MXU_DTYPE),
        compiler_params=_params("arbitrary", "arbitrary", "arbitrary"),
        name="diff_attention",
    )(uq, kd, v, lam_params, subln_g)


def _post_kernel(x_ref, a_ref, wo_ref, bo_ref, mod_ref, g_ref, wr_hi_ref, wr_lo_ref, br_ref,
                 xo_ref, h_ref, top_ref, gate_ref):
    mod = mod_ref[...]
    x = x_ref[...] + mod[2:3, :] * (_dot(a_ref[...], wo_ref[...]) + bo_ref[...])
    xo_ref[...] = x
    h = _norm_mod(x, g_ref[...], mod, 3, 4)
    _store_row_tiles(h_ref, h)
    h_hi, h_lo = _split2(h)
    logits = _dot(h_hi, wr_hi_ref[...]) + _dot(h_lo, wr_hi_ref[...]) + _dot(h_hi, wr_lo_ref[...]) + br_ref[...]
    lane = lax.broadcasted_iota(I32, logits.shape, 1)
    lane_f = lane.astype(F32)
    top = jnp.zeros(logits.shape, I32)
    gate = jnp.zeros(logits.shape, F32)
    m0 = None
    for k in range(TOP_K):
        m = jnp.max(logits, axis=-1, keepdims=True)
        idx = jnp.min(jnp.where(logits == m, lane_f, float(LANES)), axis=-1, keepdims=True).astype(I32)
        logits = jnp.where(lane == idx, -jnp.inf, logits)
        m0 = m if k == 0 else m0
        top = jnp.where(lane == k, idx, top)
        gate = jnp.where(lane == k, jnp.exp(m - m0), gate)
    top_ref[...] = top
    gate_ref[...] = gate / jnp.sum(gate, axis=-1, keepdims=True)


def _post(x, a, wo, bo, modtab, gain, wr_hi, wr_lo, br, x_tile, mod_index, name):
    rows, din = a.shape
    d = x.shape[1]
    row_spec = pl.BlockSpec((TILE, d), lambda i: (i, 0))
    meta_spec = pl.BlockSpec((TILE, LANES), lambda i: (i, 0))
    const = lambda shape: pl.BlockSpec(shape, lambda i: (0, 0))
    return pl.pallas_call(
        _post_kernel,
        grid=(rows // TILE,),
        in_specs=[pl.BlockSpec((TILE, d), lambda i: (x_tile(i), 0)),
                  pl.BlockSpec((TILE, din), lambda i: (i, 0)),
                  const((din, d)), const((1, d)),
                  pl.BlockSpec((None, ADA_CHUNKS, d), lambda i: (mod_index(i), 0, 0)),
                  const((1, d)), const((d, LANES)), const((d, LANES)), const((1, LANES))],
        out_specs=[row_spec, pl.BlockSpec((TILE * (d // LANES), LANES), lambda i: (i, 0)), meta_spec, meta_spec],
        out_shape=[jax.ShapeDtypeStruct((rows, d), F32), jax.ShapeDtypeStruct((rows * (d // LANES), LANES), F32),
                   jax.ShapeDtypeStruct((rows, LANES), I32), jax.ShapeDtypeStruct((rows, LANES), F32)],
        compiler_params=_params("arbitrary"),
        name=name,
    )(x, a, wo, bo, modtab, gain, wr_hi, wr_lo, br)


def _moe_pos_kernel(top_ref, pos_ref, meta_ref, count_ref, start_ref):
    phase = pl.program_id(0)
    i = pl.program_id(1)

    @pl.when(jnp.logical_and(phase == 0, i == 0))
    def _():
        count_ref[...] = jnp.zeros_like(count_ref)

    @pl.when(phase == 0)
    def _():
        top = top_ref[...]
        lane = lax.broadcasted_iota(I32, top.shape, 1)
        tile_count = sum(jnp.sum(jnp.where(lane == top[:, k:k + 1], 1.0, 0.0), axis=0, keepdims=True)
                         for k in range(TOP_K))
        count_ref[...] = count_ref[...] + tile_count

    @pl.when(jnp.logical_and(phase == 1, i == 0))
    def _():
        counts = count_ref[...]
        padded = jnp.ceil(counts / EXPERT_TILE) * EXPERT_TILE
        lane8 = lax.broadcasted_iota(I32, counts.shape, 1)
        incl = padded
        shift = 1
        while shift < LANES:
            incl = incl + jnp.where(lane8 >= shift, pltpu.roll(incl, shift, 1), 0.0)
            shift *= 2
        start_ref[...] = incl - padded
        sub = lax.broadcasted_iota(I32, counts.shape, 0)
        meta_ref[...] = jnp.where(sub == 0, counts, jnp.where(sub == 1, incl - padded, 0.0))
        count_ref[...] = jnp.zeros_like(count_ref)

    @pl.when(phase == 1)
    def _():
        r = lax.broadcasted_iota(I32, (TILE, TILE), 0)
        c = lax.broadcasted_iota(I32, (TILE, TILE), 1)
        strict_lower = _mx(jnp.where(r > c, 1.0, 0.0))
        base = count_ref[0:1, :] + start_ref[0:1, :]
        lane_t = lax.broadcasted_iota(I32, (TILE, LANES), 1)
        all_ones = jnp.ones((LANES, LANES), MXU_DTYPE)
        for sub in range(top_ref.shape[0] // TILE):
            rows = slice(sub * TILE, (sub + 1) * TILE)
            top_t = top_ref[rows, :]
            pos = jnp.zeros((TILE, LANES), I32)
            for k in range(TOP_K):
                onehot = jnp.where(lane_t == top_t[:, k:k + 1], 1.0, 0.0)
                before = _dot(strict_lower, _mx(onehot)) + base
                slot = _dot_exact_rhs(_split3(onehot * before), all_ones)
                pos = jnp.where(lane_t == k, slot.astype(I32), pos)
                base = base + jnp.sum(onehot, axis=0, keepdims=True)
            pos_ref[rows, :] = pos
        count_ref[...] = jnp.broadcast_to(base - start_ref[0:1, :], count_ref.shape)


def _moe_pos(top):
    rows = top.shape[0]
    step_rows = POS_ROWS if rows % POS_ROWS == 0 else TILE
    return pl.pallas_call(
        _moe_pos_kernel,
        grid=(2, rows // step_rows),
        in_specs=[pl.BlockSpec((step_rows, LANES), lambda p, i: (i, 0))],
        out_specs=[pl.BlockSpec((step_rows, LANES), lambda p, i: (i * p, 0)),
                   pl.BlockSpec((8, LANES), lambda p, i: (0, 0))],
        out_shape=[jax.ShapeDtypeStruct((rows, LANES), I32), jax.ShapeDtypeStruct((8, LANES), F32)],
        scratch_shapes=[pltpu.VMEM((8, LANES), F32), pltpu.VMEM((8, LANES), F32)],
        compiler_params=_params("arbitrary", "arbitrary"),
        name="moe_pos",
    )(top)


def _token_copy(nc, src_ref, src_tok, dst_ref, dst_tok, sem):
    def start_row(tok):
        return tok * nc if isinstance(tok, int) else pl.multiple_of(tok * nc, nc)
    return pltpu.make_async_copy(src_ref.at[pl.ds(start_row(src_tok), nc)],
                                 dst_ref.at[pl.ds(start_row(dst_tok), nc)], sem)


def _dispatch_kernel(nc, pos_ref, fill_ref, h_ref, xs_ref, zero_ref, sem):
    rows = h_ref.shape[0] // nc
    base = pl.program_id(0) * (rows * TOP_K)

    @pl.when(pl.program_id(0) == 0)
    def _():
        zero_ref[...] = jnp.zeros_like(zero_ref)

        def fill_copy(j):
            start = pl.multiple_of(jnp.maximum(fill_ref[j], 0) * nc, EXPERT_TILE * nc)
            return pltpu.make_async_copy(zero_ref, xs_ref.at[pl.ds(start, EXPERT_TILE * nc)], sem)

        for j in range(fill_ref.shape[0]):
            pl.when(fill_ref[j] >= 0)(lambda j=j: fill_copy(j).start())
        for j in range(fill_ref.shape[0]):
            pl.when(fill_ref[j] >= 0)(lambda j=j: fill_copy(j).wait())

    for t in range(rows):
        for k in range(TOP_K):
            _token_copy(nc, h_ref, t, xs_ref, pos_ref[base + t * TOP_K + k], sem).start(priority=k % 2)
    for k in range(TOP_K):
        pltpu.make_async_copy(h_ref, xs_ref.at[pl.ds(0, rows * nc)], sem).wait()


def _dispatch(pos_flat, fill_rows, h_tiles, n_slots):
    nc = h_tiles.shape[0] * TOP_K // pos_flat.shape[0]
    rows = h_tiles.shape[0] // nc
    step_rows = DISPATCH_ROWS if rows % DISPATCH_ROWS == 0 else TILE
    return pl.pallas_call(
        functools.partial(_dispatch_kernel, nc),
        grid_spec=pltpu.PrefetchScalarGridSpec(
            num_scalar_prefetch=2,
            grid=(rows // step_rows,),
            in_specs=[pl.BlockSpec((step_rows * nc, LANES), lambda i, pos, fill: (i, 0))],
            out_specs=pl.BlockSpec(memory_space=pl.ANY),
            scratch_shapes=[pltpu.VMEM((EXPERT_TILE * nc, LANES), F32), pltpu.SemaphoreType.DMA(())]),
        out_shape=jax.ShapeDtypeStruct((n_slots * nc, LANES), F32),
        compiler_params=_params("arbitrary"),
        name="moe_dispatch",
    )(pos_flat, fill_rows, h_tiles)


def _expert_kernel(layer, d_expert, te_ref, nx_ref, slot_ref, na_ref, x_ref, wgu_hbm, bgu_ref, wd_hbm, bd_ref,
                   o_ref, wgu_f32, wd_f32, wgu_mx, wd_mx, sem):
    i = pl.program_id(0)
    active = i < na_ref[0]
    expert = te_ref[i]
    slot = slot_ref[i]
    new_expert = jnp.logical_or(i == 0, expert != te_ref[jnp.maximum(i - 1, 0)])

    def fetch(e, s):
        return (pltpu.make_async_copy(wgu_hbm.at[layer, e], wgu_f32.at[s], sem.at[0, s]),
                pltpu.make_async_copy(wd_hbm.at[layer, e], wd_f32.at[s], sem.at[1, s]))

    @pl.when(i == 0)
    def _():
        for copy in fetch(expert, slot):
            copy.start()

    @pl.when(jnp.logical_and(active, new_expert))
    def _():
        for copy in fetch(expert, slot):
            copy.wait()

        @pl.when(nx_ref[i] >= 0)
        def _():
            for copy in fetch(nx_ref[i], 1 - slot):
                copy.start()

        wgu_mx[...] = _mx(wgu_f32[slot])
        wd_mx[...] = _mx(wd_f32[slot])

    @pl.when(active)
    def _():
        x = _load_row_tiles(x_ref, EXPERT_TILE, wgu_mx.shape[0] // LANES)
        gu = _dot(_mx(x), wgu_mx[...]) + bgu_ref[...]
        glu = jnp.minimum(gu[:, :d_expert], SWIGLU_LIMIT)
        lin = jnp.clip(gu[:, d_expert:], -SWIGLU_LIMIT, SWIGLU_LIMIT)
        act = glu * _sigmoid(SWIGLU_ALPHA * glu) * (lin + 1.0)
        _store_row_tiles(o_ref, _dot(_mx(act), wd_mx[...]) + bd_ref[...])

    @pl.when(jnp.logical_not(active))
    def _():
        o_ref[...] = jnp.zeros_like(o_ref)


def _experts(layer, tile_expert, next_expert, weight_slot, n_active, xs, w_gu, b_gu, w_down, b_down):
    depth, n_exp, d, two_de = w_gu.shape
    nc = d // LANES
    n_slots = xs.shape[0] // nc
    de = two_de // 2
    n_tiles = n_slots // EXPERT_TILE
    row = lambda i, te, nx, sl, na: (jnp.minimum(i, na[0] - 1), 0)
    by_expert = lambda i, te, nx, sl, na: (layer, te[i], 0, 0)
    return pl.pallas_call(
        functools.partial(_expert_kernel, layer, de),
        grid_spec=pltpu.PrefetchScalarGridSpec(
            num_scalar_prefetch=4,
            grid=(n_tiles,),
            in_specs=[pl.BlockSpec((EXPERT_TILE * nc, LANES), row),
                      pl.BlockSpec(memory_space=pl.ANY),
                      pl.BlockSpec((None, None, 1, two_de), by_expert),
                      pl.BlockSpec(memory_space=pl.ANY),
                      pl.BlockSpec((None, None, 1, d), by_expert)],
            out_specs=pl.BlockSpec((EXPERT_TILE * nc, LANES), lambda i, te, nx, sl, na: (i, 0)),
            scratch_shapes=[pltpu.VMEM((2, d, two_de), F32), pltpu.VMEM((2, de, d), F32),
                            pltpu.VMEM((d, two_de), MXU_DTYPE), pltpu.VMEM((de, d), MXU_DTYPE),
                            pltpu.SemaphoreType.DMA((2, 2))]),
        out_shape=jax.ShapeDtypeStruct((n_slots * nc, LANES), F32),
        compiler_params=_params("arbitrary"),
        name="moe_experts",
    )(tile_expert, next_expert, weight_slot, n_active, xs, w_gu, b_gu.reshape(depth, n_exp, 1, two_de), w_down,
      b_down.reshape(depth, n_exp, 1, d))


def _combine_kernel(final, n_tiles, pos_ref, x_ref, gate_ref, mod_ref, gfin_ref, ys_ref, o_ref, buf, sem):
    i = pl.program_id(0)
    nc = x_ref.shape[1] // LANES

    def issue(tile, slot):
        base = tile * (TILE * TOP_K)
        for t in range(TILE):
            for k in range(TOP_K):
                _token_copy(nc, ys_ref, pos_ref[base + t * TOP_K + k], buf.at[slot, k], t,
                            sem.at[slot]).start(priority=k % 2)

    pl.when(i == 0)(lambda: issue(0, 0))
    for slot in range(2):
        pl.when(jnp.logical_and(i + 1 < n_tiles, (i + 1) % 2 == slot))(lambda slot=slot: issue(i + 1, slot))

    slot = i % 2
    for k in range(TOP_K):
        pltpu.make_async_copy(ys_ref.at[pl.ds(0, TILE * nc)], buf.at[slot, k], sem.at[slot]).wait()
    gates = gate_ref[...]
    f = gates[:, 0:1] * _load_row_tiles(buf.at[slot, 0], TILE, nc)
    for k in range(1, TOP_K):
        f = f + gates[:, k:k + 1] * _load_row_tiles(buf.at[slot, k], TILE, nc)
    x = x_ref[...] + mod_ref[5:6, :] * f
    if final:
        x = _rms(x) * gfin_ref[...]
    o_ref[...] = x


def _combine(pos_flat, x, gates, modtab, g_final, ys, mod_index, final):
    rows, d = x.shape
    return pl.pallas_call(
        functools.partial(_combine_kernel, final, rows // TILE),
        grid_spec=pltpu.PrefetchScalarGridSpec(
            num_scalar_prefetch=1,
            grid=(rows // TILE,),
            in_specs=[pl.BlockSpec((TILE, d), lambda i, pos: (i, 0)),
                      pl.BlockSpec((TILE, LANES), lambda i, pos: (i, 0)),
                      pl.BlockSpec((None, ADA_CHUNKS, d), lambda i, pos: (mod_index(i), 0, 0)),
                      pl.BlockSpec((1, d), lambda i, pos: (0, 0)),
                      pl.BlockSpec(memory_space=pl.ANY)],
            out_specs=pl.BlockSpec((TILE, d), lambda i, pos: (i, 0)),
            scratch_shapes=[pltpu.VMEM((2, TOP_K, TILE * (d // LANES), LANES), F32),
                            pltpu.SemaphoreType.DMA((2,))]),
        out_shape=jax.ShapeDtypeStruct((rows, d), F32),
        compiler_params=_params("arbitrary"),
        name="moe_combine",
    )(pos_flat, x, gates, modtab, g_final, ys)


def _moe(layer, x, h, top, gates, modtab, mod_index, w_gu, b_gu, w_down, b_down, g_final, final):
    rows = top.shape[0]
    n_exp = w_gu.shape[1]
    pos, meta = _moe_pos(top)
    counts = meta[0, :n_exp].astype(I32)
    starts = meta[1, :n_exp].astype(I32)
    n_tiles = rows * TOP_K // EXPERT_TILE + n_exp
    tiles_per = (counts + EXPERT_TILE - 1) // EXPERT_TILE
    tile_start = starts // EXPERT_TILE
    tile_end = tile_start + tiles_per
    n_active = tile_end[-1]
    tile = jnp.minimum(jnp.arange(n_tiles, dtype=I32), n_active - 1)
    tile_expert = jnp.sum(tile_end[None, :] <= tile[:, None], axis=1).astype(I32)
    pos_flat = pos[:, :TOP_K].reshape(-1)
    last_tile = jnp.where(tiles_per > 0, tile_end - 1, -1)
    tail_tile = n_active + jnp.arange(n_exp, dtype=I32)
    tail_tile = jnp.where(tail_tile < n_tiles, tail_tile, -1)
    fill_tiles = jnp.concatenate([last_tile, tail_tile])
    fill_rows = jnp.where(fill_tiles >= 0, fill_tiles * EXPERT_TILE, -1).astype(I32)
    xs = _dispatch(pos_flat, fill_rows, h, n_tiles * EXPERT_TILE)
    ids = jnp.arange(n_exp, dtype=I32)
    later = jnp.logical_and(ids[None, :] > ids[:, None], tiles_per[None, :] > 0)
    next_nonempty = jnp.min(jnp.where(later, ids[None, :], n_exp), axis=1)
    next_nonempty = jnp.where(next_nonempty < n_exp, next_nonempty, -1).astype(I32)
    rank_nonempty = jnp.cumsum((tiles_per > 0).astype(I32)) - 1
    owner = tile_expert[:, None] == ids[None, :]
    tile_next = jnp.sum(jnp.where(owner, next_nonempty[None, :], 0), axis=1).astype(I32)
    tile_slot = jnp.sum(jnp.where(owner, rank_nonempty[None, :] % 2, 0), axis=1).astype(I32)
    ys = _experts(layer, tile_expert, tile_next, tile_slot, n_active.reshape(1), xs, w_gu, b_gu, w_down, b_down)
    return _combine(pos_flat, x, gates, modtab, g_final, ys, mod_index, final)


def _rot_cols(w):
    lead = w.shape[:-1]
    blocks = w.reshape(lead + (-1, 2, HEAD // 2))
    return jnp.concatenate([-blocks[..., 1:2, :], blocks[..., 0:1, :]], axis=-2).reshape(w.shape)


def _pair_cols(a, b):
    lead = a.shape[:-1]
    a3 = a.reshape(lead + (-1, HEAD))
    b3 = b.reshape(lead + (-1, HEAD))
    return jnp.concatenate([a3, b3], axis=-1).reshape(lead + (-1,))


def _rope_tables(geo):
    t = jnp.arange(geo.n_lat)
    rowp = (t // GRID_W).astype(F32)
    colp = (t % GRID_W).astype(F32)
    quarter = HEAD // 4
    inv_freq = ROPE_BASE ** (-jnp.arange(quarter, dtype=F32) / quarter)
    ang = jnp.concatenate([rowp[:, None] * inv_freq, colp[:, None] * inv_freq], axis=-1)
    cos = jnp.concatenate([jnp.ones((geo.n_ctx, HEAD // 2), F32), jnp.cos(ang)], axis=0)
    sin = jnp.concatenate([jnp.zeros((geo.n_ctx, HEAD // 2), F32), jnp.sin(ang)], axis=0)
    cos64 = jnp.concatenate([cos, cos], axis=-1)
    sin64 = jnp.concatenate([sin, sin], axis=-1)
    scale = HEAD ** -0.5
    q_tab = jnp.concatenate([cos64, sin64], axis=-1) * scale
    k_cos = jnp.concatenate([cos64, cos64], axis=-1)
    k_sin = jnp.concatenate([sin64, sin64], axis=-1)
    return q_tab, k_cos, k_sin


def _row(v):
    return v.reshape(1, -1).astype(F32)


def kernel(x, c, ctx, c_ctx, ada_w, ada_b, g_mix, g_ffn, g_final, conv_w_pw1, conv_b_pw1, conv_w_dw, conv_b_dw, conv_ln_g, conv_ln_b, conv_w_pw2, conv_b_pw2, ssm_w_in, ssm_w_conv, ssm_b_conv, ssm_a_log, ssm_dt_bias, ssm_d, ssm_norm_g, ssm_w_out, swa_w_qkv, swa_b_qkv, swa_sinks, swa_w_o, swa_b_o, diff_w_qkv, diff_lambda_q1, diff_lambda_k1, diff_lambda_q2, diff_lambda_k2, diff_subln_g, diff_w_o, moe_w_router, moe_b_router, moe_w_gu, moe_b_gu, moe_w_down, moe_b_down):
    batch, n_lat, d = x.shape
    n_ctx = ctx.shape[1]
    depth = ada_w.shape[0]
    geo = _Geo(batch, n_ctx, n_lat)
    n_exp = moe_w_router.shape[-1]
    q_tab, k_cos, k_sin = _rope_tables(geo)

    cond_rows = 16
    assert batch + 1 <= cond_rows
    cond = jnp.zeros((cond_rows, d), F32).at[:batch].set(c).at[batch].set(c_ctx)
    ada = _adaln(cond, ada_w, ada_b)
    mod_lat = ada[:, :batch].reshape(depth, batch, ADA_CHUNKS, d)
    mod_ctx = jnp.broadcast_to(ada[:, batch].reshape(depth, 1, ADA_CHUNKS, d), mod_lat.shape)
    modtabs = jnp.stack([mod_ctx, mod_lat], axis=2).reshape(depth, 2 * batch, ADA_CHUNKS, d)

    xs = jnp.concatenate([ctx, x], axis=1).reshape(geo.rows, d)
    zero_bias = jnp.zeros((1, d), F32)
    out = None
    for i in range(depth):
        kind, j = i % 4, i // 4
        ctx_out = i < depth - 1
        modtab = modtabs[i]
        gain = _row(g_mix[i])
        if kind == 0:
            w1 = _mx(conv_w_pw1[j])
            b1 = _row(conv_b_pw1[j])
            u, = _norm_proj(xs, gain, modtab, geo, [("glu", [w1[:, :d], b1[:, :d], w1[:, d:], b1[:, d:]], F32)],
                            "conv_pw1_glu")
            a = _dwconv(u, conv_w_dw[j], conv_b_dw[j], geo, "ln_silu", [_row(conv_ln_g[j]), _row(conv_ln_b[j])],
                        d, MXU_DTYPE, "conv_dw_ln")
            wo, bo = _mx(conv_w_pw2[j]), _row(conv_b_pw2[j])
        elif kind == 1:
            di = ssm_norm_g.shape[-1]
            heads = ssm_a_log.shape[-1]
            conv_dim = ssm_w_conv.shape[-1]
            w_in = ssm_w_in[j]
            w_dt = w_in[:, di + conv_dim:].reshape(d, 2, heads)
            w_dt = jnp.pad(w_dt, ((0, 0), (0, 0), (0, LANES - heads))).reshape(d, 2 * LANES)
            w_dt_hi = _mx(w_dt)
            w_dt_lo = _mx(w_dt - w_dt_hi.astype(F32))
            z, xbc, dt_raw = _norm_proj(xs, gain, modtab, geo, [
                ("plain", [_mx(w_in[:, :di]), jnp.zeros((1, di), F32)], MXU_DTYPE),
                ("plain", [_mx(w_in[:, di:di + conv_dim]), jnp.zeros((1, conv_dim), F32)], F32),
                ("precise", [w_dt_hi, w_dt_lo, jnp.zeros((1, 2 * LANES), F32)], F32)], "ssm_in")
            tc = 1024 if conv_dim % 1024 == 0 else 512
            xbc = _dwconv(xbc, ssm_w_conv[j], ssm_b_conv[j], geo, "silu", [], tc, F32, "ssm_conv")
            y2 = _ssd(xbc, dt_raw, ssm_dt_bias[j], ssm_a_log[j], geo, di, heads)
            a = _ssm_finish(y2, xbc, z, _row(jnp.repeat(ssm_d[j], HEAD)), _row(ssm_norm_g[j]))
            wo, bo = _mx(ssm_w_out[j]), zero_bias
        elif kind == 2:
            nh = swa_sinks.shape[-1]
            nq, nkv = nh * HEAD, SWA_KV_HEADS * HEAD
            w, b = swa_w_qkv[j], swa_b_qkv[j][None, :]
            wq, wk, wv = w[:, :nq], w[:, nq:nq + nkv], w[:, nq + nkv:]
            bq, bk, bv = b[:, :nq], b[:, nq:nq + nkv], b[:, nq + nkv:]
            uq, kd, vp = _norm_proj(xs, gain, modtab, geo, [
                ("tab1", [_mx(_pair_cols(wq, _rot_cols(wq))), _pair_cols(bq, _rot_cols(bq)), q_tab], MXU_DTYPE),
                ("tab2", [_mx(_pair_cols(wk, wk)), _pair_cols(bk, bk), k_cos,
                          _mx(_pair_cols(_rot_cols(wk), _rot_cols(wk))), _pair_cols(_rot_cols(bk), _rot_cols(bk)),
                          k_sin], MXU_DTYPE),
                ("plain", [_mx(_pair_cols(wv, jnp.zeros_like(wv))), _pair_cols(bv, jnp.zeros_like(bv))],
                 MXU_DTYPE)], "swa_qkv")
            a = _swa_attention(uq, kd, vp, swa_sinks[j].astype(F32), geo, nh)
            wo, bo = _mx(swa_w_o[j]), _row(swa_b_o[j])
        else:
            assert not ctx_out, "differential attention is only built for a layer without context output"
            lambda_init = 0.8 - 0.6 * math.exp(-0.3 * i)
            w = diff_w_qkv[j]
            wq, wk, wv = w[:, :d], w[:, d:2 * d], w[:, 2 * d:]
            zb = jnp.zeros((1, 2 * d), F32)
            uq, kd, v = _norm_proj(xs, gain, modtab, geo, [
                ("tab1", [_mx(_pair_cols(wq, _rot_cols(wq))), zb, q_tab], MXU_DTYPE),
                ("tab2", [_mx(_pair_cols(wk, wk)), zb, k_cos,
                          _mx(_pair_cols(_rot_cols(wk), _rot_cols(wk))), zb, k_sin], MXU_DTYPE),
                ("plain", [_mx(wv), zero_bias], MXU_DTYPE)], "diff_qkv")
            lam_params = jnp.stack([diff_lambda_q1[j], diff_lambda_k1[j], diff_lambda_q2[j],
                                    diff_lambda_k2[j]]).astype(F32)
            a = _diff_attention(uq, kd, v, lam_params, _row(diff_subln_g[j]), geo, d // (2 * HEAD), lambda_init)
            wo, bo = _mx(diff_w_o[j]), zero_bias

        wr = jnp.pad(moe_w_router[i], ((0, 0), (0, LANES - n_exp)))
        wr_hi = _mx(wr)
        wr_lo = _mx(wr - wr_hi.astype(F32))
        br = jnp.pad(_row(moe_b_router[i]), ((0, 0), (0, LANES - n_exp)), constant_values=-1e30)
        if ctx_out:
            x_tile, mod_index = (lambda t: t), geo.mod_all
        else:
            x_tile, mod_index = geo.lat_tile, geo.mod_lat
            if a.shape[0] == geo.rows:
                a = a.reshape(batch, geo.rpb, -1)[:, n_ctx:].reshape(batch * n_lat, -1)
        xs, h, top, gates = _post(xs, a, wo, bo, modtab, _row(g_ffn[i]), wr_hi, wr_lo, br, x_tile, mod_index,
                                  "post_mixer")
        final = i == depth - 1
        xs = _moe(i, xs, h, top, gates, modtab, mod_index, moe_w_gu, moe_b_gu, moe_w_down, moe_b_down,
                  _row(g_final), final)
        if not ctx_out and not final:
            raise NotImplementedError("a layer without context output must be the last layer")
        out = xs
    return out.reshape(batch, n_lat, d)
```
